```python
import math
import jax, jax.numpy as jnp
from jax import lax
import numpy as np

D_MODEL = 2048
BATCH = 4
SEQ = 4096
DEPTH = 1
DEC_BATCH = 128
DEC_SEQ = 8
PAST_LEN = 16384
PAGE_SIZE = 128

NSA_HEADS = 16
NSA_KV_HEADS = 2
NSA_GROUP = NSA_HEADS // NSA_KV_HEADS
HEAD_DIM = 64
CMP_BLOCK = 32
CMP_STRIDE = 16
SEL_BLOCK = 64
N_SEL = 16
WINDOW = 512
MLA_HEADS = 16
QK_NOPE = 64
QK_ROPE = 32
V_DIM = 64
KV_LORA = 256
Q_LORA = 512
ROPE_THETA = 10000.0
N_BUCKETS = 32
MAX_DISTANCE = 128
D_FF = -(-8 * D_MODEL // (3 * 256)) * 256
Q_BLOCK = 128
EPS = 1e-6
NEG = -1e30
FORCE_BONUS = 1e6

NSA_Q_W = NSA_HEADS * HEAD_DIM
NSA_KV_W = 2 * NSA_KV_HEADS * HEAD_DIM
NSA_GATE_W = 3 * NSA_HEADS
IN_WIDTHS = (NSA_Q_W, NSA_KV_W, NSA_KV_W, NSA_KV_W, NSA_GATE_W, Q_LORA, KV_LORA, QK_ROPE, D_MODEL, D_MODEL)
IN_SPLITS = tuple(int(v) for v in np.cumsum(IN_WIDTHS)[:-1])
D_IN = sum(IN_WIDTHS)

kernel_name = 'nsa_mla_gated_hybrid_step'


def rms_norm(x, g):
    xf = x.astype(jnp.float32)
    y = xf * lax.rsqrt(jnp.mean(xf * xf, axis=-1, keepdims=True) + EPS)
    return (y * g.astype(jnp.float32)).astype(x.dtype)


def rope(x, pos):
    half = x.shape[-1] // 2
    inv_freq = ROPE_THETA ** (-jnp.arange(half, dtype=jnp.float32) / half)
    ang = pos.astype(jnp.float32)[:, None] * inv_freq
    ang = ang.reshape((ang.shape[0],) + (1,) * (x.ndim - 3) + (half,))
    cos, sin = jnp.cos(ang), jnp.sin(ang)
    xf = x.astype(jnp.float32)
    x1, x2 = xf[..., :half], xf[..., half:]
    return jnp.concatenate([x1 * cos - x2 * sin, x2 * cos + x1 * sin], axis=-1).astype(x.dtype)


def t5_bucket(dist):
    n = jnp.maximum(dist, 0)
    exact = N_BUCKETS // 2
    log_ratio = jnp.log(jnp.maximum(n, 1).astype(jnp.float32) / exact) / math.log(MAX_DISTANCE / exact)
    large = jnp.minimum(exact + (log_ratio * (N_BUCKETS - exact)).astype(jnp.int32), N_BUCKETS - 1)
    return jnp.where(n < exact, n, large)


def rel_bias(table, dist):
    b = table[t5_bucket(dist)].reshape(dist.shape + (NSA_KV_HEADS, NSA_GROUP))
    return jnp.moveaxis(b, -3, -1)


def masked_softmax(logits, mask):
    l = jnp.where(mask, logits.astype(jnp.float32), NEG)
    return jax.nn.softmax(l, axis=-1) * mask


def to_qblocks(a):
    B, S = a.shape[:2]
    return jnp.moveaxis(a.reshape((B, S // Q_BLOCK, Q_BLOCK) + a.shape[2:]), 1, 0)


def from_qblocks(a):
    nb, B, qb = a.shape[:3]
    return jnp.moveaxis(a, 0, 1).reshape((B, nb * qb) + a.shape[3:])


def mixer_inputs(x, pos, attn_norm, w_in, g_q_nsa, g_k_slc, g_k_win, g_cq, w_uq, g_q_mla, g_ckv, g_krope, w_uk):
    B, T, _ = x.shape
    h = rms_norm(x, attn_norm)
    z = h @ w_in
    (q_n, kv_c, kv_s, kv_w, gate_n, c_q, c_kv, k_r, gate_a, gate_b) = jnp.split(z, IN_SPLITS, axis=-1)
    kv_shape = (B, T, 2, NSA_KV_HEADS, HEAD_DIM)
    q_n = rms_norm(q_n.reshape(B, T, NSA_HEADS, HEAD_DIM), g_q_nsa)
    kv_c = kv_c.reshape(kv_shape)
    kv_s = kv_s.reshape(kv_shape)
    kv_s = jnp.stack([rms_norm(kv_s[:, :, 0], g_k_slc), kv_s[:, :, 1]], axis=2)
    kv_w = kv_w.reshape(kv_shape)
    kv_w = jnp.stack([rms_norm(kv_w[:, :, 0], g_k_win), kv_w[:, :, 1]], axis=2)
    gate_n = jax.nn.sigmoid(gate_n.reshape(B, T, 3, NSA_HEADS))
    q_m = (rms_norm(c_q, g_cq) @ w_uq).reshape(B, T, MLA_HEADS, QK_NOPE + QK_ROPE)
    q_m = rms_norm(q_m, g_q_mla)
    q_abs = jnp.einsum('bthd,hcd->bthc', q_m[..., :QK_NOPE], w_uk)
    q_rope = rope(q_m[..., QK_NOPE:], pos)
    c_kv = rms_norm(c_kv, g_ckv)
    k_rope = rope(rms_norm(k_r, g_krope), pos)
    return q_n, kv_c, kv_s, kv_w, gate_n, q_abs, q_rope, c_kv, k_rope, gate_a, gate_b


def chunk_partials(rows, phi):
    L = rows.shape[-4]
    ch = rows.reshape(rows.shape[:-4] + (L // CMP_STRIDE, CMP_STRIDE) + rows.shape[-3:])
    lo = jnp.einsum('...cjegd,ej->...cegd', ch, phi[:, :CMP_STRIDE])
    hi = jnp.einsum('...cjegd,ej->...cegd', ch, phi[:, CMP_STRIDE:])
    return lo, hi


def compressed_blocks(lo, hi, g_k_cmp):
    hi_next = jnp.concatenate([hi[:, 1:], jnp.zeros_like(hi[:, :1])], axis=1)
    blk = lo + hi_next
    end = jnp.arange(blk.shape[1]) * CMP_STRIDE + CMP_BLOCK - 1
    return rms_norm(blk[:, :, 0], g_k_cmp), blk[:, :, 1], end


def nsa_cmp_sel(q, q_pos, k_cmp, v_cmp, blk_end, fetch, bias_table):
    B, T = q.shape[:2]
    G, R = NSA_KV_HEADS, NSA_GROUP
    scale = HEAD_DIM ** -0.5
    qg = q.reshape(B, T, G, R, HEAD_DIM)
    dist_c = q_pos[:, None] - blk_end[None, :]
    logits_c = jnp.einsum('btgrd,bcgd->btgrc', qg, k_cmp) * scale + rel_bias(bias_table, dist_c)
    p_c = masked_softmax(logits_c, (dist_c >= 0)[:, None, None, :])
    o_cmp = jnp.einsum('btgrc,bcgd->btgrd', p_c.astype(v_cmp.dtype), v_cmp)
    ratio = SEL_BLOCK // CMP_STRIDE
    n_blocks = p_c.shape[-1] // ratio
    p_grp = p_c.sum(axis=3).reshape(B, T, G, n_blocks, ratio)
    prev = jnp.pad(p_grp[..., :-1, -1], ((0, 0), (0, 0), (0, 0), (1, 0)))
    score = p_grp.sum(-1) + prev
    blk = jnp.arange(n_blocks)
    cur = (q_pos // SEL_BLOCK)[:, None]
    forced = (blk == 0) | (blk == cur) | (blk == cur - 1)
    valid = blk * SEL_BLOCK <= q_pos[:, None]
    score = jnp.where(valid[:, None], jnp.where(forced[:, None], score + FORCE_BONUS, score), NEG)
    _, idx = lax.top_k(score, min(N_SEL, n_blocks))
    kv_sel = fetch(idx)
    n = idx.shape[-1]
    s_pos = idx[..., None] * SEL_BLOCK + jnp.arange(SEL_BLOCK)
    dist_s = q_pos[None, :, None, None, None] - s_pos
    bias_gr = bias_table.reshape(N_BUCKETS, G, R).transpose(1, 0, 2)
    bias_s = jnp.moveaxis(bias_gr[jnp.arange(G)[:, None, None], t5_bucket(dist_s)], -1, 3)
    logits_s = jnp.einsum('btgrd,btgnkd->btgrnk', qg, kv_sel[..., 0, :]) * scale + bias_s
    p_s = masked_softmax(logits_s.reshape(B, T, G, R, n * SEL_BLOCK), (dist_s >= 0).reshape(B, T, G, 1, n * SEL_BLOCK))
    v_sel = kv_sel[..., 1, :].reshape(B, T, G, n * SEL_BLOCK, HEAD_DIM)
    o_sel = jnp.einsum('btgrm,btgmd->btgrd', p_s.astype(v_sel.dtype), v_sel)
    return o_cmp.reshape(B, T, NSA_HEADS, HEAD_DIM), o_sel.reshape(B, T, NSA_HEADS, HEAD_DIM)


def window_prompt(q, kv_w, bias_table):
    B, S = q.shape[:2]
    nqb, nkb = S // Q_BLOCK, WINDOW // Q_BLOCK + 1
    padded = jnp.pad(kv_w, ((0, 0), (WINDOW, 0), (0, 0), (0, 0), (0, 0)))
    blocks = padded.reshape(B, nqb + nkb - 1, Q_BLOCK, 2, NSA_KV_HEADS, HEAD_DIM)
    band = blocks[:, jnp.arange(nqb)[:, None] + jnp.arange(nkb)]
    band = band.reshape(B, nqb, nkb * Q_BLOCK, 2, NSA_KV_HEADS, HEAD_DIM)
    qg = q.reshape(B, nqb, Q_BLOCK, NSA_KV_HEADS, NSA_GROUP, HEAD_DIM)
    q_pos = jnp.arange(S).reshape(nqb, Q_BLOCK)
    k_pos = jnp.arange(nqb)[:, None] * Q_BLOCK - WINDOW + jnp.arange(nkb * Q_BLOCK)
    dist = q_pos[:, :, None] - k_pos[:, None, :]
    mask = (dist >= 0) & (dist < WINDOW) & (k_pos[:, None, :] >= 0)
    logits = jnp.einsum('bnqgrd,bnkgd->bnqgrk', qg, band[:, :, :, 0]) * HEAD_DIM ** -0.5 + rel_bias(bias_table, dist)
    p = masked_softmax(logits, mask[:, :, None, None, :])
    o = jnp.einsum('bnqgrk,bnkgd->bnqgrd', p.astype(band.dtype), band[:, :, :, 1])
    return o.reshape(B, S, NSA_HEADS, HEAD_DIM)


def window_attend(q, q_pos, kv_ctx, k_pos, bias_table):
    B, T = q.shape[:2]
    qg = q.reshape(B, T, NSA_KV_HEADS, NSA_GROUP, HEAD_DIM)
    dist = q_pos[:, None] - k_pos[None, :]
    logits = jnp.einsum('btgrd,blgd->btgrl', qg, kv_ctx[:, :, 0]) * HEAD_DIM ** -0.5 + rel_bias(bias_table, dist)
    p = masked_softmax(logits, ((dist >= 0) & (dist < WINDOW))[:, None, None, :])
    o = jnp.einsum('btgrl,blgd->btgrd', p.astype(kv_ctx.dtype), kv_ctx[:, :, 1])
    return o.reshape(B, T, NSA_HEADS, HEAD_DIM)


def mla_attend(q_abs, q_rope, c_kv, k_rope, q_pos, k_pos):
    scale = (QK_NOPE + QK_ROPE) ** -0.5
    logits = (jnp.einsum('bthc,bsc->bhts', q_abs, c_kv) + jnp.einsum('bthr,bsr->bhts', q_rope, k_rope)) * scale
    p = masked_softmax(logits, q_pos[:, None] >= k_pos[None, :])
    return jnp.einsum('bhts,bsc->bthc', p.astype(c_kv.dtype), c_kv)


def layer_step(xp, xs, cache_cmp, cache_slc, cache_ckv, cache_kr, win_state, page_table, bias_table,
               attn_norm, w_in, g_q_nsa, g_k_cmp, g_k_slc, g_k_win, phi_cmp, g_cq, w_uq, g_q_mla, g_ckv, g_krope,
               w_uk, w_uv, w_proj_nsa, w_proj_mla, w_out, ffn_norm, w_gate, w_up, w_down):
    G = NSA_KV_HEADS

    def proj(x, pos):
        return mixer_inputs(x, pos, attn_norm, w_in, g_q_nsa, g_k_slc, g_k_win, g_cq, w_uq, g_q_mla, g_ckv, g_krope, w_uk)

    def nsa_merge(g_n, o_cmp, o_sel, o_win):
        return g_n[:, :, 0, :, None] * o_cmp + g_n[:, :, 1, :, None] * o_sel + g_n[:, :, 2, :, None] * o_win

    def finish(x, o_nsa, o_lat, gate_a, gate_b):
        B, T = x.shape[:2]
        o_mla = jnp.einsum('bthc,hcd->bthd', o_lat, w_uv)
        mix = (jax.nn.sigmoid(gate_a) * (o_nsa.reshape(B, T, -1) @ w_proj_nsa)
               + jax.nn.sigmoid(gate_b) * (o_mla.reshape(B, T, -1) @ w_proj_mla))
        x = x + mix @ w_out
        h = rms_norm(x, ffn_norm)
        return x + (jax.nn.silu(h @ w_gate) * (h @ w_up)) @ w_down

    B, S = xp.shape[:2]
    pos_p = jnp.arange(S)
    q_n, kvc_p, kvs_p, kvw_p, gn_p, qa_p, qr_p, ckv_p, kr_p, ga_p, gb_p = proj(xp, pos_p)
    lo, hi = chunk_partials(kvc_p, phi_cmp)
    kc_p, vc_p, end_p = compressed_blocks(lo, hi, g_k_cmp)
    sel_blocks = kvs_p.reshape(B, S // SEL_BLOCK, SEL_BLOCK, 2, G, HEAD_DIM)
    b_idx_p = jnp.arange(B)[:, None, None, None]
    g_idx = jnp.arange(G)[None, None, :, None]

    def fetch_p(idx):
        return sel_blocks[b_idx_p, idx, :, :, g_idx]

    def nsa_qblock(args):
        qb, qpos = args
        return nsa_cmp_sel(qb, qpos, kc_p, vc_p, end_p, fetch_p, bias_table)

    o_cmp, o_sel = lax.map(nsa_qblock, (to_qblocks(q_n), pos_p.reshape(-1, Q_BLOCK)))
    o_nsa = nsa_merge(gn_p, from_qblocks(o_cmp), from_qblocks(o_sel), window_prompt(q_n, kvw_p, bias_table))

    def mla_qblock(args):
        qa, qr, qpos = args
        return mla_attend(qa, qr, ckv_p, kr_p, qpos, pos_p)

    o_lat = from_qblocks(lax.map(mla_qblock, (to_qblocks(qa_p), to_qblocks(qr_p), pos_p.reshape(-1, Q_BLOCK))))
    yp = finish(xp, o_nsa, o_lat, ga_p, gb_p)
    win_p = kvw_p[:, S - min(WINDOW, S):]

    Bd, T = xs.shape[:2]
    pos_s = PAST_LEN + jnp.arange(T)
    q_n, kvc_s, kvs_s, kvw_s, gn_s, qa_s, qr_s, ckv_s, kr_s, ga_s, gb_s = proj(xs, pos_s)
    t_pad = -(-T // SEL_BLOCK) * SEL_BLOCK

    def pad_rows(a):
        return jnp.pad(a, ((0, 0), (0, t_pad - T), (0, 0), (0, 0), (0, 0)))

    lo_pool, hi_pool = chunk_partials(cache_cmp, phi_cmp)
    lo_new, hi_new = chunk_partials(pad_rows(kvc_s), phi_cmp)
    past_chunks = PAST_LEN // CMP_STRIDE
    lo = jnp.concatenate([lo_pool[page_table].reshape((Bd, past_chunks) + lo_new.shape[2:]), lo_new], axis=1)
    hi = jnp.concatenate([hi_pool[page_table].reshape((Bd, past_chunks) + hi_new.shape[2:]), hi_new], axis=1)
    kc_s, vc_s, end_s = compressed_blocks(lo, hi, g_k_cmp)
    bpp = PAGE_SIZE // SEL_BLOCK
    pool_blocks = cache_slc.reshape((cache_slc.shape[0], bpp, SEL_BLOCK) + cache_slc.shape[2:])
    tail_blocks = pad_rows(kvs_s).reshape(Bd, t_pad // SEL_BLOCK, SEL_BLOCK, 2, G, HEAD_DIM)
    past_blocks = PAST_LEN // SEL_BLOCK
    b_idx_s = jnp.arange(Bd)[:, None, None, None]

    def fetch_s(idx):
        jp = jnp.minimum(idx, past_blocks - 1)
        past = pool_blocks[page_table[b_idx_s, jp // bpp], jp % bpp, :, :, g_idx]
        jt = jnp.clip(idx - past_blocks, 0, tail_blocks.shape[1] - 1)
        tail = tail_blocks[b_idx_s, jt, :, :, g_idx]
        return jnp.where((idx >= past_blocks)[..., None, None, None], tail, past)

    o_cmp, o_sel = nsa_cmp_sel(q_n, pos_s, kc_s, vc_s, end_s, fetch_s, bias_table)
    win_ctx = jnp.concatenate([win_state, kvw_s], axis=1)
    k_pos_w = PAST_LEN - win_state.shape[1] + jnp.arange(win_ctx.shape[1])
    o_nsa = nsa_merge(gn_s, o_cmp, o_sel, window_attend(q_n, pos_s, win_ctx, k_pos_w, bias_table))
    c_ctx = jnp.concatenate([cache_ckv[page_table].reshape(Bd, PAST_LEN, KV_LORA), ckv_s], axis=1)
    kr_ctx = jnp.concatenate([cache_kr[page_table].reshape(Bd, PAST_LEN, QK_ROPE), kr_s], axis=1)
    o_lat = mla_attend(qa_s, qr_s, c_ctx, kr_ctx, pos_s, jnp.arange(PAST_LEN + T))
    ys = finish(xs, o_nsa, o_lat, ga_s, gb_s)
    win_s = win_ctx[:, win_ctx.shape[1] - min(WINDOW, PAST_LEN + T):]
    return (yp, ys, kvc_p, kvc_s, kvs_p, kvs_s, ckv_p, ckv_s, kr_p, kr_s, win_p, win_s)


def setup_inputs(seed: int = 0) -> dict:
    key = jax.random.key(seed)
    keys = iter(jax.random.split(key, 48))

    def nrm(shape, scale=1.0):
        return jax.random.normal(next(keys), shape, jnp.float32) * scale

    def gain(shape):
        return 1.0 + nrm(shape, 0.02)

    L = DEPTH
    n_pages = PAST_LEN // PAGE_SIZE
    n_used = DEC_BATCH * n_pages
    n_pool = n_used + max(1, n_used // 4)
    page_table = jax.random.permutation(next(keys), n_pool)[:n_used].reshape(DEC_BATCH, n_pages).astype(jnp.int32)
    kvh = (2, NSA_KV_HEADS, HEAD_DIM)
    win_len = min(WINDOW, PAST_LEN)
    mla_q_w = MLA_HEADS * (QK_NOPE + QK_ROPE)
    mla_v_w = MLA_HEADS * V_DIM
    return {
        'x_prompt': nrm((BATCH, SEQ, D_MODEL)),
        'x_sample': nrm((DEC_BATCH, DEC_SEQ, D_MODEL)),
        'cache_cmp_kv': nrm((L, n_pool, PAGE_SIZE) + kvh),
        'cache_slc_kv': nrm((L, n_pool, PAGE_SIZE) + kvh),
        'cache_mla_ckv': nrm((L, n_pool, PAGE_SIZE, KV_LORA)),
        'cache_mla_krope': nrm((L, n_pool, PAGE_SIZE, QK_ROPE)),
        'state_win_kv': nrm((L, DEC_BATCH, win_len) + kvh),
        'page_table': page_table,
        'rel_bias_table': nrm((N_BUCKETS, NSA_HEADS), 0.3),
        'attn_norm': gain((L, D_MODEL)),
        'w_in': nrm((L, D_MODEL, D_IN), D_MODEL ** -0.5),
        'g_q_nsa': gain((L, HEAD_DIM)),
        'g_k_cmp': gain((L, HEAD_DIM)),
        'g_k_slc': gain((L, HEAD_DIM)),
        'g_k_win': gain((L, HEAD_DIM)),
        'phi_cmp': (1.0 + nrm((L, 2, CMP_BLOCK), 0.1)) / CMP_BLOCK,
        'g_cq': gain((L, Q_LORA)),
        'w_uq': nrm((L, Q_LORA, mla_q_w), Q_LORA ** -0.5),
        'g_q_mla': gain((L, QK_NOPE + QK_ROPE)),
        'g_ckv': gain((L, KV_LORA)),
        'g_krope': gain((L, QK_ROPE)),
        'w_uk': nrm((L, MLA_HEADS, KV_LORA, QK_NOPE), KV_LORA ** -0.5),
        'w_uv': nrm((L, MLA_HEADS, KV_LORA, V_DIM), KV_LORA ** -0.5),
        'w_proj_nsa': nrm((L, NSA_Q_W, D_MODEL), NSA_Q_W ** -0.5),
        'w_proj_mla': nrm((L, mla_v_w, D_MODEL), mla_v_w ** -0.5),
        'w_out': nrm((L, D_MODEL, D_MODEL), D_MODEL ** -0.5),
        'ffn_norm': gain((L, D_MODEL)),
        'w_gate': nrm((L, D_MODEL, D_FF), D_MODEL ** -0.5),
        'w_up': nrm((L, D_MODEL, D_FF), D_MODEL ** -0.5),
        'w_down': nrm((L, D_FF, D_MODEL), D_FF ** -0.5),
    }


def reference(x_prompt, x_sample, cache_cmp_kv, cache_slc_kv, cache_mla_ckv, cache_mla_krope, state_win_kv,
              page_table, rel_bias_table, attn_norm, w_in, g_q_nsa, g_k_cmp, g_k_slc, g_k_win, phi_cmp, g_cq, w_uq,
              g_q_mla, g_ckv, g_krope, w_uk, w_uv, w_proj_nsa, w_proj_mla, w_out, ffn_norm, w_gate, w_up, w_down):
    xp, xs = x_prompt, x_sample
    per_layer = []
    for l in range(DEPTH):
        xp, xs, *st = layer_step(
            xp, xs, cache_cmp_kv[l], cache_slc_kv[l], cache_mla_ckv[l], cache_mla_krope[l], state_win_kv[l],
            page_table, rel_bias_table, attn_norm[l], w_in[l], g_q_nsa[l], g_k_cmp[l], g_k_slc[l], g_k_win[l],
            phi_cmp[l], g_cq[l], w_uq[l], g_q_mla[l], g_ckv[l], g_krope[l], w_uk[l], w_uv[l], w_proj_nsa[l],
            w_proj_mla[l], w_out[l], ffn_norm[l], w_gate[l], w_up[l], w_down[l])
        per_layer.append(st)
    (cmp_p, cmp_s, slc_p, slc_s, ckv_p, ckv_s, kr_p, kr_s, win_p, win_s) = [jnp.stack(s) for s in zip(*per_layer)]
    return (xp, xs, cmp_p, cmp_s, slc_p, slc_s, ckv_p, ckv_s, kr_p, kr_s, win_p, win_s)
```

```python
import functools
import math

import numpy as np
import jax
import jax.numpy as jnp
from jax import lax
from jax.experimental import pallas as pl
from jax.experimental.pallas import tpu as pltpu

F32 = jnp.float32
BF16 = jnp.bfloat16

NSA_HEADS = 16
NSA_KV_HEADS = 2
NSA_GROUP = NSA_HEADS // NSA_KV_HEADS
HEAD_DIM = 64
CMP_BLOCK = 32
CMP_STRIDE = 16
SEL_BLOCK = 64
N_SEL = 16
WINDOW = 512
MLA_HEADS = 16
QK_NOPE = 64
QK_ROPE = 32
V_DIM = 64
KV_LORA = 256
Q_LORA = 512
ROPE_THETA = 10000.0
N_BUCKETS = 32
MAX_DISTANCE = 128
PAGE_SIZE = 128
Q_BLOCK = 128
EPS = 1e-6
NEG = -1e30
PAD_SCORE = -3e38
FORCE_BONUS = 1e6

LANE = 128
KV_W = 2 * NSA_KV_HEADS * HEAD_DIM
GATE_N_W = 3 * NSA_HEADS
MISC_GATE_OFF = QK_ROPE
VMEM_LIMIT = 56 * 1024 * 1024


def _round_up(a, b):
    return -(-a // b) * b


def _cparams(sem):
    return pltpu.CompilerParams(dimension_semantics=sem, vmem_limit_bytes=VMEM_LIMIT)


def _dot(a, b):
    return jnp.dot(a, b, preferred_element_type=F32)


def _dot_nt(a, b):
    return lax.dot_general(a, b, (((1,), (1,)), ((), ())), preferred_element_type=F32)


def _split3(x):
    hi = x.astype(BF16)
    r = x - hi.astype(F32)
    mid = r.astype(BF16)
    lo = (r - mid.astype(F32)).astype(BF16)
    return hi, mid, lo


def _dot_sel(x, m):
    hi, mid, lo = _split3(x)
    return _dot(hi, m) + _dot(mid, m) + _dot(lo, m)


def _dot_sel_nt(m, x):
    hi, mid, lo = _split3(x)
    return _dot_nt(m, hi) + _dot_nt(m, mid) + _dot_nt(m, lo)


def _seg_rinv(x, e, et, width):
    ss = _dot_sel(x * x, e)
    r = lax.rsqrt(ss / width + EPS)
    return _dot_sel(r, et)


def _t5_bucket_np(dist):
    n = np.maximum(dist, 0)
    exact = N_BUCKETS // 2
    log_ratio = np.log(np.maximum(n, 1).astype(np.float32) / exact) / math.log(MAX_DISTANCE / exact)
    large = np.minimum(exact + (log_ratio * (N_BUCKETS - exact)).astype(np.int32), N_BUCKETS - 1)
    return np.where(n < exact, n, large)


def _seg_indicator(width, seg):
    e = np.zeros((width, LANE), np.float32)
    e[np.arange(width), np.arange(width) // seg] = 1.0
    return e


def _score_matrix(nc, nb_pad):
    ratio = SEL_BLOCK // CMP_STRIDE
    a = np.zeros((nc, nb_pad), np.float32)
    c = np.arange(nc)
    a[c, c // ratio] = 1.0
    cc = c[(c % ratio == ratio - 1) & (c // ratio + 1 < nc // ratio)]
    a[cc, cc // ratio + 1] = 1.0
    return a


def _bias_expand_kernel(tt_ref, oh_ref, o_ref):
    hi, mid, lo = _split3(tt_ref[...])
    oh = oh_ref[...]
    o_ref[...] = _dot(hi, oh) + _dot(mid, oh) + _dot(lo, oh)


def _bias_tables(rel_bias_table, dist_list):
    sizes = [_round_up(d.size, LANE) for d in dist_list]
    tile = 2048
    total = _round_up(sum(sizes), tile)
    onehot = np.zeros((N_BUCKETS, total), np.float32)
    off = 0
    for d, sz in zip(dist_list, sizes):
        b = _t5_bucket_np(d.reshape(-1))
        onehot[b, off + np.arange(d.size)] = 1.0
        off += sz
    out = pl.pallas_call(
        _bias_expand_kernel,
        grid=(total // tile,),
        in_specs=[pl.BlockSpec((NSA_HEADS, N_BUCKETS), lambda i: (0, 0)),
                  pl.BlockSpec((N_BUCKETS, tile), lambda i: (0, i))],
        out_specs=pl.BlockSpec((NSA_HEADS, tile), lambda i: (0, i)),
        out_shape=jax.ShapeDtypeStruct((NSA_HEADS, total), F32),
        compiler_params=_cparams(("arbitrary",)),
        name="bias_expand",
    )(rel_bias_table.T, jnp.asarray(onehot, BF16))
    res, off = [], 0
    for d, sz in zip(dist_list, sizes):
        res.append(out[:, off:off + d.size].reshape((NSA_HEADS,) + d.shape))
        off += sz
    return res


def _z_layout(d_model):
    off = {}
    off["q"] = 0
    off["cq"] = NSA_HEADS * HEAD_DIM
    off["misc"] = off["cq"] + Q_LORA
    pos = _round_up(off["misc"] + LANE, d_model)
    off["ga"] = pos
    off["gb"] = pos + d_model
    pos = _round_up(off["gb"] + d_model, 3 * KV_W)
    off["kv"] = pos
    off["ckv"] = off["kv"] + 3 * KV_W
    off["end"] = off["ckv"] + KV_LORA
    return off


def _proj_kernel(x_ref, g_ref, w_ref, z_ref, h_ref):
    @pl.when(pl.program_id(1) == 0)
    def _():
        x = x_ref[...]
        ms = jnp.mean(x * x, axis=-1, keepdims=True)
        h_ref[...] = (x * lax.rsqrt(ms + EPS) * g_ref[...]).astype(BF16)

    z_ref[...] = _dot(h_ref[...], w_ref[...])


def _proj(x2d, g_norm, w_cat, tn):
    n, d = x2d.shape
    zw = w_cat.shape[1]
    tm = min(512, n)
    return pl.pallas_call(
        _proj_kernel,
        grid=(n // tm, zw // tn),
        in_specs=[pl.BlockSpec((tm, d), lambda i, j: (i, 0)),
                  pl.BlockSpec((1, d), lambda i, j: (0, 0)),
                  pl.BlockSpec((d, tn), lambda i, j: (0, j))],
        out_specs=pl.BlockSpec((tm, tn), lambda i, j: (i, j)),
        out_shape=jax.ShapeDtypeStruct((n, zw), F32),
        scratch_shapes=[pltpu.VMEM((tm, d), BF16)],
        compiler_params=_cparams(("parallel", "arbitrary")),
        name="in_proj",
    )(x2d, g_norm, w_cat)


def _post_kernel(zq_ref, zcq_ref, zkv_ref, zckv_ref, zmisc_ref, cos_ref, sina_ref, sinb_ref,
                 gq_ref, gks_ref, gkw_ref, gcq_ref, wuq_ref, gmn_ref, gmr_ref, gckv_ref, gkr_ref, wuk_ref,
                 e64_ref, e64t_ref, e128_ref, e128t_ref,
                 qn_ref, kvc_ref, kvs_ref, kvw_ref, kvsg_ref, kvwg_ref, ckv_ref, ckvb_ref, misc_ref, krp_ref,
                 qa_ref, qr_ref):
    e64, e64t = e64_ref[...], e64t_ref[...]
    e128, e128t = e128_ref[...], e128t_ref[...]
    nq = NSA_HEADS * HEAD_DIM

    zq = zq_ref[...]
    qn_ref[...] = (zq * _seg_rinv(zq, e64, e64t, HEAD_DIM) * gq_ref[...] * (HEAD_DIM ** -0.5)).astype(BF16)

    zkv = zkv_ref[...]
    kvc_ref[...] = zkv[:, :KV_W]
    half = KV_W // 2
    for idx, (g_ref, out_ref, outg_ref) in enumerate(((gks_ref, kvs_ref, kvsg_ref), (gkw_ref, kvw_ref, kvwg_ref))):
        kv = zkv[:, (idx + 1) * KV_W:(idx + 2) * KV_W]
        k, v = kv[:, :half], kv[:, half:]
        kn = k * _seg_rinv(k, e64[:half], e64t[:, :half], HEAD_DIM) * g_ref[...]
        out_ref[...] = jnp.concatenate([kn, v], axis=1)
        for g in range(NSA_KV_HEADS):
            sl = slice(g * HEAD_DIM, (g + 1) * HEAD_DIM)
            outg_ref[g] = jnp.concatenate([kn[:, sl], v[:, sl]], axis=1).astype(BF16)

    zc = zckv_ref[...]
    ckv = zc * lax.rsqrt(jnp.mean(zc * zc, axis=-1, keepdims=True) + EPS) * gckv_ref[...]
    ckv_ref[...] = ckv
    ckvb_ref[...] = ckv.astype(BF16)

    cos, sina, sinb = cos_ref[...], sina_ref[...], sinb_ref[...]
    zm = zmisc_ref[...]
    lane = lax.broadcasted_iota(jnp.int32, zm.shape, 1)
    is_kr = lane < QK_ROPE
    ms = jnp.sum(jnp.where(is_kr, zm * zm, 0.0), axis=-1, keepdims=True) / QK_ROPE
    krn = zm * lax.rsqrt(ms + EPS) * gkr_ref[...]
    half_r = QK_ROPE // 2
    kr = krn * cos + pltpu.roll(krn, half_r, 1) * sina + pltpu.roll(krn, LANE - half_r, 1) * sinb
    kr = jnp.where(is_kr, kr, 0.0)
    gates = jnp.where(lane < MISC_GATE_OFF + GATE_N_W, jax.nn.sigmoid(zm), 0.0)
    misc_ref[...] = jnp.where(is_kr, kr, gates)
    krp_ref[...] = kr.astype(BF16)

    zcq = zcq_ref[...]
    cqn = (zcq * lax.rsqrt(jnp.mean(zcq * zcq, axis=-1, keepdims=True) + EPS) * gcq_ref[...]).astype(BF16)
    qm = _dot(cqn, wuq_ref[...])
    nope, ropep = qm[:, :nq], qm[:, nq:]
    ss = _dot_sel(nope * nope, e64) + _dot_sel(ropep * ropep, e128)
    r = lax.rsqrt(ss / (QK_NOPE + QK_ROPE) + EPS)
    scale = (QK_NOPE + QK_ROPE) ** -0.5
    nope_n = nope * _dot_sel(r, e64t) * gmn_ref[...] * scale
    rope_n = ropep * _dot_sel(r, e128t) * gmr_ref[...] * scale
    wr = MLA_HEADS * LANE
    cos_t = jnp.concatenate([cos] * MLA_HEADS, axis=1)
    sina_t = jnp.concatenate([sina] * MLA_HEADS, axis=1)
    sinb_t = jnp.concatenate([sinb] * MLA_HEADS, axis=1)
    qr = rope_n * cos_t + pltpu.roll(rope_n, half_r, 1) * sina_t + pltpu.roll(rope_n, wr - half_r, 1) * sinb_t
    qr_ref[...] = qr.astype(BF16)
    nb = nope_n.astype(BF16)
    for h in range(MLA_HEADS):
        qa = _dot_nt(nb[:, h * QK_NOPE:(h + 1) * QK_NOPE], wuk_ref[h])
        qa_ref[:, h * KV_LORA:(h + 1) * KV_LORA] = qa.astype(BF16)


def _post(z, zoff, tabs, prm):
    n = z.shape[0]
    tm = min(256, n)
    nq = NSA_HEADS * HEAD_DIM
    row = lambda w, c: pl.BlockSpec((tm, w), lambda i, c=c: (i, c))
    full = lambda a: pl.BlockSpec(a.shape, lambda i, nd=a.ndim: (0,) * nd)
    cos, sina, sinb = tabs
    consts = [prm["gq"], prm["gks"], prm["gkw"], prm["gcq"], prm["wuq"], prm["gmn"], prm["gmr"], prm["gckv"],
              prm["gkr"], prm["wuk"], prm["e64"], prm["e64t"], prm["e128"], prm["e128t"]]
    in_specs = [row(nq, zoff["q"] // nq), row(Q_LORA, zoff["cq"] // Q_LORA), row(3 * KV_W, zoff["kv"] // (3 * KV_W)),
                row(KV_LORA, zoff["ckv"] // KV_LORA), row(LANE, zoff["misc"] // LANE),
                row(LANE, 0), row(LANE, 0), row(LANE, 0)] + [full(a) for a in consts]
    sds = jax.ShapeDtypeStruct
    out_shape = [sds((n, nq), BF16), sds((n, KV_W), F32), sds((n, KV_W), F32), sds((n, KV_W), F32),
                 sds((NSA_KV_HEADS, n, LANE), BF16), sds((NSA_KV_HEADS, n, LANE), BF16),
                 sds((n, KV_LORA), F32), sds((n, KV_LORA), BF16), sds((n, LANE), F32), sds((n, LANE), BF16),
                 sds((n, MLA_HEADS * KV_LORA), BF16), sds((n, MLA_HEADS * LANE), BF16)]
    grp = pl.BlockSpec((NSA_KV_HEADS, tm, LANE), lambda i: (0, i, 0))
    out_specs = [row(nq, 0), row(KV_W, 0), row(KV_W, 0), row(KV_W, 0), grp, grp,
                 row(KV_LORA, 0), row(KV_LORA, 0), row(LANE, 0), row(LANE, 0),
                 row(MLA_HEADS * KV_LORA, 0), row(MLA_HEADS * LANE, 0)]
    outs = pl.pallas_call(
        _post_kernel, grid=(n // tm,), in_specs=in_specs, out_specs=out_specs, out_shape=out_shape,
        compiler_params=_cparams(("parallel",)), name="post_proj",
    )(z, z, z, z, z, cos, sina, sinb, *consts)
    names = ("qn", "kvc", "kvs", "kvw", "kvsg", "kvwg", "ckv", "ckvb", "misc", "krp", "qa", "qr")
    return dict(zip(names, outs))


def _chunk_partials(rows, wlo, whi):
    ch = rows.reshape(rows.shape[0] // CMP_STRIDE, CMP_STRIDE, rows.shape[1])
    return jnp.sum(ch * wlo[None], axis=1), jnp.sum(ch * whi[None], axis=1)


def _blocks_to_groups(lo, hi, gkc, e64, e64t):
    nr = lo.shape[0]
    rid = lax.broadcasted_iota(jnp.int32, hi.shape, 0)
    hi_next = jnp.where(rid < nr - 1, pltpu.roll(hi, nr - 1, 0), 0.0)
    blk = lo + hi_next
    half = KV_W // 2
    k, v = blk[:, :half], blk[:, half:]
    kn = k * _seg_rinv(k, e64[:half], e64t[:, :half], HEAD_DIM) * gkc
    out = []
    for g in range(NSA_KV_HEADS):
        sl = slice(g * HEAD_DIM, (g + 1) * HEAD_DIM)
        out.append(jnp.concatenate([kn[:, sl], v[:, sl]], axis=1).astype(BF16))
    return out


def _cmp_prompt_kernel(kvc_ref, wlo_ref, whi_ref, gkc_ref, e64_ref, e64t_ref, o_ref):
    lo, hi = _chunk_partials(kvc_ref[0], wlo_ref[...], whi_ref[...])
    groups = _blocks_to_groups(lo, hi, gkc_ref[...], e64_ref[...], e64t_ref[...])
    for g in range(NSA_KV_HEADS):
        o_ref[0, g] = groups[g]


def _cmp_prompt(kvc, prm):
    b, s, _ = kvc.shape
    nc = s // CMP_STRIDE
    full = lambda a: pl.BlockSpec(a.shape, lambda i, nd=a.ndim: (0,) * nd)
    consts = [prm["wlo"], prm["whi"], prm["gkc"], prm["e64"], prm["e64t"]]
    return pl.pallas_call(
        _cmp_prompt_kernel, grid=(b,),
        in_specs=[pl.BlockSpec((1, s, KV_W), lambda i: (i, 0, 0))] + [full(a) for a in consts],
        out_specs=pl.BlockSpec((1, NSA_KV_HEADS, nc, LANE), lambda i: (i, 0, 0, 0)),
        out_shape=jax.ShapeDtypeStruct((b, NSA_KV_HEADS, nc, LANE), BF16),
        compiler_params=_cparams(("parallel",)), name="cmp_blocks_prompt",
    )(kvc, *consts)


def _softmax_update(s, mask, v, m_ref, l_ref, acc_ref):
    m_old = m_ref[...]
    m_new = jnp.maximum(m_old, jnp.max(s, axis=-1, keepdims=True))
    p = jnp.exp(s - m_new)
    if mask is not None:
        p = jnp.where(mask, p, 0.0)
    alpha = jnp.exp(m_old - m_new)
    l_ref[...] = alpha * l_ref[...] + jnp.sum(p, axis=-1, keepdims=True)
    acc_ref[...] = alpha * acc_ref[...] + _dot(p.astype(BF16), v)
    m_ref[...] = m_new


def _softmax_init(m_ref, l_ref, acc_ref):
    m_ref[...] = jnp.full(m_ref.shape, NEG, F32)
    l_ref[...] = jnp.zeros(l_ref.shape, F32)
    acc_ref[...] = jnp.zeros(acc_ref.shape, F32)


def _nsa_prompt_kernel(q_ref, misc_ref, kvc_ref, kvs_ref, kvw_ref, t01_ref, bwin_ref, c31_ref, at_ref, o_ref,
                       m_ref, l_ref, acc_ref, sc_ref):
    i = pl.program_id(1)
    q = q_ref[0]
    misc = misc_ref[0]
    qb = Q_BLOCK
    rows = NSA_GROUP * qb
    nc = kvc_ref.shape[2]
    nb = at_ref.shape[0]
    bw = bwin_ref.shape[1]
    tq2 = lax.broadcasted_iota(jnp.int32, (qb, qb), 0)
    tk2 = lax.broadcasted_iota(jnp.int32, (qb, qb), 1)
    causal = tk2 <= tq2
    anti = tk2 > tq2
    pieces = []
    for g in range(NSA_KV_HEADS):
        h0 = g * NSA_GROUP
        qg = jnp.concatenate([q[:, (h0 + h) * HEAD_DIM:(h0 + h + 1) * HEAD_DIM] for h in range(NSA_GROUP)], axis=0)
        qp = jnp.concatenate([qg, jnp.zeros_like(qg)], axis=1)
        c31 = c31_ref[h0:h0 + NSA_GROUP, 0:1].reshape(NSA_GROUP, 1, 1)

        kvc = kvc_ref[0, g]
        s = _dot_nt(qp, kvc).reshape(NSA_GROUP, qb, nc)
        shift_j = lax.broadcasted_iota(jnp.int32, (bw, nc), 0)
        shift_c = lax.broadcasted_iota(jnp.int32, (bw, nc), 1)
        first = (qb // CMP_STRIDE) * i - (bw - qb // CMP_STRIDE)
        shift = (shift_c - shift_j == first).astype(BF16)
        placed = _dot_sel(bwin_ref[h0 * qb:(h0 + NSA_GROUP) * qb, :], shift).reshape(NSA_GROUP, qb, nc)
        c_id = lax.broadcasted_iota(jnp.int32, (qb, nc), 1)
        t_id = lax.broadcasted_iota(jnp.int32, (qb, nc), 0)
        bias = jnp.where((c_id < first)[None], c31, placed)
        mask_c = (qb * i + t_id - CMP_STRIDE * c_id - (CMP_BLOCK - 1)) >= 0
        s = jnp.where(mask_c[None], s + bias, NEG)
        e = jnp.exp(s - jnp.max(s, axis=-1, keepdims=True))
        p = jnp.where(mask_c[None], e / jnp.sum(e, axis=-1, keepdims=True), 0.0)
        o_cmp = _dot(p.reshape(rows, nc).astype(BF16), kvc)

        pg = jnp.sum(p, axis=0)
        sc = _dot_sel_nt(at_ref[...], pg)
        j_id = lax.broadcasted_iota(jnp.int32, (nb, qb), 0)
        qpos = qb * i + lax.broadcasted_iota(jnp.int32, (nb, qb), 1)
        cur = qpos // SEL_BLOCK
        forced = (j_id == 0) | (j_id == cur) | (j_id == cur - 1)
        valid = j_id * SEL_BLOCK <= qpos
        sc = jnp.where(valid, jnp.where(forced, sc + FORCE_BONUS, sc), NEG)
        sc_ref[...] = sc

        def rank_body(r, rank):
            sr = sc_ref[pl.ds(r, 1), :]
            ahead = (sr > sc) | ((sr == sc) & (r < j_id))
            return rank + ahead.astype(jnp.int32)

        rank = lax.fori_loop(0, nb, rank_body, jnp.zeros((nb, qb), jnp.int32))
        sel = (rank < min(N_SEL, nb)).astype(F32).T.astype(BF16)

        def sel_mask(kt):
            ej = lax.broadcasted_iota(jnp.int32, (nb, qb), 0)
            ec = lax.broadcasted_iota(jnp.int32, (nb, qb), 1)
            expand = (ej == (qb // SEL_BLOCK) * kt + ec // SEL_BLOCK).astype(BF16)
            return _dot(sel, expand) > 0.5

        def tile(kv_ref, kt, bias3, mask):
            kv = kv_ref[g, 0, pl.ds(pl.multiple_of(kt * qb, qb), qb), :]
            st = _dot_nt(qp, kv).reshape(NSA_GROUP, qb, qb) + bias3
            if mask is None:
                _softmax_update(st.reshape(rows, qb), None, kv, m_ref, l_ref, acc_ref)
            else:
                m3 = jnp.broadcast_to(mask[None], st.shape)
                st = jnp.where(m3, st, NEG)
                _softmax_update(st.reshape(rows, qb), m3.reshape(rows, qb), kv, m_ref, l_ref, acc_ref)

        t0 = t01_ref[h0:h0 + NSA_GROUP, 0]
        t1 = t01_ref[h0:h0 + NSA_GROUP, 1]

        _softmax_init(m_ref, l_ref, acc_ref)

        def far_body(kt, carry):
            tile(kvs_ref, kt, c31, sel_mask(kt))
            return carry

        lax.fori_loop(0, jnp.maximum(i - 1, 0), far_body, 0)

        @pl.when(i >= 1)
        def _():
            tile(kvs_ref, i - 1, t1, sel_mask(i - 1))

        tile(kvs_ref, i, t0, sel_mask(i) & causal)
        o_sel = acc_ref[...] / l_ref[...]

        _softmax_init(m_ref, l_ref, acc_ref)
        nwin = WINDOW // qb
        for diff in range(nwin, -1, -1):
            bias3 = t0 if diff == 0 else (t1 if diff == 1 else c31)
            mask = causal if diff == 0 else (anti if diff == nwin else None)
            if diff == 0:
                tile(kvw_ref, i, bias3, mask)
            else:
                @pl.when(i >= diff)
                def _(diff=diff, bias3=bias3, mask=mask):
                    tile(kvw_ref, i - diff, bias3, mask)
        o_win = acc_ref[...] / l_ref[...]

        for h in range(NSA_GROUP):
            hh = h0 + h
            rs = slice(h * qb, (h + 1) * qb)
            gc = misc[:, MISC_GATE_OFF + hh:MISC_GATE_OFF + hh + 1]
            gs = misc[:, MISC_GATE_OFF + NSA_HEADS + hh:MISC_GATE_OFF + NSA_HEADS + hh + 1]
            gw = misc[:, MISC_GATE_OFF + 2 * NSA_HEADS + hh:MISC_GATE_OFF + 2 * NSA_HEADS + hh + 1]
            oh = gc * o_cmp[rs] + gs * o_sel[rs] + gw * o_win[rs]
            pieces.append(oh[:, HEAD_DIM:])
    o_ref[0] = jnp.concatenate(pieces, axis=1).astype(BF16)


def _nsa_prompt(qn, misc, kvcg, kvsg, kvwg, t01, bwin, c31, at, b, s):
    nqb = s // Q_BLOCK
    nq = NSA_HEADS * HEAD_DIM
    nc = s // CMP_STRIDE
    nb = s // SEL_BLOCK
    rows = NSA_GROUP * Q_BLOCK
    full = lambda a: pl.BlockSpec(a.shape, lambda bi, i, nd=a.ndim: (0,) * nd)
    return pl.pallas_call(
        _nsa_prompt_kernel, grid=(b, nqb),
        in_specs=[pl.BlockSpec((1, Q_BLOCK, nq), lambda bi, i: (bi, i, 0)),
                  pl.BlockSpec((1, Q_BLOCK, LANE), lambda bi, i: (bi, i, 0)),
                  pl.BlockSpec((1, NSA_KV_HEADS, nc, LANE), lambda bi, i: (bi, 0, 0, 0)),
                  pl.BlockSpec((NSA_KV_HEADS, 1, s, LANE), lambda bi, i: (0, bi, 0, 0)),
                  pl.BlockSpec((NSA_KV_HEADS, 1, s, LANE), lambda bi, i: (0, bi, 0, 0)),
                  full(t01), full(bwin), full(c31), full(at)],
        out_specs=pl.BlockSpec((1, Q_BLOCK, nq), lambda bi, i: (bi, i, 0)),
        out_shape=jax.ShapeDtypeStruct((b, s, nq), BF16),
        scratch_shapes=[pltpu.VMEM((rows, 1), F32), pltpu.VMEM((rows, 1), F32), pltpu.VMEM((rows, LANE), F32),
                        pltpu.VMEM((nb, Q_BLOCK), F32)],
        compiler_params=_cparams(("parallel", "arbitrary")), name="nsa_prompt",
    )(qn.reshape(b, s, nq), misc.reshape(b, s, LANE), kvcg,
      kvsg.reshape(NSA_KV_HEADS, b, s, LANE), kvwg.reshape(NSA_KV_HEADS, b, s, LANE), t01, bwin, c31, at)


def _mla_prompt_kernel(qa_ref, qr_ref, ckv_ref, krp_ref, wuv_ref, o_ref, m_ref, l_ref, acc_ref, *, tk):
    i = pl.program_id(1)
    qb = Q_BLOCK
    qa = jnp.concatenate([qa_ref[0, :, h * KV_LORA:(h + 1) * KV_LORA] for h in range(MLA_HEADS)], axis=0)
    qr = jnp.concatenate([qr_ref[0, :, h * LANE:(h + 1) * LANE] for h in range(MLA_HEADS)], axis=0)
    rows = MLA_HEADS * qb
    _softmax_init(m_ref, l_ref, acc_ref)

    def tile(kt, masked):
        ks = pl.ds(pl.multiple_of(kt * tk, tk), tk)
        ck = ckv_ref[0, ks, :]
        s = _dot_nt(qa, ck) + _dot_nt(qr, krp_ref[0, ks, :])
        if masked:
            tq = qb * i + lax.broadcasted_iota(jnp.int32, (qb, tk), 0)
            kp = kt * tk + lax.broadcasted_iota(jnp.int32, (qb, tk), 1)
            m3 = jnp.broadcast_to((kp <= tq)[None], (MLA_HEADS, qb, tk))
            s = jnp.where(m3, s.reshape(MLA_HEADS, qb, tk), NEG).reshape(rows, tk)
            _softmax_update(s, m3.reshape(rows, tk), ck, m_ref, l_ref, acc_ref)
        else:
            _softmax_update(s, None, ck, m_ref, l_ref, acc_ref)

    nfull = (i * qb) // tk

    def body(kt, carry):
        tile(kt, False)
        return carry

    lax.fori_loop(0, nfull, body, 0)
    tile(nfull, True)
    o_lat = (acc_ref[...] / l_ref[...]).astype(BF16)
    pieces = [_dot(o_lat[h * qb:(h + 1) * qb], wuv_ref[h]) for h in range(MLA_HEADS)]
    o_ref[0] = jnp.concatenate(pieces, axis=1).astype(BF16)


def _mla_prompt(qa, qr, ckvb, krp, wuv, b, s):
    nqb = s // Q_BLOCK
    tk = min(512, s)
    rows = MLA_HEADS * Q_BLOCK
    wa, wr = MLA_HEADS * KV_LORA, MLA_HEADS * LANE
    return pl.pallas_call(
        functools.partial(_mla_prompt_kernel, tk=tk), grid=(b, nqb),
        in_specs=[pl.BlockSpec((1, Q_BLOCK, wa), lambda bi, i: (bi, i, 0)),
                  pl.BlockSpec((1, Q_BLOCK, wr), lambda bi, i: (bi, i, 0)),
                  pl.BlockSpec((1, s, KV_LORA), lambda bi, i: (bi, 0, 0)),
                  pl.BlockSpec((1, s, LANE), lambda bi, i: (bi, 0, 0)),
                  pl.BlockSpec(wuv.shape, lambda bi, i: (0, 0, 0))],
        out_specs=pl.BlockSpec((1, Q_BLOCK, MLA_HEADS * V_DIM), lambda bi, i: (bi, i, 0)),
        out_shape=jax.ShapeDtypeStruct((b, s, MLA_HEADS * V_DIM), BF16),
        scratch_shapes=[pltpu.VMEM((rows, 1), F32), pltpu.VMEM((rows, 1), F32), pltpu.VMEM((rows, KV_LORA), F32)],
        compiler_params=_cparams(("parallel", "arbitrary")), name="mla_prompt",
    )(qa.reshape(b, s, wa), qr.reshape(b, s, wr), ckvb.reshape(b, s, KV_LORA), krp.reshape(b, s, LANE), wuv)


def _outproj_kernel(x_ref, on_ref, om_ref, ga_ref, gb_ref, wpn_ref, wpm_ref, wo_ref, gf_ref, x1_ref, h2_ref):
    a = _dot(on_ref[...], wpn_ref[...])
    b = _dot(om_ref[...], wpm_ref[...])
    mix = jax.nn.sigmoid(ga_ref[...]) * a + jax.nn.sigmoid(gb_ref[...]) * b
    x1 = x_ref[...] + _dot(mix.astype(BF16), wo_ref[...])
    x1_ref[...] = x1
    ms = jnp.mean(x1 * x1, axis=-1, keepdims=True)
    h2_ref[...] = (x1 * lax.rsqrt(ms + EPS) * gf_ref[...]).astype(BF16)


def _outproj(x2d, o_nsa, o_mla, z, zoff, prm):
    n, d = x2d.shape
    tm = min(256, n)
    row = lambda w, c: pl.BlockSpec((tm, w), lambda i, c=c: (i, c))
    once = lambda a: pl.BlockSpec(a.shape, lambda i, nd=a.ndim: (0,) * nd, pipeline_mode=pl.Buffered(1))
    return pl.pallas_call(
        _outproj_kernel, grid=(n // tm,),
        in_specs=[row(d, 0), row(o_nsa.shape[1], 0), row(o_mla.shape[1], 0), row(d, zoff["ga"] // d),
                  row(d, zoff["gb"] // d), once(prm["wpn"]), once(prm["wpm"]), once(prm["wo"]), once(prm["gf"])],
        out_specs=[row(d, 0), row(d, 0)],
        out_shape=[jax.ShapeDtypeStruct((n, d), F32), jax.ShapeDtypeStruct((n, d), BF16)],
        compiler_params=_cparams(("parallel",)), name="out_proj",
    )(x2d, o_nsa, o_mla, z, z, prm["wpn"], prm["wpm"], prm["wo"], prm["gf"])


def _ffn_kernel(h_ref, x1_ref, wg_ref, wu_ref, wd_ref, y_ref, acc_ref):
    f = pl.program_id(1)

    @pl.when(f == 0)
    def _():
        acc_ref[...] = jnp.zeros(acc_ref.shape, F32)

    h = h_ref[...]
    t = jax.nn.silu(_dot(h, wg_ref[...])) * _dot(h, wu_ref[...])
    acc_ref[...] += _dot(t.astype(BF16), wd_ref[...])

    @pl.when(f == pl.num_programs(1) - 1)
    def _():
        y_ref[...] = x1_ref[...] + acc_ref[...]


def _ffn(h2, x1, prm):
    n, d = x1.shape
    dff = prm["wg"].shape[1]
    tm = min(512, n)
    tf = 512 if dff % 512 == 0 else 256
    return pl.pallas_call(
        _ffn_kernel, grid=(n // tm, dff // tf),
        in_specs=[pl.BlockSpec((tm, d), lambda i, f: (i, 0)), pl.BlockSpec((tm, d), lambda i, f: (i, 0)),
                  pl.BlockSpec((d, tf), lambda i, f: (0, f)), pl.BlockSpec((d, tf), lambda i, f: (0, f)),
                  pl.BlockSpec((tf, d), lambda i, f: (f, 0))],
        out_specs=pl.BlockSpec((tm, d), lambda i, f: (i, 0)),
        out_shape=jax.ShapeDtypeStruct((n, d), F32),
        scratch_shapes=[pltpu.VMEM((tm, d), F32)],
        compiler_params=_cparams(("parallel", "arbitrary")), name="ffn",
    )(h2, x1, prm["wg"], prm["wu"], prm["wd"])


def _page_specs(pp, width):
    return [pl.BlockSpec((1, PAGE_SIZE, width), functools.partial(lambda b, s, pt, j: (pt[b, s * pp + j], 0, 0), j=j))
            for j in range(pp)]


def _group_queries(q, lanes):
    t = q.shape[0]
    blocks = []
    for g in range(NSA_KV_HEADS):
        qg = jnp.concatenate([q[:, (g * NSA_GROUP + h) * HEAD_DIM:(g * NSA_GROUP + h + 1) * HEAD_DIM]
                              for h in range(NSA_GROUP)], axis=0)
        parts = []
        if g > 0:
            parts.append(jnp.zeros((NSA_GROUP * t, g * HEAD_DIM), q.dtype))
        parts.append(qg)
        parts.append(jnp.zeros((NSA_GROUP * t, lanes - (g + 1) * HEAD_DIM), q.dtype))
        blocks.append(jnp.concatenate(parts, axis=1))
    return jnp.concatenate(blocks, axis=0)


def _smp_cmp_kernel(pt_ref, *refs, pp, past_chunks, nsb):
    pages = refs[:pp]
    (new_ref, q_ref, wlo_ref, whi_ref, gkc_ref, e64_ref, e64t_ref, bias_ref, a_ref,
     ocmp_ref, score_ref, lo_ref, hi_ref) = refs[pp:]
    s = pl.program_id(1)
    nr = lo_ref.shape[0]
    t = q_ref.shape[0]
    wlo, whi = wlo_ref[...], whi_ref[...]
    cpp = PAGE_SIZE // CMP_STRIDE

    @pl.when(s == 0)
    def _():
        lo_ref[past_chunks:, :] = jnp.zeros((nr - past_chunks, KV_W), F32)
        hi_ref[past_chunks:, :] = jnp.zeros((nr - past_chunks, KV_W), F32)

    for j in range(pp):
        lo, hi = _chunk_partials(pages[j][0], wlo, whi)
        r0 = pl.multiple_of((s * pp + j) * cpp, cpp)
        lo_ref[pl.ds(r0, cpp), :] = lo
        hi_ref[pl.ds(r0, cpp), :] = hi

    @pl.when(s == pl.num_programs(1) - 1)
    def _():
        new = jnp.concatenate([new_ref[0], jnp.zeros((CMP_STRIDE - t, KV_W), F32)], axis=0)
        lo_n, hi_n = _chunk_partials(new, wlo, whi)
        pad = jnp.zeros((cpp - 1, KV_W), F32)
        lo_ref[past_chunks:past_chunks + cpp, :] = jnp.concatenate([lo_n, pad], axis=0)
        hi_ref[past_chunks:past_chunks + cpp, :] = jnp.concatenate([hi_n, pad], axis=0)
        groups = _blocks_to_groups(lo_ref[...], hi_ref[...], gkc_ref[...], e64_ref[...], e64t_ref[...])
        q = q_ref[...]
        nbp = a_ref.shape[1]
        past = past_chunks * CMP_STRIDE
        c_id = lax.broadcasted_iota(jnp.int32, (t, nr), 1)
        t_id = lax.broadcasted_iota(jnp.int32, (t, nr), 0)
        mask_c = (past + t_id - CMP_STRIDE * c_id - (CMP_BLOCK - 1)) >= 0
        j_id = lax.broadcasted_iota(jnp.int32, (t, nbp), 1)
        qpos = past + lax.broadcasted_iota(jnp.int32, (t, nbp), 0)
        cur = qpos // SEL_BLOCK
        forced = (j_id == 0) | (j_id == cur) | (j_id == cur - 1)
        valid = j_id * SEL_BLOCK <= qpos
        for g in range(NSA_KV_HEADS):
            h0 = g * NSA_GROUP
            qg = jnp.concatenate([q[:, (h0 + h) * HEAD_DIM:(h0 + h + 1) * HEAD_DIM] for h in range(NSA_GROUP)], axis=0)
            qp = jnp.concatenate([qg, jnp.zeros_like(qg)], axis=1)
            kv = groups[g]
            sc = _dot_nt(qp, kv).reshape(NSA_GROUP, t, nr) + bias_ref[h0:h0 + NSA_GROUP]
            sc = jnp.where(mask_c[None], sc, NEG)
            e = jnp.exp(sc - jnp.max(sc, axis=-1, keepdims=True))
            p = jnp.where(mask_c[None], e / jnp.sum(e, axis=-1, keepdims=True), 0.0)
            ocmp_ref[0, g * NSA_GROUP * t:(g + 1) * NSA_GROUP * t, :] = _dot(
                p.reshape(NSA_GROUP * t, nr).astype(BF16), kv)
            score = _dot_sel(jnp.sum(p, axis=0), a_ref[...])
            score = jnp.where(valid, jnp.where(forced, score + FORCE_BONUS, score), NEG)
            score_ref[0, g * t:(g + 1) * t, :] = jnp.where(j_id < nsb, score, PAD_SCORE)


def _smp_cmp(page_table, cache_cmp, kvc_new, qn, bias_c, a_mat, prm, pp):
    bd, n_pages = page_table.shape
    t = kvc_new.shape[1]
    past_chunks = n_pages * PAGE_SIZE // CMP_STRIDE
    nr = a_mat.shape[0]
    nbp = a_mat.shape[1]
    nsb = (past_chunks + SEL_BLOCK // CMP_STRIDE) // (SEL_BLOCK // CMP_STRIDE)
    rows = NSA_HEADS * t
    full = lambda a: pl.BlockSpec(a.shape, lambda b, s, pt, nd=a.ndim: (0,) * nd)
    consts = [prm["wlo"], prm["whi"], prm["gkc"], prm["e64"], prm["e64t"], bias_c, a_mat]
    gs = pltpu.PrefetchScalarGridSpec(
        num_scalar_prefetch=1, grid=(bd, n_pages // pp),
        in_specs=_page_specs(pp, KV_W) + [pl.BlockSpec((1, t, KV_W), lambda b, s, pt: (b, 0, 0)),
                                          pl.BlockSpec((t, qn.shape[1]), lambda b, s, pt: (b, 0))]
        + [full(a) for a in consts],
        out_specs=[pl.BlockSpec((1, rows, LANE), lambda b, s, pt: (b, 0, 0)),
                   pl.BlockSpec((1, NSA_KV_HEADS * t, nbp), lambda b, s, pt: (b, 0, 0))],
        scratch_shapes=[pltpu.VMEM((nr, KV_W), F32), pltpu.VMEM((nr, KV_W), F32)])
    return pl.pallas_call(
        functools.partial(_smp_cmp_kernel, pp=pp, past_chunks=past_chunks, nsb=nsb), grid_spec=gs,
        out_shape=[jax.ShapeDtypeStruct((bd, rows, LANE), F32),
                   jax.ShapeDtypeStruct((bd, NSA_KV_HEADS * t, nbp), F32)],
        compiler_params=_cparams(("parallel", "arbitrary")), name="sample_cmp",
    )(page_table, *([cache_cmp] * pp), kvc_new, qn, *consts)


def _topk_mask_kernel(score_ref, exp_ref, o_ref, sel_ref, sc_ref, *, nsb):
    @pl.when(pl.program_id(0) == 0)
    def _():
        sc = score_ref[...].T
        sc_ref[...] = sc
        j_id = lax.broadcasted_iota(jnp.int32, sc.shape, 0)

        def rank_body(r, rank):
            sr = sc_ref[pl.ds(r, 1), :]
            ahead = (sr > sc) | ((sr == sc) & (r < j_id))
            return rank + ahead.astype(jnp.int32)

        rank = lax.fori_loop(0, nsb, rank_body, jnp.zeros(sc.shape, jnp.int32))
        sel_ref[...] = ((rank < min(N_SEL, nsb)) & (j_id < nsb)).astype(F32).T.astype(BF16)

    rows = sel_ref.shape[0]
    rc = min(512, rows)
    for r in range(0, rows, rc):
        o_ref[r:r + rc, :] = _dot(sel_ref[r:r + rc, :], exp_ref[...]).astype(BF16)


def _topk_mask(score2d, expand, nsb, tile):
    rows, nbp = score2d.shape
    keys = expand.shape[1]
    return pl.pallas_call(
        functools.partial(_topk_mask_kernel, nsb=nsb), grid=(keys // tile,),
        in_specs=[pl.BlockSpec((rows, nbp), lambda k: (0, 0)), pl.BlockSpec((nbp, tile), lambda k: (0, k))],
        out_specs=pl.BlockSpec((rows, tile), lambda k: (0, k)),
        out_shape=jax.ShapeDtypeStruct((rows, keys), BF16),
        scratch_shapes=[pltpu.VMEM((rows, nbp), BF16), pltpu.VMEM((nbp, rows), F32)],
        compiler_params=_cparams(("arbitrary",)), name="sample_topk_mask",
    )(score2d, expand)


def _rows_pad(a, rows):
    return jnp.concatenate([a, jnp.zeros((rows - a.shape[0], a.shape[1]), a.dtype)], axis=0)


def _smp_sel_kernel(pt_ref, *refs, pp):
    pages = refs[:pp]
    (mask_ref, maskt_ref, q_ref, new_ref, c31_ref, blast_ref, bnew_ref, o_ref, qp_ref, m_ref, l_ref, acc_ref) = refs[pp:]
    s = pl.program_id(1)
    t = q_ref.shape[0]
    rows = NSA_HEADS * t
    last = pl.num_programs(1) - 1

    @pl.when(s == 0)
    def _():
        qp_ref[...] = _group_queries(q_ref[...], KV_W)
        _softmax_init(m_ref, l_ref, acc_ref)

    qp = qp_ref[...]

    def expand_mask(mk, nk):
        m4 = jnp.broadcast_to(mk.astype(F32).reshape(NSA_KV_HEADS, 1, t, nk), (NSA_KV_HEADS, NSA_GROUP, t, nk))
        return m4.reshape(rows, nk) > 0.5

    keys = jnp.concatenate([pages[j][0] for j in range(pp)], axis=0).astype(BF16)
    nk = keys.shape[0]
    raw = _dot_nt(qp, keys)
    mask = expand_mask(mask_ref[0], nk)
    nl = blast_ref.shape[1]

    @pl.when(s != last)
    def _():
        st = jnp.where(mask, raw + c31_ref[...], NEG)
        _softmax_update(st, mask, keys, m_ref, l_ref, acc_ref)

    @pl.when(s == last)
    def _():
        bias = jnp.concatenate([jnp.broadcast_to(c31_ref[...], (rows, nk - nl)), blast_ref[...]], axis=1)
        st = jnp.where(mask, raw + bias, NEG)
        _softmax_update(st, mask, keys, m_ref, l_ref, acc_ref)
        kn = _rows_pad(new_ref[0], LANE).astype(BF16)
        j_id = lax.broadcasted_iota(jnp.int32, (rows, LANE), 1)
        t_id = lax.broadcasted_iota(jnp.int32, (rows, LANE), 0) % t
        mt = expand_mask(maskt_ref[0][:, :LANE], LANE) & (j_id <= t_id) & (j_id < t)
        st = jnp.where(mt, _dot_nt(qp, kn) + bnew_ref[...], NEG)
        _softmax_update(st, mt, kn, m_ref, l_ref, acc_ref)
        o = acc_ref[...] / l_ref[...]
        half = KV_W // 2
        hr = NSA_GROUP * t
        o_ref[0] = jnp.concatenate([o[g * hr:(g + 1) * hr, half + g * HEAD_DIM:half + (g + 1) * HEAD_DIM]
                                    for g in range(NSA_KV_HEADS)], axis=0)


def _smp_sel(page_table, cache_slc, mask3, qn, kvs_new, c31r, blast, bnew, pp):
    bd, n_pages = page_table.shape
    t = kvs_new.shape[1]
    rows = NSA_HEADS * t
    nk = pp * PAGE_SIZE
    nsteps = n_pages // pp
    full = lambda a: pl.BlockSpec(a.shape, lambda b, s, pt, nd=a.ndim: (0,) * nd)
    gs = pltpu.PrefetchScalarGridSpec(
        num_scalar_prefetch=1, grid=(bd, nsteps),
        in_specs=_page_specs(pp, KV_W) + [
            pl.BlockSpec((1, NSA_KV_HEADS * t, nk), lambda b, s, pt: (b, 0, s)),
            pl.BlockSpec((1, NSA_KV_HEADS * t, nk), lambda b, s, pt: (b, 0, nsteps)),
            pl.BlockSpec((t, qn.shape[1]), lambda b, s, pt: (b, 0)),
            pl.BlockSpec((1, t, KV_W), lambda b, s, pt: (b, 0, 0)),
            full(c31r), full(blast), full(bnew)],
        out_specs=pl.BlockSpec((1, rows, HEAD_DIM), lambda b, s, pt: (b, 0, 0)),
        scratch_shapes=[pltpu.VMEM((rows, KV_W), BF16), pltpu.VMEM((rows, 1), F32), pltpu.VMEM((rows, 1), F32),
                        pltpu.VMEM((rows, KV_W), F32)])
    return pl.pallas_call(
        functools.partial(_smp_sel_kernel, pp=pp), grid_spec=gs,
        out_shape=jax.ShapeDtypeStruct((bd, rows, HEAD_DIM), F32),
        compiler_params=_cparams(("parallel", "arbitrary")), name="sample_sel",
    )(page_table, *([cache_slc] * pp), mask3, mask3, qn, kvs_new, c31r, blast, bnew)


def _smp_mla_kernel(pt_ref, *refs, pp):
    cpages = refs[:pp]
    rpages = refs[pp:2 * pp]
    (qa_ref, qr_ref, cnew_ref, rnew_ref, wuv_ref, o_ref, qa_s, qr_s, m_ref, l_ref, acc_ref) = refs[2 * pp:]
    s = pl.program_id(1)
    t = qa_ref.shape[0]
    rows = MLA_HEADS * t

    @pl.when(s == 0)
    def _():
        qa_s[...] = jnp.concatenate([qa_ref[:, h * KV_LORA:(h + 1) * KV_LORA] for h in range(MLA_HEADS)], axis=0)
        qr_s[...] = jnp.concatenate([qr_ref[:, h * LANE:(h + 1) * LANE] for h in range(MLA_HEADS)], axis=0)
        _softmax_init(m_ref, l_ref, acc_ref)

    qa, qr = qa_s[...], qr_s[...]
    ck = jnp.concatenate([cpages[j][0] for j in range(pp)], axis=0).astype(BF16)
    kr = jnp.concatenate([rpages[j][0] for j in range(pp)], axis=0).astype(BF16)
    st = _dot_nt(qa, ck) + _dot_nt(qr[:, :QK_ROPE], kr)
    _softmax_update(st, None, ck, m_ref, l_ref, acc_ref)

    @pl.when(s == pl.num_programs(1) - 1)
    def _():
        cn = _rows_pad(cnew_ref[0], LANE)
        rn = _rows_pad(rnew_ref[0], LANE)
        j_id = lax.broadcasted_iota(jnp.int32, (rows, LANE), 1)
        t_id = lax.broadcasted_iota(jnp.int32, (rows, LANE), 0) % t
        mt = (j_id <= t_id) & (j_id < t)
        sn = jnp.where(mt, _dot_nt(qa, cn) + _dot_nt(qr, rn), NEG)
        _softmax_update(sn, mt, cn, m_ref, l_ref, acc_ref)
        o_lat = (acc_ref[...] / l_ref[...]).astype(BF16)
        o_ref[...] = jnp.concatenate([_dot(o_lat[h * t:(h + 1) * t], wuv_ref[h]) for h in range(MLA_HEADS)],
                                     axis=1).astype(BF16)


def _smp_mla(page_table, cache_ckv, cache_kr, qa, qr, ckvb_new, krp_new, wuv, pp):
    bd, n_pages = page_table.shape
    t = ckvb_new.shape[1]
    rows = MLA_HEADS * t
    gs = pltpu.PrefetchScalarGridSpec(
        num_scalar_prefetch=1, grid=(bd, n_pages // pp),
        in_specs=_page_specs(pp, KV_LORA) + _page_specs(pp, QK_ROPE) + [
            pl.BlockSpec((t, qa.shape[1]), lambda b, s, pt: (b, 0)),
            pl.BlockSpec((t, qr.shape[1]), lambda b, s, pt: (b, 0)),
            pl.BlockSpec((1, t, KV_LORA), lambda b, s, pt: (b, 0, 0)),
            pl.BlockSpec((1, t, LANE), lambda b, s, pt: (b, 0, 0)),
            pl.BlockSpec(wuv.shape, lambda b, s, pt: (0, 0, 0))],
        out_specs=pl.BlockSpec((t, MLA_HEADS * V_DIM), lambda b, s, pt: (b, 0)),
        scratch_shapes=[pltpu.VMEM((rows, KV_LORA), BF16), pltpu.VMEM((rows, LANE), BF16),
                        pltpu.VMEM((rows, 1), F32), pltpu.VMEM((rows, 1), F32), pltpu.VMEM((rows, KV_LORA), F32)])
    return pl.pallas_call(
        functools.partial(_smp_mla_kernel, pp=pp), grid_spec=gs,
        out_shape=jax.ShapeDtypeStruct((bd * t, MLA_HEADS * V_DIM), BF16),
        compiler_params=_cparams(("parallel", "arbitrary")), name="sample_mla",
    )(page_table, *([cache_ckv] * pp), *([cache_kr] * pp), qa, qr, ckvb_new, krp_new, wuv)


def _smp_win_kernel(state_ref, new_ref, q_ref, misc_ref, ocmp_ref, osel_ref, bias_ref, o_ref, win_ref):
    t = q_ref.shape[0]
    wl = state_ref.shape[1]
    rows = NSA_HEADS * t
    state = state_ref[0]
    new = new_ref[0]
    win_ref[0, :wl - t, :] = state[t:, :]
    win_ref[0, wl - t:, :] = new
    nkp = bias_ref.shape[1]
    ctx = jnp.concatenate([state, new, jnp.zeros((nkp - wl - t, KV_W), F32)], axis=0).astype(BF16)
    qp = _group_queries(q_ref[...], KV_W)
    j_id = lax.broadcasted_iota(jnp.int32, (rows, nkp), 1)
    t_id = lax.broadcasted_iota(jnp.int32, (rows, nkp), 0) % t
    dist = wl + t_id - j_id
    mask = (dist >= 0) & (dist < WINDOW) & (j_id < wl + t)
    s = jnp.where(mask, _dot_nt(qp, ctx) + bias_ref[...], NEG)
    e = jnp.exp(s - jnp.max(s, axis=-1, keepdims=True))
    p = jnp.where(mask, e / jnp.sum(e, axis=-1, keepdims=True), 0.0)
    o_win = _dot(p.astype(BF16), ctx)
    misc = misc_ref[...]
    ocmp = ocmp_ref[0]
    osel = osel_ref[0]
    half = KV_W // 2
    pieces = []
    for hh in range(NSA_HEADS):
        g = hh // NSA_GROUP
        rs = slice(hh * t, (hh + 1) * t)
        gc = misc[:, MISC_GATE_OFF + hh:MISC_GATE_OFF + hh + 1]
        gsel = misc[:, MISC_GATE_OFF + NSA_HEADS + hh:MISC_GATE_OFF + NSA_HEADS + hh + 1]
        gw = misc[:, MISC_GATE_OFF + 2 * NSA_HEADS + hh:MISC_GATE_OFF + 2 * NSA_HEADS + hh + 1]
        pieces.append(gc * ocmp[rs, HEAD_DIM:] + gsel * osel[rs, :]
                      + gw * o_win[rs, half + g * HEAD_DIM:half + (g + 1) * HEAD_DIM])
    o_ref[...] = jnp.concatenate(pieces, axis=1).astype(BF16)


def _smp_win(state, kvw_new, qn, misc, ocmp, osel, bias_w):
    bd, wl, _ = state.shape
    t = kvw_new.shape[1]
    rows = NSA_HEADS * t
    nq = NSA_HEADS * HEAD_DIM
    return pl.pallas_call(
        _smp_win_kernel, grid=(bd,),
        in_specs=[pl.BlockSpec((1, wl, KV_W), lambda b: (b, 0, 0)), pl.BlockSpec((1, t, KV_W), lambda b: (b, 0, 0)),
                  pl.BlockSpec((t, nq), lambda b: (b, 0)), pl.BlockSpec((t, LANE), lambda b: (b, 0)),
                  pl.BlockSpec((1, rows, LANE), lambda b: (b, 0, 0)), pl.BlockSpec((1, rows, HEAD_DIM), lambda b: (b, 0, 0)),
                  pl.BlockSpec(bias_w.shape, lambda b: (0, 0))],
        out_specs=[pl.BlockSpec((t, nq), lambda b: (b, 0)), pl.BlockSpec((1, wl, KV_W), lambda b: (b, 0, 0))],
        out_shape=[jax.ShapeDtypeStruct((bd * t, nq), BF16), jax.ShapeDtypeStruct((bd, wl, KV_W), F32)],
        compiler_params=_cparams(("parallel",)), name="sample_window_merge",
    )(state, kvw_new, qn, misc, ocmp, osel, bias_w)


def _prep_params(d, attn_norm, w_in, g_q_nsa, g_k_cmp, g_k_slc, g_k_win, phi_cmp, g_cq, w_uq, g_q_mla, g_ckv,
                 g_krope, w_uk, w_uv, w_proj_nsa, w_proj_mla, w_out, ffn_norm, w_gate, w_up, w_down):
    nq = NSA_HEADS * HEAD_DIM
    widths = (nq, KV_W, KV_W, KV_W, GATE_N_W, Q_LORA, KV_LORA, QK_ROPE, d, d)
    names = ("q", "kvc", "kvs", "kvw", "gn", "cq", "ckv", "kr", "ga", "gb")
    splits = np.cumsum(widths)[:-1]
    seg = dict(zip(names, jnp.split(w_in, splits, axis=1)))
    zoff = _z_layout(d)
    tn = 1024 if d >= 1024 else 256
    zw = _round_up(zoff["end"], tn)
    zc = lambda w: jnp.zeros((d, w), w_in.dtype)
    cols = [seg["q"], seg["cq"], seg["kr"], seg["gn"], zc(zoff["ga"] - zoff["misc"] - QK_ROPE - GATE_N_W),
            seg["ga"], seg["gb"], zc(zoff["kv"] - zoff["gb"] - d), seg["kvc"], seg["kvs"], seg["kvw"], seg["ckv"],
            zc(zw - zoff["end"])]
    prm = {"w_cat": jnp.concatenate(cols, axis=1).astype(BF16), "tn": tn}
    prm["g_attn"] = attn_norm.reshape(1, d)
    prm["gq"] = jnp.tile(g_q_nsa, NSA_HEADS).reshape(1, nq)
    prm["gks"] = jnp.tile(g_k_slc, NSA_KV_HEADS).reshape(1, LANE)
    prm["gkw"] = jnp.tile(g_k_win, NSA_KV_HEADS).reshape(1, LANE)
    prm["gkc"] = jnp.tile(g_k_cmp, NSA_KV_HEADS).reshape(1, LANE)
    prm["gcq"] = g_cq.reshape(1, Q_LORA)
    wq = w_uq.reshape(Q_LORA, MLA_HEADS, QK_NOPE + QK_ROPE)
    w_rope = jnp.pad(wq[:, :, QK_NOPE:], ((0, 0), (0, 0), (0, LANE - QK_ROPE)))
    prm["wuq"] = jnp.concatenate([wq[:, :, :QK_NOPE].reshape(Q_LORA, MLA_HEADS * QK_NOPE),
                                  w_rope.reshape(Q_LORA, MLA_HEADS * LANE)], axis=1).astype(BF16)
    prm["gmn"] = jnp.tile(g_q_mla[:QK_NOPE], MLA_HEADS).reshape(1, MLA_HEADS * QK_NOPE)
    prm["gmr"] = jnp.tile(jnp.pad(g_q_mla[QK_NOPE:], (0, LANE - QK_ROPE)), MLA_HEADS).reshape(1, MLA_HEADS * LANE)
    prm["gckv"] = g_ckv.reshape(1, KV_LORA)
    prm["gkr"] = jnp.pad(g_krope, (0, LANE - QK_ROPE)).reshape(1, LANE)
    prm["wuk"] = w_uk.astype(BF16)
    prm["wuv"] = w_uv.astype(BF16)
    e64 = _seg_indicator(nq, HEAD_DIM)
    e128 = _seg_indicator(MLA_HEADS * LANE, LANE)
    prm["e64"], prm["e64t"] = jnp.asarray(e64, BF16), jnp.asarray(e64.T, BF16)
    prm["e128"], prm["e128t"] = jnp.asarray(e128, BF16), jnp.asarray(e128.T, BF16)
    ph = jnp.repeat(phi_cmp, KV_W // 2, axis=0).T
    prm["wlo"], prm["whi"] = ph[:CMP_STRIDE], ph[CMP_STRIDE:]
    prm["wpn"], prm["wpm"], prm["wo"] = w_proj_nsa.astype(BF16), w_proj_mla.astype(BF16), w_out.astype(BF16)
    prm["gf"] = ffn_norm.reshape(1, d)
    prm["wg"], prm["wu"], prm["wd"] = w_gate.astype(BF16), w_up.astype(BF16), w_down.astype(BF16)
    return prm, zoff


def _rope_tables(pos):
    half = QK_ROPE // 2
    inv_freq = ROPE_THETA ** (-jnp.arange(half, dtype=jnp.float32) / half)
    ang = pos.astype(jnp.float32)[:, None] * inv_freq
    cos, sin = jnp.cos(ang), jnp.sin(ang)
    z = jnp.zeros((pos.shape[0], LANE - QK_ROPE), F32)
    zh = jnp.zeros_like(sin)
    return (jnp.concatenate([cos, cos, z], axis=1), jnp.concatenate([zh, sin, z], axis=1),
            jnp.concatenate([-sin, zh, z], axis=1))


def _layer(xp, xs, cache_cmp, cache_slc, cache_ckv, cache_kr, win_state, page_table, rel_bias_table, weights):
    b, s, d = xp.shape
    bd, t, _ = xs.shape
    n_pages = page_table.shape[1]
    past = n_pages * PAGE_SIZE
    n_pool = cache_cmp.shape[0]
    wl = win_state.shape[1]
    prm, zoff = _prep_params(d, *weights)
    pp = max(1, min(16, n_pages // 2))
    nsteps = n_pages // pp

    qb = Q_BLOCK
    ar = np.arange
    d_t01 = (ar(qb)[None, :, None] - ar(qb)[None, None, :]) + qb * ar(2)[:, None, None]
    bw = 32
    d_bwin = ar(qb)[:, None] - CMP_STRIDE * (ar(bw)[None, :] - (bw - qb // CMP_STRIDE)) - (CMP_BLOCK - 1)
    d_c31 = np.full((LANE,), MAX_DISTANCE)
    past_chunks = past // CMP_STRIDE
    nc_s = past_chunks + SEL_BLOCK // CMP_STRIDE
    nr_s = _round_up(nc_s, LANE)
    nsb = nc_s // (SEL_BLOCK // CMP_STRIDE)
    nbp = _round_up(nsb, LANE)
    d_cmp_s = past + ar(t)[:, None] - CMP_STRIDE * ar(nr_s)[None, :] - (CMP_BLOCK - 1)
    nl = min(2 * LANE, pp * PAGE_SIZE)
    d_last = nl + ar(t)[:, None] - ar(nl)[None, :]
    d_new = ar(t)[:, None] - ar(LANE)[None, :]
    nkp = _round_up(wl + t, LANE)
    d_win = wl + ar(t)[:, None] - ar(nkp)[None, :]
    t01, bwin, c31, bias_c, blast, bnew, bias_w = _bias_tables(
        rel_bias_table, [d_t01, d_bwin, d_c31, d_cmp_s, d_last, d_new, d_win])
    bwin = bwin.reshape(NSA_HEADS * qb, bw)
    rows_s = NSA_HEADS * t
    c31r = jnp.repeat(c31, t, axis=0)[:, :1] * jnp.ones((1, 1), F32)
    blast = blast.reshape(rows_s, nl)
    bnew = bnew.reshape(rows_s, LANE)
    bias_w = bias_w.reshape(rows_s, nkp)
    at = jnp.asarray(_score_matrix(s // CMP_STRIDE, s // SEL_BLOCK).T, BF16)
    a_s = np.zeros((nr_s, nbp), np.float32)
    a_s[:nc_s] = _score_matrix(nc_s, nbp)
    a_s = jnp.asarray(a_s, BF16)
    keys_pad = (nsteps + 1) * pp * PAGE_SIZE
    expand = np.zeros((nbp, keys_pad), np.float32)
    kk = ar(nsb * SEL_BLOCK)
    expand[kk // SEL_BLOCK, kk] = 1.0
    expand = jnp.asarray(expand, BF16)

    pos_p = jnp.arange(s)
    tabs_p = tuple(jnp.tile(a, (b, 1)) for a in _rope_tables(pos_p))
    zp = _proj(xp.reshape(b * s, d), prm["g_attn"], prm["w_cat"], prm["tn"])
    pp_ = _post(zp, zoff, tabs_p, prm)
    kvcg = _cmp_prompt(pp_["kvc"].reshape(b, s, KV_W), prm)
    o_nsa_p = _nsa_prompt(pp_["qn"], pp_["misc"], kvcg, pp_["kvsg"], pp_["kvwg"], t01, bwin, c31, at, b, s)
    o_mla_p = _mla_prompt(pp_["qa"], pp_["qr"], pp_["ckvb"], pp_["krp"], prm["wuv"], b, s)
    x1p, h2p = _outproj(xp.reshape(b * s, d), o_nsa_p.reshape(b * s, -1), o_mla_p.reshape(b * s, -1), zp, zoff, prm)
    yp = _ffn(h2p, x1p, prm).reshape(b, s, d)

    pos_s = past + jnp.arange(t)
    tabs_s = tuple(jnp.tile(a, (bd, 1)) for a in _rope_tables(pos_s))
    zs = _proj(xs.reshape(bd * t, d), prm["g_attn"], prm["w_cat"], prm["tn"])
    ps = _post(zs, zoff, tabs_s, prm)
    cc = cache_cmp.reshape(n_pool, PAGE_SIZE, KV_W)
    cs = cache_slc.reshape(n_pool, PAGE_SIZE, KV_W)
    ocmp_s, score = _smp_cmp(page_table, cc, ps["kvc"].reshape(bd, t, KV_W), ps["qn"], bias_c, a_s, prm, pp)
    mask = _topk_mask(score.reshape(bd * NSA_KV_HEADS * t, nbp), expand, nsb, pp * PAGE_SIZE)
    osel_s = _smp_sel(page_table, cs, mask.reshape(bd, NSA_KV_HEADS * t, keys_pad), ps["qn"],
                      ps["kvs"].reshape(bd, t, KV_W), c31r, blast, bnew, pp)
    o_mla_s = _smp_mla(page_table, cache_ckv, cache_kr, ps["qa"], ps["qr"], ps["ckvb"].reshape(bd, t, KV_LORA),
                       ps["krp"].reshape(bd, t, LANE), prm["wuv"], pp)
    o_nsa_s, win_s = _smp_win(win_state.reshape(bd, wl, KV_W), ps["kvw"].reshape(bd, t, KV_W), ps["qn"], ps["misc"],
                              ocmp_s, osel_s, bias_w)
    x1s, h2s = _outproj(xs.reshape(bd * t, d), o_nsa_s, o_mla_s, zs, zoff, prm)
    ys = _ffn(h2s, x1s, prm).reshape(bd, t, d)

    kvh = (2, NSA_KV_HEADS, HEAD_DIM)
    wp = min(WINDOW, s)
    return (yp, ys,
            pp_["kvc"].reshape((b, s) + kvh), ps["kvc"].reshape((bd, t) + kvh),
            pp_["kvs"].reshape((b, s) + kvh), ps["kvs"].reshape((bd, t) + kvh),
            pp_["ckv"].reshape(b, s, KV_LORA), ps["ckv"].reshape(bd, t, KV_LORA),
            pp_["misc"][:, :QK_ROPE].reshape(b, s, QK_ROPE), ps["misc"][:, :QK_ROPE].reshape(bd, t, QK_ROPE),
            pp_["kvw"].reshape((b, s) + kvh)[:, s - wp:], win_s.reshape((bd, wl) + kvh))


def kernel(x_prompt, x_sample, cache_cmp_kv, cache_slc_kv, cache_mla_ckv, cache_mla_krope, state_win_kv, page_table, rel_bias_table, attn_norm, w_in, g_q_nsa, g_k_cmp, g_k_slc, g_k_win, phi_cmp, g_cq, w_uq, g_q_mla, g_ckv, g_krope, w_uk, w_uv, w_proj_nsa, w_proj_mla, w_out, ffn_norm, w_gate, w_up, w_down):
    depth = w_in.shape[0]
    xp, xs = x_prompt, x_sample
    per_layer = []
    for l in range(depth):
        weights = (attn_norm[l], w_in[l], g_q_nsa[l], g_k_cmp[l], g_k_slc[l], g_k_win[l], phi_cmp[l], g_cq[l],
                   w_uq[l], g_q_mla[l], g_ckv[l], g_krope[l], w_uk[l], w_uv[l], w_proj_nsa[l], w_proj_mla[l],
                   w_out[l], ffn_norm[l], w_gate[l], w_up[l], w_down[l])
        xp, xs, *st = _layer(xp, xs, cache_cmp_kv[l], cache_slc_kv[l], cache_mla_ckv[l], cache_mla_krope[l],
                             state_win_kv[l], page_table, rel_bias_table, weights)
        per_layer.append(st)
    stacked = [jnp.stack(v) for v in zip(*per_layer)]
    return (xp, xs, *stacked)
```

```python
import functools
import math

import numpy as np
import jax
import jax.numpy as jnp
from jax import lax
from jax.experimental import pallas as pl
from jax.experimental.pallas import tpu as pltpu

F32 = jnp.float32
BF16 = jnp.bfloat16

NSA_HEADS = 16
NSA_KV_HEADS = 2
NSA_GROUP = NSA_HEADS // NSA_KV_HEADS
HEAD_DIM = 64
CMP_BLOCK = 32
CMP_STRIDE = 16
SEL_BLOCK = 64
N_SEL = 16
WINDOW = 512
MLA_HEADS = 16
QK_NOPE = 64
QK_ROPE = 32
V_DIM = 64
KV_LORA = 256
Q_LORA = 512
ROPE_THETA = 10000.0
N_BUCKETS = 32
MAX_DISTANCE = 128
PAGE_SIZE = 128
Q_BLOCK = 128
EPS = 1e-6
NEG = -1e30
PAD_SCORE = -3e38
FORCE_BONUS = 1e6

LANE = 128
KV_W = 2 * NSA_KV_HEADS * HEAD_DIM
GATE_N_W = 3 * NSA_HEADS
MISC_GATE_OFF = QK_ROPE
VMEM_LIMIT = 56 * 1024 * 1024


def _round_up(a, b):
    return -(-a // b) * b


def _cparams(sem):
    return pltpu.CompilerParams(dimension_semantics=sem, vmem_limit_bytes=VMEM_LIMIT)


def _dot(a, b):
    return jnp.dot(a, b, preferred_element_type=F32)


def _dot_nt(a, b):
    return lax.dot_general(a, b, (((1,), (1,)), ((), ())), preferred_element_type=F32)


def _split3(x):
    hi = x.astype(BF16)
    r = x - hi.astype(F32)
    mid = r.astype(BF16)
    lo = (r - mid.astype(F32)).astype(BF16)
    return hi, mid, lo


def _dot_sel(x, m):
    hi, mid, lo = _split3(x)
    return _dot(hi, m) + _dot(mid, m) + _dot(lo, m)


def _dot_sel_nt(m, x):
    hi, mid, lo = _split3(x)
    return _dot_nt(m, hi) + _dot_nt(m, mid) + _dot_nt(m, lo)


def _seg_rinv(x, e, et, width):
    ss = _dot_sel(x * x, e)
    r = lax.rsqrt(ss / width + EPS)
    return _dot_sel(r, et)


def _t5_bucket_np(dist):
    n = np.maximum(dist, 0)
    exact = N_BUCKETS // 2
    log_ratio = np.log(np.maximum(n, 1).astype(np.float32) / exact) / math.log(MAX_DISTANCE / exact)
    large = np.minimum(exact + (log_ratio * (N_BUCKETS - exact)).astype(np.int32), N_BUCKETS - 1)
    return np.where(n < exact, n, large)


def _seg_indicator(width, seg):
    e = np.zeros((width, LANE), np.float32)
    e[np.arange(width), np.arange(width) // seg] = 1.0
    return e


def _score_matrix(nc, nb_pad):
    ratio = SEL_BLOCK // CMP_STRIDE
    a = np.zeros((nc, nb_pad), np.float32)
    c = np.arange(nc)
    a[c, c // ratio] = 1.0
    cc = c[(c % ratio == ratio - 1) & (c // ratio + 1 < nc // ratio)]
    a[cc, cc // ratio + 1] = 1.0
    return a


def _bias_expand_kernel(tt_ref, oh_ref, o_ref):
    hi, mid, lo = _split3(tt_ref[...])
    oh = oh_ref[...]
    o_ref[...] = _dot(hi, oh) + _dot(mid, oh) + _dot(lo, oh)


def _bias_tables(rel_bias_table, dist_list):
    sizes = [_round_up(d.size, LANE) for d in dist_list]
    tile = 2048
    total = _round_up(sum(sizes), tile)
    onehot = np.zeros((N_BUCKETS, total), np.float32)
    off = 0
    for d, sz in zip(dist_list, sizes):
        b = _t5_bucket_np(d.reshape(-1))
        onehot[b, off + np.arange(d.size)] = 1.0
        off += sz
    out = pl.pallas_call(
        _bias_expand_kernel,
        grid=(total // tile,),
        in_specs=[pl.BlockSpec((NSA_HEADS, N_BUCKETS), lambda i: (0, 0)),
                  pl.BlockSpec((N_BUCKETS, tile), lambda i: (0, i))],
        out_specs=pl.BlockSpec((NSA_HEADS, tile), lambda i: (0, i)),
        out_shape=jax.ShapeDtypeStruct((NSA_HEADS, total), F32),
        compiler_params=_cparams(("arbitrary",)),
        name="bias_expand",
    )(rel_bias_table.T, jnp.asarray(onehot, BF16))
    res, off = [], 0
    for d, sz in zip(dist_list, sizes):
        res.append(out[:, off:off + d.size].reshape((NSA_HEADS,) + d.shape))
        off += sz
    return res


def _z_layout(d_model):
    off = {}
    off["q"] = 0
    off["cq"] = NSA_HEADS * HEAD_DIM
    off["misc"] = off["cq"] + Q_LORA
    pos = _round_up(off["misc"] + LANE, d_model)
    off["ga"] = pos
    off["gb"] = pos + d_model
    pos = _round_up(off["gb"] + d_model, 3 * KV_W)
    off["kv"] = pos
    off["ckv"] = off["kv"] + 3 * KV_W
    off["end"] = off["ckv"] + KV_LORA
    return off


def _proj_kernel(x_ref, g_ref, w_ref, z_ref, h_ref):
    @pl.when(pl.program_id(1) == 0)
    def _():
        x = x_ref[...]
        ms = jnp.mean(x * x, axis=-1, keepdims=True)
        h_ref[...] = (x * lax.rsqrt(ms + EPS) * g_ref[...]).astype(BF16)

    z_ref[...] = _dot(h_ref[...], w_ref[...])


def _proj(x2d, g_norm, w_cat, tn):
    n, d = x2d.shape
    zw = w_cat.shape[1]
    tm = min(512, n)
    return pl.pallas_call(
        _proj_kernel,
        grid=(n // tm, zw // tn),
        in_specs=[pl.BlockSpec((tm, d), lambda i, j: (i, 0)),
                  pl.BlockSpec((1, d), lambda i, j: (0, 0)),
                  pl.BlockSpec((d, tn), lambda i, j: (0, j))],
        out_specs=pl.BlockSpec((tm, tn), lambda i, j: (i, j)),
        out_shape=jax.ShapeDtypeStruct((n, zw), F32),
        scratch_shapes=[pltpu.VMEM((tm, d), BF16)],
        compiler_params=_cparams(("parallel", "arbitrary")),
        name="in_proj",
    )(x2d, g_norm, w_cat)


def _post_kernel(zq_ref, zcq_ref, zkv_ref, zckv_ref, zmisc_ref, cos_ref, sina_ref, sinb_ref,
                 gq_ref, gks_ref, gkw_ref, gcq_ref, wuq_ref, gmn_ref, gmr_ref, gckv_ref, gkr_ref, wuk_ref,
                 e64_ref, e64t_ref, e128_ref, e128t_ref,
                 qn_ref, kvc_ref, kvs_ref, kvw_ref, kvsg_ref, kvwg_ref,
                 ckv_ref, ckvb_ref, misc_ref, krp_ref, qa_ref, qr_ref, *feature_major_refs):
    e64, e64t = e64_ref[...], e64t_ref[...]
    e128, e128t = e128_ref[...], e128t_ref[...]
    nq = NSA_HEADS * HEAD_DIM

    zq = zq_ref[...]
    qn = zq * _seg_rinv(zq, e64, e64t, HEAD_DIM) * gq_ref[...] * (HEAD_DIM ** -0.5)
    qn_ref[...] = qn.astype(BF16)
    if feature_major_refs:
        feature_major_refs[0][...] = qn.T.astype(BF16)

    zkv = zkv_ref[...]
    kvc_ref[...] = zkv[:, :KV_W]
    half = KV_W // 2
    for idx, (g_ref, out_ref, outg_ref) in enumerate(((gks_ref, kvs_ref, kvsg_ref), (gkw_ref, kvw_ref, kvwg_ref))):
        kv = zkv[:, (idx + 1) * KV_W:(idx + 2) * KV_W]
        k, v = kv[:, :half], kv[:, half:]
        kn = k * _seg_rinv(k, e64[:half], e64t[:, :half], HEAD_DIM) * g_ref[...]
        out_ref[...] = jnp.concatenate([kn, v], axis=1)
        for g in range(NSA_KV_HEADS):
            sl = slice(g * HEAD_DIM, (g + 1) * HEAD_DIM)
            kvg = jnp.concatenate([kn[:, sl], v[:, sl]], axis=1)
            outg_ref[g] = kvg.astype(BF16)
            if feature_major_refs:
                for sub in range(kvg.shape[0] // Q_BLOCK):
                    feature_major_refs[1 + idx][g, sub] = kvg[sub * Q_BLOCK:(sub + 1) * Q_BLOCK].T.astype(BF16)

    zc = zckv_ref[...]
    ckv = zc * lax.rsqrt(jnp.mean(zc * zc, axis=-1, keepdims=True) + EPS) * gckv_ref[...]
    ckv_ref[...] = ckv
    ckvb_ref[...] = ckv.astype(BF16)

    cos, sina, sinb = cos_ref[...], sina_ref[...], sinb_ref[...]
    zm = zmisc_ref[...]
    lane = lax.broadcasted_iota(jnp.int32, zm.shape, 1)
    is_kr = lane < QK_ROPE
    ms = jnp.sum(jnp.where(is_kr, zm * zm, 0.0), axis=-1, keepdims=True) / QK_ROPE
    krn = zm * lax.rsqrt(ms + EPS) * gkr_ref[...]
    half_r = QK_ROPE // 2
    kr = krn * cos + pltpu.roll(krn, half_r, 1) * sina + pltpu.roll(krn, LANE - half_r, 1) * sinb
    kr = jnp.where(is_kr, kr, 0.0)
    gates = jnp.where(lane < MISC_GATE_OFF + GATE_N_W, jax.nn.sigmoid(zm), 0.0)
    misc_ref[...] = jnp.where(is_kr, kr, gates)
    krp_ref[...] = kr.astype(BF16)

    zcq = zcq_ref[...]
    cqn = (zcq * lax.rsqrt(jnp.mean(zcq * zcq, axis=-1, keepdims=True) + EPS) * gcq_ref[...]).astype(BF16)
    qm = _dot(cqn, wuq_ref[...])
    nope, ropep = qm[:, :nq], qm[:, nq:]
    ss = _dot_sel(nope * nope, e64) + _dot_sel(ropep * ropep, e128)
    r = lax.rsqrt(ss / (QK_NOPE + QK_ROPE) + EPS)
    scale = (QK_NOPE + QK_ROPE) ** -0.5
    nope_n = nope * _dot_sel(r, e64t) * gmn_ref[...] * scale
    rope_n = ropep * _dot_sel(r, e128t) * gmr_ref[...] * scale
    wr = MLA_HEADS * LANE
    cos_t = jnp.concatenate([cos] * MLA_HEADS, axis=1)
    sina_t = jnp.concatenate([sina] * MLA_HEADS, axis=1)
    sinb_t = jnp.concatenate([sinb] * MLA_HEADS, axis=1)
    qr = rope_n * cos_t + pltpu.roll(rope_n, half_r, 1) * sina_t + pltpu.roll(rope_n, wr - half_r, 1) * sinb_t
    qr_ref[...] = qr.astype(BF16)
    nb = nope_n.astype(BF16)
    for h in range(MLA_HEADS):
        qa = _dot_nt(nb[:, h * QK_NOPE:(h + 1) * QK_NOPE], wuk_ref[h])
        qa_ref[:, h * KV_LORA:(h + 1) * KV_LORA] = qa.astype(BF16)


def _post(z, zoff, tabs, prm, feature_major):
    n = z.shape[0]
    tm = min(256, n)
    nq = NSA_HEADS * HEAD_DIM
    row = lambda w, c: pl.BlockSpec((tm, w), lambda i, c=c: (i, c))
    full = lambda a: pl.BlockSpec(a.shape, lambda i, nd=a.ndim: (0,) * nd)
    cos, sina, sinb = tabs
    consts = [prm["gq"], prm["gks"], prm["gkw"], prm["gcq"], prm["wuq"], prm["gmn"], prm["gmr"], prm["gckv"],
              prm["gkr"], prm["wuk"], prm["e64"], prm["e64t"], prm["e128"], prm["e128t"]]
    in_specs = [row(nq, zoff["q"] // nq), row(Q_LORA, zoff["cq"] // Q_LORA), row(3 * KV_W, zoff["kv"] // (3 * KV_W)),
                row(KV_LORA, zoff["ckv"] // KV_LORA), row(LANE, zoff["misc"] // LANE),
                row(LANE, 0), row(LANE, 0), row(LANE, 0)] + [full(a) for a in consts]
    sds = jax.ShapeDtypeStruct
    out_shape = [sds((n, nq), BF16), sds((n, KV_W), F32), sds((n, KV_W), F32), sds((n, KV_W), F32),
                 sds((NSA_KV_HEADS, n, LANE), BF16), sds((NSA_KV_HEADS, n, LANE), BF16),
                 sds((n, KV_LORA), F32), sds((n, KV_LORA), BF16), sds((n, LANE), F32), sds((n, LANE), BF16),
                 sds((n, MLA_HEADS * KV_LORA), BF16), sds((n, MLA_HEADS * LANE), BF16)]
    grp = pl.BlockSpec((NSA_KV_HEADS, tm, LANE), lambda i: (0, i, 0))
    out_specs = [row(nq, 0), row(KV_W, 0), row(KV_W, 0), row(KV_W, 0), grp, grp,
                 row(KV_LORA, 0), row(KV_LORA, 0), row(LANE, 0), row(LANE, 0),
                 row(MLA_HEADS * KV_LORA, 0), row(MLA_HEADS * LANE, 0)]
    names = ["qn", "kvc", "kvs", "kvw", "kvsg", "kvwg", "ckv", "ckvb", "misc", "krp", "qa", "qr"]
    if feature_major:
        tiles = (NSA_KV_HEADS, n // Q_BLOCK, LANE, Q_BLOCK)
        grpt = pl.BlockSpec((NSA_KV_HEADS, tm // Q_BLOCK, LANE, Q_BLOCK), lambda i: (0, i, 0, 0))
        out_shape += [sds((nq, n), BF16), sds(tiles, BF16), sds(tiles, BF16)]
        out_specs += [pl.BlockSpec((nq, tm), lambda i: (0, i)), grpt, grpt]
        names += ["qnt", "kvsgt", "kvwgt"]
    outs = pl.pallas_call(
        _post_kernel, grid=(n // tm,), in_specs=in_specs, out_specs=out_specs, out_shape=out_shape,
        compiler_params=_cparams(("parallel",)), name="post_proj",
    )(z, z, z, z, z, cos, sina, sinb, *consts)
    return dict(zip(names, outs))


def _chunk_partials(rows, wlo, whi):
    ch = rows.reshape(rows.shape[0] // CMP_STRIDE, CMP_STRIDE, rows.shape[1])
    return jnp.sum(ch * wlo[None], axis=1), jnp.sum(ch * whi[None], axis=1)


def _blocks_to_groups(lo, hi, gkc, e64, e64t):
    nr = lo.shape[0]
    rid = lax.broadcasted_iota(jnp.int32, hi.shape, 0)
    hi_next = jnp.where(rid < nr - 1, pltpu.roll(hi, nr - 1, 0), 0.0)
    blk = lo + hi_next
    half = KV_W // 2
    k, v = blk[:, :half], blk[:, half:]
    kn = k * _seg_rinv(k, e64[:half], e64t[:, :half], HEAD_DIM) * gkc
    out = []
    for g in range(NSA_KV_HEADS):
        sl = slice(g * HEAD_DIM, (g + 1) * HEAD_DIM)
        out.append(jnp.concatenate([kn[:, sl], v[:, sl]], axis=1))
    return out


def _cmp_prompt_kernel(kvc_ref, wlo_ref, whi_ref, gkc_ref, e64_ref, e64t_ref, o_ref, ot_ref):
    lo, hi = _chunk_partials(kvc_ref[0], wlo_ref[...], whi_ref[...])
    groups = _blocks_to_groups(lo, hi, gkc_ref[...], e64_ref[...], e64t_ref[...])
    for g in range(NSA_KV_HEADS):
        o_ref[0, g] = groups[g].astype(BF16)
        ot_ref[0, g] = groups[g].T.astype(BF16)


def _cmp_prompt(kvc, prm):
    b, s, _ = kvc.shape
    nc = s // CMP_STRIDE
    full = lambda a: pl.BlockSpec(a.shape, lambda i, nd=a.ndim: (0,) * nd)
    consts = [prm["wlo"], prm["whi"], prm["gkc"], prm["e64"], prm["e64t"]]
    return pl.pallas_call(
        _cmp_prompt_kernel, grid=(b,),
        in_specs=[pl.BlockSpec((1, s, KV_W), lambda i: (i, 0, 0))] + [full(a) for a in consts],
        out_specs=[pl.BlockSpec((1, NSA_KV_HEADS, nc, LANE), lambda i: (i, 0, 0, 0)),
                   pl.BlockSpec((1, NSA_KV_HEADS, LANE, nc), lambda i: (i, 0, 0, 0))],
        out_shape=[jax.ShapeDtypeStruct((b, NSA_KV_HEADS, nc, LANE), BF16),
                   jax.ShapeDtypeStruct((b, NSA_KV_HEADS, LANE, nc), BF16)],
        compiler_params=_cparams(("parallel",)), name="cmp_blocks_prompt",
    )(kvc, *consts)


def _softmax_update(s, mask, v, m_ref, l_ref, acc_ref, v_feature_major=False):
    m_old = m_ref[...]
    m_new = jnp.maximum(m_old, jnp.max(s, axis=-1, keepdims=True))
    p = jnp.exp(s - m_new)
    if mask is not None:
        p = jnp.where(mask, p, 0.0)
    alpha = jnp.exp(m_old - m_new)
    l_ref[...] = alpha * l_ref[...] + jnp.sum(p, axis=-1, keepdims=True)
    pv = _dot_nt(p.astype(BF16), v) if v_feature_major else _dot(p.astype(BF16), v)
    acc_ref[...] = alpha * acc_ref[...] + pv
    m_ref[...] = m_new


def _softmax_init(m_ref, l_ref, acc_ref):
    m_ref[...] = jnp.full(m_ref.shape, NEG, F32)
    l_ref[...] = jnp.zeros(l_ref.shape, F32)
    acc_ref[...] = jnp.zeros(acc_ref.shape, F32)


def _col_softmax_update(s, mask, kvt, m_ref, l_ref, acc_ref):
    m_old = m_ref[...]
    m_new = jnp.maximum(m_old, jnp.max(s, axis=0, keepdims=True))
    p = jnp.exp(s - m_new)
    if mask is not None:
        p = jnp.where(mask, p, 0.0)
    alpha = jnp.exp(m_old - m_new)
    l_ref[...] = alpha * l_ref[...] + jnp.sum(p, axis=0, keepdims=True)
    acc_ref[...] = alpha * acc_ref[...] + _dot(kvt, p.astype(BF16))
    m_ref[...] = m_new


def _nsa_prompt_kernel(qt_ref, misc_ref, kvc_ref, kvct_ref, kvs_ref, kvst_ref, kvw_ref, kvwt_ref,
                       t01_ref, bwin_ref, c31_ref, at_ref, o_ref, m_ref, l_ref, acc_ref, sc_ref):
    i = pl.program_id(1)
    qb = Q_BLOCK
    nc = kvc_ref.shape[2]
    nb = at_ref.shape[0]
    bw = bwin_ref.shape[0]
    gate_t = misc_ref[0].T
    tk2 = lax.broadcasted_iota(jnp.int32, (qb, qb), 0)
    tq2 = lax.broadcasted_iota(jnp.int32, (qb, qb), 1)
    causal = tk2 <= tq2
    anti = tk2 > tq2
    tile8 = lambda m: jnp.concatenate([m] * NSA_GROUP, axis=1)
    pieces = []
    for g in range(NSA_KV_HEADS):
        h0 = g * NSA_GROUP
        ls = slice(h0 * qb, (h0 + NSA_GROUP) * qb)
        qt = jnp.concatenate([qt_ref[(h0 + h) * HEAD_DIM:(h0 + h + 1) * HEAD_DIM, :] for h in range(NSA_GROUP)], axis=1)
        qt = jnp.concatenate([qt, jnp.zeros_like(qt)], axis=0)
        c31 = c31_ref[:, ls]

        kvc = kvc_ref[0, g]
        s = _dot(kvc, qt)
        shift_c = lax.broadcasted_iota(jnp.int32, (nc, bw), 0)
        shift_j = lax.broadcasted_iota(jnp.int32, (nc, bw), 1)
        first = (qb // CMP_STRIDE) * i - (bw - qb // CMP_STRIDE)
        shift = (shift_c - shift_j == first).astype(BF16)
        hi, mid, lo = _split3(bwin_ref[:, ls])
        placed = _dot(shift, hi) + _dot(shift, mid) + _dot(shift, lo)
        bias = jnp.where(lax.broadcasted_iota(jnp.int32, (nc, 1), 0) < first, c31, placed)
        c_id = lax.broadcasted_iota(jnp.int32, (nc, qb), 0)
        t_id = lax.broadcasted_iota(jnp.int32, (nc, qb), 1)
        mask_c = tile8((qb * i + t_id - CMP_STRIDE * c_id - (CMP_BLOCK - 1)) >= 0)
        s = jnp.where(mask_c, s + bias, NEG)
        e = jnp.exp(s - jnp.max(s, axis=0, keepdims=True))
        p = jnp.where(mask_c, e * (1.0 / jnp.sum(e, axis=0, keepdims=True)), 0.0)
        o_cmp = _dot(kvct_ref[0, g], p.astype(BF16))

        pg = p[:, :qb]
        for h in range(1, NSA_GROUP):
            pg = pg + p[:, h * qb:(h + 1) * qb]
        hi, mid, lo = _split3(pg)
        at = at_ref[...]
        sc = _dot(at, hi) + _dot(at, mid) + _dot(at, lo)
        j_id = lax.broadcasted_iota(jnp.int32, (nb, qb), 0)
        qpos = qb * i + lax.broadcasted_iota(jnp.int32, (nb, qb), 1)
        cur = qpos // SEL_BLOCK
        forced = (j_id == 0) | (j_id == cur) | (j_id == cur - 1)
        valid = j_id * SEL_BLOCK <= qpos
        sc = jnp.where(valid, jnp.where(forced, sc + FORCE_BONUS, sc), NEG)
        sc_ref[...] = sc

        def rank_body(r, rank):
            sr = sc_ref[pl.ds(r, 1), :]
            ahead = (sr > sc) | ((sr == sc) & (r < j_id))
            return rank + ahead.astype(jnp.int32)

        rank = lax.fori_loop(0, nb, rank_body, jnp.zeros((nb, qb), jnp.int32))
        sel_t = (rank < min(N_SEL, nb)).astype(F32).astype(BF16)

        def sel_mask(kt):
            ek = lax.broadcasted_iota(jnp.int32, (qb, nb), 0)
            ej = lax.broadcasted_iota(jnp.int32, (qb, nb), 1)
            expand = (ej == (qb // SEL_BLOCK) * kt + ek // SEL_BLOCK).astype(BF16)
            return _dot(expand, sel_t) > 0.5

        def tile(kv_ref, kvt_ref, kt, near, mask):
            kv = kv_ref[g, 0, pl.ds(pl.multiple_of(kt * qb, qb), qb), :]
            bias = c31 if near is None else t01_ref[near, :, ls]
            st = _dot(kv, qt) + bias
            if mask is None:
                _col_softmax_update(st, None, kvt_ref[g, 0, kt], m_ref, l_ref, acc_ref)
            else:
                m8 = tile8(mask)
                _col_softmax_update(jnp.where(m8, st, NEG), m8, kvt_ref[g, 0, kt], m_ref, l_ref, acc_ref)

        _softmax_init(m_ref, l_ref, acc_ref)

        def far_body(kt, carry):
            tile(kvs_ref, kvst_ref, kt, None, sel_mask(kt))
            return carry

        lax.fori_loop(0, jnp.maximum(i - 1, 0), far_body, 0)

        @pl.when(i >= 1)
        def _():
            tile(kvs_ref, kvst_ref, i - 1, 1, sel_mask(i - 1))

        tile(kvs_ref, kvst_ref, i, 0, sel_mask(i) & causal)
        o_sel = acc_ref[...] * (1.0 / l_ref[...])

        _softmax_init(m_ref, l_ref, acc_ref)
        nwin = WINDOW // qb
        for diff in range(nwin, -1, -1):
            near = diff if diff <= 1 else None
            mask = causal if diff == 0 else (anti if diff == nwin else None)
            if diff == 0:
                tile(kvw_ref, kvwt_ref, i, near, mask)
            else:
                @pl.when(i >= diff)
                def _(diff=diff, near=near, mask=mask):
                    tile(kvw_ref, kvwt_ref, i - diff, near, mask)
        o_win = acc_ref[...] * (1.0 / l_ref[...])

        for h in range(NSA_GROUP):
            hh = h0 + h
            cs = slice(h * qb, (h + 1) * qb)
            r0 = MISC_GATE_OFF + hh
            gc = gate_t[r0:r0 + 1, :]
            gs = gate_t[r0 + NSA_HEADS:r0 + NSA_HEADS + 1, :]
            gw = gate_t[r0 + 2 * NSA_HEADS:r0 + 2 * NSA_HEADS + 1, :]
            oh = gc * o_cmp[HEAD_DIM:, cs] + gs * o_sel[HEAD_DIM:, cs] + gw * o_win[HEAD_DIM:, cs]
            pieces.append(oh.T)
    o_ref[0] = jnp.concatenate(pieces, axis=1).astype(BF16)


def _nsa_prompt(qnt, misc, kvcg, kvcgt, kvsg, kvsgt, kvwg, kvwgt, t01t, bwint, c31row, at, b, s):
    nqb = s // Q_BLOCK
    nq = NSA_HEADS * HEAD_DIM
    nc = s // CMP_STRIDE
    nb = s // SEL_BLOCK
    cols = NSA_GROUP * Q_BLOCK
    full = lambda a: pl.BlockSpec(a.shape, lambda bi, i, nd=a.ndim: (0,) * nd)
    rowm = pl.BlockSpec((NSA_KV_HEADS, 1, s, LANE), lambda bi, i: (0, bi, 0, 0))
    featm = pl.BlockSpec((NSA_KV_HEADS, 1, nqb, LANE, Q_BLOCK), lambda bi, i: (0, bi, 0, 0, 0))
    return pl.pallas_call(
        _nsa_prompt_kernel, grid=(b, nqb),
        in_specs=[pl.BlockSpec((nq, Q_BLOCK), lambda bi, i: (0, bi * nqb + i)),
                  pl.BlockSpec((1, Q_BLOCK, LANE), lambda bi, i: (bi, i, 0)),
                  pl.BlockSpec((1, NSA_KV_HEADS, nc, LANE), lambda bi, i: (bi, 0, 0, 0)),
                  pl.BlockSpec((1, NSA_KV_HEADS, LANE, nc), lambda bi, i: (bi, 0, 0, 0)),
                  rowm, featm, rowm, featm, full(t01t), full(bwint), full(c31row), full(at)],
        out_specs=pl.BlockSpec((1, Q_BLOCK, nq), lambda bi, i: (bi, i, 0)),
        out_shape=jax.ShapeDtypeStruct((b, s, nq), BF16),
        scratch_shapes=[pltpu.VMEM((1, cols), F32), pltpu.VMEM((1, cols), F32), pltpu.VMEM((LANE, cols), F32),
                        pltpu.VMEM((nb, Q_BLOCK), F32)],
        compiler_params=_cparams(("parallel", "arbitrary")), name="nsa_prompt",
    )(qnt, misc.reshape(b, s, LANE), kvcg, kvcgt,
      kvsg.reshape(NSA_KV_HEADS, b, s, LANE), kvsgt.reshape(NSA_KV_HEADS, b, nqb, LANE, Q_BLOCK),
      kvwg.reshape(NSA_KV_HEADS, b, s, LANE), kvwgt.reshape(NSA_KV_HEADS, b, nqb, LANE, Q_BLOCK),
      t01t, bwint, c31row, at)


def _mla_prompt_kernel(qa_ref, qr_ref, ckv_ref, krp_ref, wuv_ref, o_ref, m_ref, l_ref, acc_ref, *, tk):
    i = pl.program_id(1)
    qb = Q_BLOCK
    qa = jnp.concatenate([qa_ref[0, :, h * KV_LORA:(h + 1) * KV_LORA] for h in range(MLA_HEADS)], axis=0)
    qr = jnp.concatenate([qr_ref[0, :, h * LANE:(h + 1) * LANE] for h in range(MLA_HEADS)], axis=0)
    rows = MLA_HEADS * qb
    _softmax_init(m_ref, l_ref, acc_ref)

    def tile(kt, masked):
        ks = pl.ds(pl.multiple_of(kt * tk, tk), tk)
        ck = ckv_ref[0, ks, :]
        s = _dot_nt(qa, ck) + _dot_nt(qr, krp_ref[0, ks, :])
        if masked:
            tq = qb * i + lax.broadcasted_iota(jnp.int32, (qb, tk), 0)
            kp = kt * tk + lax.broadcasted_iota(jnp.int32, (qb, tk), 1)
            m3 = jnp.broadcast_to((kp <= tq)[None], (MLA_HEADS, qb, tk))
            s = jnp.where(m3, s.reshape(MLA_HEADS, qb, tk), NEG).reshape(rows, tk)
            _softmax_update(s, m3.reshape(rows, tk), ck, m_ref, l_ref, acc_ref)
        else:
            _softmax_update(s, None, ck, m_ref, l_ref, acc_ref)

    nfull = (i * qb) // tk

    def body(kt, carry):
        tile(kt, False)
        return carry

    lax.fori_loop(0, nfull, body, 0)
    tile(nfull, True)
    o_lat = (acc_ref[...] / l_ref[...]).astype(BF16)
    pieces = [_dot(o_lat[h * qb:(h + 1) * qb], wuv_ref[h]) for h in range(MLA_HEADS)]
    o_ref[0] = jnp.concatenate(pieces, axis=1).astype(BF16)


def _mla_prompt(qa, qr, ckvb, krp, wuv, b, s):
    nqb = s // Q_BLOCK
    tk = min(512, s)
    rows = MLA_HEADS * Q_BLOCK
    wa, wr = MLA_HEADS * KV_LORA, MLA_HEADS * LANE
    return pl.pallas_call(
        functools.partial(_mla_prompt_kernel, tk=tk), grid=(b, nqb),
        in_specs=[pl.BlockSpec((1, Q_BLOCK, wa), lambda bi, i: (bi, i, 0)),
                  pl.BlockSpec((1, Q_BLOCK, wr), lambda bi, i: (bi, i, 0)),
                  pl.BlockSpec((1, s, KV_LORA), lambda bi, i: (bi, 0, 0)),
                  pl.BlockSpec((1, s, LANE), lambda bi, i: (bi, 0, 0)),
                  pl.BlockSpec(wuv.shape, lambda bi, i: (0, 0, 0))],
        out_specs=pl.BlockSpec((1, Q_BLOCK, MLA_HEADS * V_DIM), lambda bi, i: (bi, i, 0)),
        out_shape=jax.ShapeDtypeStruct((b, s, MLA_HEADS * V_DIM), BF16),
        scratch_shapes=[pltpu.VMEM((rows, 1), F32), pltpu.VMEM((rows, 1), F32), pltpu.VMEM((rows, KV_LORA), F32)],
        compiler_params=_cparams(("parallel", "arbitrary")), name="mla_prompt",
    )(qa.reshape(b, s, wa), qr.reshape(b, s, wr), ckvb.reshape(b, s, KV_LORA), krp.reshape(b, s, LANE), wuv)


def _outproj_kernel(x_ref, on_ref, om_ref, ga_ref, gb_ref, wpn_ref, wpm_ref, wo_ref, gf_ref, x1_ref, h2_ref):
    a = _dot(on_ref[...], wpn_ref[...])
    b = _dot(om_ref[...], wpm_ref[...])
    mix = jax.nn.sigmoid(ga_ref[...]) * a + jax.nn.sigmoid(gb_ref[...]) * b
    x1 = x_ref[...] + _dot(mix.astype(BF16), wo_ref[...])
    x1_ref[...] = x1
    ms = jnp.mean(x1 * x1, axis=-1, keepdims=True)
    h2_ref[...] = (x1 * lax.rsqrt(ms + EPS) * gf_ref[...]).astype(BF16)


def _outproj(x2d, o_nsa, o_mla, z, zoff, prm):
    n, d = x2d.shape
    tm = min(256, n)
    row = lambda w, c: pl.BlockSpec((tm, w), lambda i, c=c: (i, c))
    once = lambda a: pl.BlockSpec(a.shape, lambda i, nd=a.ndim: (0,) * nd, pipeline_mode=pl.Buffered(1))
    return pl.pallas_call(
        _outproj_kernel, grid=(n // tm,),
        in_specs=[row(d, 0), row(o_nsa.shape[1], 0), row(o_mla.shape[1], 0), row(d, zoff["ga"] // d),
                  row(d, zoff["gb"] // d), once(prm["wpn"]), once(prm["wpm"]), once(prm["wo"]), once(prm["gf"])],
        out_specs=[row(d, 0), row(d, 0)],
        out_shape=[jax.ShapeDtypeStruct((n, d), F32), jax.ShapeDtypeStruct((n, d), BF16)],
        compiler_params=_cparams(("parallel",)), name="out_proj",
    )(x2d, o_nsa, o_mla, z, z, prm["wpn"], prm["wpm"], prm["wo"], prm["gf"])


def _ffn_kernel(h_ref, x1_ref, wg_ref, wu_ref, wd_ref, y_ref, acc_ref):
    f = pl.program_id(1)

    @pl.when(f == 0)
    def _():
        acc_ref[...] = jnp.zeros(acc_ref.shape, F32)

    h = h_ref[...]
    t = jax.nn.silu(_dot(h, wg_ref[...])) * _dot(h, wu_ref[...])
    acc_ref[...] += _dot(t.astype(BF16), wd_ref[...])

    @pl.when(f == pl.num_programs(1) - 1)
    def _():
        y_ref[...] = x1_ref[...] + acc_ref[...]


def _ffn(h2, x1, prm):
    n, d = x1.shape
    dff = prm["wg"].shape[1]
    tm = min(512, n)
    tf = 512 if dff % 512 == 0 else 256
    return pl.pallas_call(
        _ffn_kernel, grid=(n // tm, dff // tf),
        in_specs=[pl.BlockSpec((tm, d), lambda i, f: (i, 0)), pl.BlockSpec((tm, d), lambda i, f: (i, 0)),
                  pl.BlockSpec((d, tf), lambda i, f: (0, f)), pl.BlockSpec((d, tf), lambda i, f: (0, f)),
                  pl.BlockSpec((tf, d), lambda i, f: (f, 0))],
        out_specs=pl.BlockSpec((tm, d), lambda i, f: (i, 0)),
        out_shape=jax.ShapeDtypeStruct((n, d), F32),
        scratch_shapes=[pltpu.VMEM((tm, d), F32)],
        compiler_params=_cparams(("parallel", "arbitrary")), name="ffn",
    )(h2, x1, prm["wg"], prm["wu"], prm["wd"])


def _page_specs(pp, rows, cols):
    return [pl.BlockSpec((1, rows, cols), functools.partial(lambda b, s, pt, j: (pt[b, s * pp + j], 0, 0), j=j))
            for j in range(pp)]


def _feature_major_pages(cache):
    n_pool = cache.shape[0]
    return jnp.swapaxes(cache.reshape(n_pool, PAGE_SIZE, -1), 1, 2)


def _group_queries(q, lanes):
    t = q.shape[0]
    blocks = []
    for g in range(NSA_KV_HEADS):
        qg = jnp.concatenate([q[:, (g * NSA_GROUP + h) * HEAD_DIM:(g * NSA_GROUP + h + 1) * HEAD_DIM]
                              for h in range(NSA_GROUP)], axis=0)
        parts = []
        if g > 0:
            parts.append(jnp.zeros((NSA_GROUP * t, g * HEAD_DIM), q.dtype))
        parts.append(qg)
        parts.append(jnp.zeros((NSA_GROUP * t, lanes - (g + 1) * HEAD_DIM), q.dtype))
        blocks.append(jnp.concatenate(parts, axis=1))
    return jnp.concatenate(blocks, axis=0)


def _smp_cmp_kernel(pt_ref, *refs, pp, past_chunks, nsb):
    pages = refs[:pp]
    (new_ref, q_ref, wlo_ref, whi_ref, wt_ref, gkc_ref, e64_ref, e64t_ref, bias_ref, a_ref,
     ocmp_ref, score_ref, lo_ref, hi_ref) = refs[pp:]
    s = pl.program_id(1)
    nr = lo_ref.shape[0]
    t = q_ref.shape[0]
    wlo, whi = wlo_ref[...], whi_ref[...]
    cpp = PAGE_SIZE // CMP_STRIDE
    half = KV_W // 2

    @pl.when(s == 0)
    def _():
        lo_ref[past_chunks:, :] = jnp.zeros((nr - past_chunks, KV_W), F32)
        hi_ref[past_chunks:, :] = jnp.zeros((nr - past_chunks, KV_W), F32)

    x = jnp.concatenate([pages[j][0] for j in range(pp)], axis=0)
    x_hi = x.astype(BF16)
    x_lo = (x - x_hi.astype(F32)).astype(BF16)
    wt = wt_ref[...]
    w_hi = wt.astype(BF16)
    w_lo = (wt - w_hi.astype(F32)).astype(BF16)
    r = _dot_nt(w_hi, x_hi) + _dot_nt(w_lo, x_hi) + _dot_nt(w_hi, x_lo)
    for j in range(pp):
        c0 = j * KV_W
        r0 = pl.multiple_of((s * pp + j) * cpp, cpp)
        lo_ref[pl.ds(r0, cpp), :] = jnp.concatenate([r[0:cpp, c0:c0 + half],
                                                     r[2 * cpp:3 * cpp, c0 + half:c0 + KV_W]], axis=1)
        hi_ref[pl.ds(r0, cpp), :] = jnp.concatenate([r[cpp:2 * cpp, c0:c0 + half],
                                                     r[3 * cpp:4 * cpp, c0 + half:c0 + KV_W]], axis=1)

    @pl.when(s == pl.num_programs(1) - 1)
    def _():
        new = jnp.concatenate([new_ref[0], jnp.zeros((CMP_STRIDE - t, KV_W), F32)], axis=0)
        lo_n, hi_n = _chunk_partials(new, wlo, whi)
        pad = jnp.zeros((cpp - 1, KV_W), F32)
        lo_ref[past_chunks:past_chunks + cpp, :] = jnp.concatenate([lo_n, pad], axis=0)
        hi_ref[past_chunks:past_chunks + cpp, :] = jnp.concatenate([hi_n, pad], axis=0)
        groups = _blocks_to_groups(lo_ref[...], hi_ref[...], gkc_ref[...], e64_ref[...], e64t_ref[...])
        q = q_ref[...]
        nbp = a_ref.shape[1]
        past = past_chunks * CMP_STRIDE
        c_id = lax.broadcasted_iota(jnp.int32, (t, nr), 1)
        t_id = lax.broadcasted_iota(jnp.int32, (t, nr), 0)
        mask_c = (past + t_id - CMP_STRIDE * c_id - (CMP_BLOCK - 1)) >= 0
        j_id = lax.broadcasted_iota(jnp.int32, (t, nbp), 1)
        qpos = past + lax.broadcasted_iota(jnp.int32, (t, nbp), 0)
        cur = qpos // SEL_BLOCK
        forced = (j_id == 0) | (j_id == cur) | (j_id == cur - 1)
        valid = j_id * SEL_BLOCK <= qpos
        for g in range(NSA_KV_HEADS):
            h0 = g * NSA_GROUP
            qg = jnp.concatenate([q[:, (h0 + h) * HEAD_DIM:(h0 + h + 1) * HEAD_DIM] for h in range(NSA_GROUP)], axis=0)
            qp = jnp.concatenate([qg, jnp.zeros_like(qg)], axis=1)
            kv = groups[g].astype(BF16)
            sc = _dot_nt(qp, kv).reshape(NSA_GROUP, t, nr) + bias_ref[h0:h0 + NSA_GROUP]
            sc = jnp.where(mask_c[None], sc, NEG)
            e = jnp.exp(sc - jnp.max(sc, axis=-1, keepdims=True))
            p = jnp.where(mask_c[None], e / jnp.sum(e, axis=-1, keepdims=True), 0.0)
            ocmp_ref[0, g * NSA_GROUP * t:(g + 1) * NSA_GROUP * t, :] = _dot(
                p.reshape(NSA_GROUP * t, nr).astype(BF16), kv)
            score = _dot_sel(jnp.sum(p, axis=0), a_ref[...])
            score = jnp.where(valid, jnp.where(forced, score + FORCE_BONUS, score), NEG)
            score_ref[0, g * t:(g + 1) * t, :] = jnp.where(j_id < nsb, score, PAD_SCORE)


def _smp_cmp(page_table, cache_cmp, kvc_new, qn, bias_c, a_mat, prm, pp):
    bd, n_pages = page_table.shape
    t = kvc_new.shape[1]
    past_chunks = n_pages * PAGE_SIZE // CMP_STRIDE
    nr = a_mat.shape[0]
    nbp = a_mat.shape[1]
    nsb = (past_chunks + SEL_BLOCK // CMP_STRIDE) // (SEL_BLOCK // CMP_STRIDE)
    rows = NSA_HEADS * t
    full = lambda a: pl.BlockSpec(a.shape, lambda b, s, pt, nd=a.ndim: (0,) * nd)
    consts = [prm["wlo"], prm["whi"], prm["wt"], prm["gkc"], prm["e64"], prm["e64t"], bias_c, a_mat]
    gs = pltpu.PrefetchScalarGridSpec(
        num_scalar_prefetch=1, grid=(bd, n_pages // pp),
        in_specs=_page_specs(pp, KV_W, PAGE_SIZE) + [pl.BlockSpec((1, t, KV_W), lambda b, s, pt: (b, 0, 0)),
                                          pl.BlockSpec((t, qn.shape[1]), lambda b, s, pt: (b, 0))]
        + [full(a) for a in consts],
        out_specs=[pl.BlockSpec((1, rows, LANE), lambda b, s, pt: (b, 0, 0)),
                   pl.BlockSpec((1, NSA_KV_HEADS * t, nbp), lambda b, s, pt: (b, 0, 0))],
        scratch_shapes=[pltpu.VMEM((nr, KV_W), F32), pltpu.VMEM((nr, KV_W), F32)])
    return pl.pallas_call(
        functools.partial(_smp_cmp_kernel, pp=pp, past_chunks=past_chunks, nsb=nsb), grid_spec=gs,
        out_shape=[jax.ShapeDtypeStruct((bd, rows, LANE), F32),
                   jax.ShapeDtypeStruct((bd, NSA_KV_HEADS * t, nbp), F32)],
        compiler_params=_cparams(("parallel", "arbitrary")), name="sample_cmp",
    )(page_table, *([cache_cmp] * pp), kvc_new, qn, *consts)


def _topk_mask_kernel(score_ref, exp_ref, o_ref, sel_ref, sc_ref, *, nsb):
    @pl.when(pl.program_id(0) == 0)
    def _():
        sc = score_ref[...].T
        sc_ref[...] = sc
        j_id = lax.broadcasted_iota(jnp.int32, sc.shape, 0)

        def rank_body(r, rank):
            sr = sc_ref[pl.ds(r, 1), :]
            ahead = (sr > sc) | ((sr == sc) & (r < j_id))
            return rank + ahead.astype(jnp.int32)

        rank = lax.fori_loop(0, nsb, rank_body, jnp.zeros(sc.shape, jnp.int32))
        sel_ref[...] = ((rank < min(N_SEL, nsb)) & (j_id < nsb)).astype(F32).T.astype(BF16)

    rows = sel_ref.shape[0]
    rc = min(512, rows)
    for r in range(0, rows, rc):
        o_ref[r:r + rc, :] = _dot(sel_ref[r:r + rc, :], exp_ref[...]).astype(BF16)


def _topk_mask(score2d, expand, nsb, tile):
    rows, nbp = score2d.shape
    keys = expand.shape[1]
    return pl.pallas_call(
        functools.partial(_topk_mask_kernel, nsb=nsb), grid=(keys // tile,),
        in_specs=[pl.BlockSpec((rows, nbp), lambda k: (0, 0)), pl.BlockSpec((nbp, tile), lambda k: (0, k))],
        out_specs=pl.BlockSpec((rows, tile), lambda k: (0, k)),
        out_shape=jax.ShapeDtypeStruct((rows, keys), BF16),
        scratch_shapes=[pltpu.VMEM((rows, nbp), BF16), pltpu.VMEM((nbp, rows), F32)],
        compiler_params=_cparams(("arbitrary",)), name="sample_topk_mask",
    )(score2d, expand)


def _rows_pad(a, rows):
    return jnp.concatenate([a, jnp.zeros((rows - a.shape[0], a.shape[1]), a.dtype)], axis=0)


def _smp_sel_kernel(pt_ref, *refs, pp):
    pages = refs[:pp]
    (mask_ref, maskt_ref, q_ref, new_ref, c31_ref, blast_ref, bnew_ref, o_ref, qp_ref, m_ref, l_ref, acc_ref) = refs[pp:]
    s = pl.program_id(1)
    t = q_ref.shape[0]
    rows = NSA_HEADS * t
    last = pl.num_programs(1) - 1

    @pl.when(s == 0)
    def _():
        qp_ref[...] = _group_queries(q_ref[...], KV_W)
        _softmax_init(m_ref, l_ref, acc_ref)

    qp = qp_ref[...]

    def expand_mask(mk, nk):
        m4 = jnp.broadcast_to(mk.astype(F32).reshape(NSA_KV_HEADS, 1, t, nk), (NSA_KV_HEADS, NSA_GROUP, t, nk))
        return m4.reshape(rows, nk) > 0.5

    keys = jnp.concatenate([pages[j][0] for j in range(pp)], axis=1).astype(BF16)
    nk = keys.shape[1]
    raw = _dot(qp, keys)
    mask = expand_mask(mask_ref[0], nk)
    nl = blast_ref.shape[1]

    @pl.when(s != last)
    def _():
        st = jnp.where(mask, raw + c31_ref[...], NEG)
        _softmax_update(st, mask, keys, m_ref, l_ref, acc_ref, v_feature_major=True)

    @pl.when(s == last)
    def _():
        bias = jnp.concatenate([jnp.broadcast_to(c31_ref[...], (rows, nk - nl)), blast_ref[...]], axis=1)
        st = jnp.where(mask, raw + bias, NEG)
        _softmax_update(st, mask, keys, m_ref, l_ref, acc_ref, v_feature_major=True)
        kn = _rows_pad(new_ref[0], LANE).astype(BF16)
        j_id = lax.broadcasted_iota(jnp.int32, (rows, LANE), 1)
        t_id = lax.broadcasted_iota(jnp.int32, (rows, LANE), 0) % t
        mt = expand_mask(maskt_ref[0][:, :LANE], LANE) & (j_id <= t_id) & (j_id < t)
        st = jnp.where(mt, _dot_nt(qp, kn) + bnew_ref[...], NEG)
        _softmax_update(st, mt, kn, m_ref, l_ref, acc_ref)
        o = acc_ref[...] / l_ref[...]
        half = KV_W // 2
        hr = NSA_GROUP * t
        o_ref[0] = jnp.concatenate([o[g * hr:(g + 1) * hr, half + g * HEAD_DIM:half + (g + 1) * HEAD_DIM]
                                    for g in range(NSA_KV_HEADS)], axis=0)


def _smp_sel(page_table, cache_slc, mask3, qn, kvs_new, c31r, blast, bnew, pp):
    bd, n_pages = page_table.shape
    t = kvs_new.shape[1]
    rows = NSA_HEADS * t
    nk = pp * PAGE_SIZE
    nsteps = n_pages // pp
    full = lambda a: pl.BlockSpec(a.shape, lambda b, s, pt, nd=a.ndim: (0,) * nd)
    gs = pltpu.PrefetchScalarGridSpec(
        num_scalar_prefetch=1, grid=(bd, nsteps),
        in_specs=_page_specs(pp, KV_W, PAGE_SIZE) + [
            pl.BlockSpec((1, NSA_KV_HEADS * t, nk), lambda b, s, pt: (b, 0, s)),
            pl.BlockSpec((1, NSA_KV_HEADS * t, nk), lambda b, s, pt: (b, 0, nsteps)),
            pl.BlockSpec((t, qn.shape[1]), lambda b, s, pt: (b, 0)),
            pl.BlockSpec((1, t, KV_W), lambda b, s, pt: (b, 0, 0)),
            full(c31r), full(blast), full(bnew)],
        out_specs=pl.BlockSpec((1, rows, HEAD_DIM), lambda b, s, pt: (b, 0, 0)),
        scratch_shapes=[pltpu.VMEM((rows, KV_W), BF16), pltpu.VMEM((rows, 1), F32), pltpu.VMEM((rows, 1), F32),
                        pltpu.VMEM((rows, KV_W), F32)])
    return pl.pallas_call(
        functools.partial(_smp_sel_kernel, pp=pp), grid_spec=gs,
        out_shape=jax.ShapeDtypeStruct((bd, rows, HEAD_DIM), F32),
        compiler_params=_cparams(("parallel", "arbitrary")), name="sample_sel",
    )(page_table, *([cache_slc] * pp), mask3, mask3, qn, kvs_new, c31r, blast, bnew)


def _smp_mla_kernel(pt_ref, *refs, pp):
    cpages = refs[:pp]
    rpages = refs[pp:2 * pp]
    (qa_ref, qr_ref, cnew_ref, rnew_ref, wuv_ref, o_ref, qa_s, qr_s, m_ref, l_ref, acc_ref) = refs[2 * pp:]
    s = pl.program_id(1)
    t = qa_ref.shape[0]
    rows = MLA_HEADS * t

    @pl.when(s == 0)
    def _():
        qa_s[...] = jnp.concatenate([qa_ref[:, h * KV_LORA:(h + 1) * KV_LORA] for h in range(MLA_HEADS)], axis=0)
        qr_s[...] = jnp.concatenate([qr_ref[:, h * LANE:(h + 1) * LANE] for h in range(MLA_HEADS)], axis=0)
        _softmax_init(m_ref, l_ref, acc_ref)

    qa, qr = qa_s[...], qr_s[...]
    ck = jnp.concatenate([cpages[j][0] for j in range(pp)], axis=0).astype(BF16)
    kr = jnp.concatenate([rpages[j][0] for j in range(pp)], axis=1).astype(BF16)
    st = _dot_nt(qa, ck) + _dot(qr[:, :QK_ROPE], kr)
    _softmax_update(st, None, ck, m_ref, l_ref, acc_ref)

    @pl.when(s == pl.num_programs(1) - 1)
    def _():
        cn = _rows_pad(cnew_ref[0], LANE)
        rn = _rows_pad(rnew_ref[0], LANE)
        j_id = lax.broadcasted_iota(jnp.int32, (rows, LANE), 1)
        t_id = lax.broadcasted_iota(jnp.int32, (rows, LANE), 0) % t
        mt = (j_id <= t_id) & (j_id < t)
        sn = jnp.where(mt, _dot_nt(qa, cn) + _dot_nt(qr, rn), NEG)
        _softmax_update(sn, mt, cn, m_ref, l_ref, acc_ref)
        o_lat = (acc_ref[...] / l_ref[...]).astype(BF16)
        o_ref[...] = jnp.concatenate([_dot(o_lat[h * t:(h + 1) * t], wuv_ref[h]) for h in range(MLA_HEADS)],
                                     axis=1).astype(BF16)


def _smp_mla(page_table, cache_ckv, cache_kr, qa, qr, ckvb_new, krp_new, wuv, pp):
    bd, n_pages = page_table.shape
    t = ckvb_new.shape[1]
    rows = MLA_HEADS * t
    gs = pltpu.PrefetchScalarGridSpec(
        num_scalar_prefetch=1, grid=(bd, n_pages // pp),
        in_specs=_page_specs(pp, PAGE_SIZE, KV_LORA) + _page_specs(pp, QK_ROPE, PAGE_SIZE) + [
            pl.BlockSpec((t, qa.shape[1]), lambda b, s, pt: (b, 0)),
            pl.BlockSpec((t, qr.shape[1]), lambda b, s, pt: (b, 0)),
            pl.BlockSpec((1, t, KV_LORA), lambda b, s, pt: (b, 0, 0)),
            pl.BlockSpec((1, t, LANE), lambda b, s, pt: (b, 0, 0)),
            pl.BlockSpec(wuv.shape, lambda b, s, pt: (0, 0, 0))],
        out_specs=pl.BlockSpec((t, MLA_HEADS * V_DIM), lambda b, s, pt: (b, 0)),
        scratch_shapes=[pltpu.VMEM((rows, KV_LORA), BF16), pltpu.VMEM((rows, LANE), BF16),
                        pltpu.VMEM((rows, 1), F32), pltpu.VMEM((rows, 1), F32), pltpu.VMEM((rows, KV_LORA), F32)])
    return pl.pallas_call(
        functools.partial(_smp_mla_kernel, pp=pp), grid_spec=gs,
        out_shape=jax.ShapeDtypeStruct((bd * t, MLA_HEADS * V_DIM), BF16),
        compiler_params=_cparams(("parallel", "arbitrary")), name="sample_mla",
    )(page_table, *([cache_ckv] * pp), *([cache_kr] * pp), qa, qr, ckvb_new, krp_new, wuv)


def _smp_win_kernel(state_ref, new_ref, q_ref, misc_ref, ocmp_ref, osel_ref, bias_ref, o_ref, win_ref):
    t = q_ref.shape[0]
    wl = state_ref.shape[1]
    rows = NSA_HEADS * t
    state = state_ref[0]
    new = new_ref[0]
    win_ref[0, :wl - t, :] = state[t:, :]
    win_ref[0, wl - t:, :] = new
    nkp = bias_ref.shape[1]
    ctx = jnp.concatenate([state, new, jnp.zeros((nkp - wl - t, KV_W), F32)], axis=0).astype(BF16)
    qp = _group_queries(q_ref[...], KV_W)
    j_id = lax.broadcasted_iota(jnp.int32, (rows, nkp), 1)
    t_id = lax.broadcasted_iota(jnp.int32, (rows, nkp), 0) % t
    dist = wl + t_id - j_id
    mask = (dist >= 0) & (dist < WINDOW) & (j_id < wl + t)
    s = jnp.where(mask, _dot_nt(qp, ctx) + bias_ref[...], NEG)
    e = jnp.exp(s - jnp.max(s, axis=-1, keepdims=True))
    p = jnp.where(mask, e / jnp.sum(e, axis=-1, keepdims=True), 0.0)
    o_win = _dot(p.astype(BF16), ctx)
    misc = misc_ref[...]
    ocmp = ocmp_ref[0]
    osel = osel_ref[0]
    half = KV_W // 2
    pieces = []
    for hh in range(NSA_HEADS):
        g = hh // NSA_GROUP
        rs = slice(hh * t, (hh + 1) * t)
        gc = misc[:, MISC_GATE_OFF + hh:MISC_GATE_OFF + hh + 1]
        gsel = misc[:, MISC_GATE_OFF + NSA_HEADS + hh:MISC_GATE_OFF + NSA_HEADS + hh + 1]
        gw = misc[:, MISC_GATE_OFF + 2 * NSA_HEADS + hh:MISC_GATE_OFF + 2 * NSA_HEADS + hh + 1]
        pieces.append(gc * ocmp[rs, HEAD_DIM:] + gsel * osel[rs, :]
                      + gw * o_win[rs, half + g * HEAD_DIM:half + (g + 1) * HEAD_DIM])
    o_ref[...] = jnp.concatenate(pieces, axis=1).astype(BF16)


def _smp_win(state, kvw_new, qn, misc, ocmp, osel, bias_w):
    bd, wl, _ = state.shape
    t = kvw_new.shape[1]
    rows = NSA_HEADS * t
    nq = NSA_HEADS * HEAD_DIM
    return pl.pallas_call(
        _smp_win_kernel, grid=(bd,),
        in_specs=[pl.BlockSpec((1, wl, KV_W), lambda b: (b, 0, 0)), pl.BlockSpec((1, t, KV_W), lambda b: (b, 0, 0)),
                  pl.BlockSpec((t, nq), lambda b: (b, 0)), pl.BlockSpec((t, LANE), lambda b: (b, 0)),
                  pl.BlockSpec((1, rows, LANE), lambda b: (b, 0, 0)), pl.BlockSpec((1, rows, HEAD_DIM), lambda b: (b, 0, 0)),
                  pl.BlockSpec(bias_w.shape, lambda b: (0, 0))],
        out_specs=[pl.BlockSpec((t, nq), lambda b: (b, 0)), pl.BlockSpec((1, wl, KV_W), lambda b: (b, 0, 0))],
        out_shape=[jax.ShapeDtypeStruct((bd * t, nq), BF16), jax.ShapeDtypeStruct((bd, wl, KV_W), F32)],
        compiler_params=_cparams(("parallel",)), name="sample_window_merge",
    )(state, kvw_new, qn, misc, ocmp, osel, bias_w)


def _prep_params(d, attn_norm, w_in, g_q_nsa, g_k_cmp, g_k_slc, g_k_win, phi_cmp, g_cq, w_uq, g_q_mla, g_ckv,
                 g_krope, w_uk, w_uv, w_proj_nsa, w_proj_mla, w_out, ffn_norm, w_gate, w_up, w_down):
    nq = NSA_HEADS * HEAD_DIM
    widths = (nq, KV_W, KV_W, KV_W, GATE_N_W, Q_LORA, KV_LORA, QK_ROPE, d, d)
    names = ("q", "kvc", "kvs", "kvw", "gn", "cq", "ckv", "kr", "ga", "gb")
    splits = np.cumsum(widths)[:-1]
    seg = dict(zip(names, jnp.split(w_in, splits, axis=1)))
    zoff = _z_layout(d)
    tn = 1024 if d >= 1024 else 256
    zw = _round_up(zoff["end"], tn)
    zc = lambda w: jnp.zeros((d, w), w_in.dtype)
    cols = [seg["q"], seg["cq"], seg["kr"], seg["gn"], zc(zoff["ga"] - zoff["misc"] - QK_ROPE - GATE_N_W),
            seg["ga"], seg["gb"], zc(zoff["kv"] - zoff["gb"] - d), seg["kvc"], seg["kvs"], seg["kvw"], seg["ckv"],
            zc(zw - zoff["end"])]
    prm = {"w_cat": jnp.concatenate(cols, axis=1).astype(BF16), "tn": tn}
    prm["g_attn"] = attn_norm.reshape(1, d)
    prm["gq"] = jnp.tile(g_q_nsa, NSA_HEADS).reshape(1, nq)
    prm["gks"] = jnp.tile(g_k_slc, NSA_KV_HEADS).reshape(1, LANE)
    prm["gkw"] = jnp.tile(g_k_win, NSA_KV_HEADS).reshape(1, LANE)
    prm["gkc"] = jnp.tile(g_k_cmp, NSA_KV_HEADS).reshape(1, LANE)
    prm["gcq"] = g_cq.reshape(1, Q_LORA)
    wq = w_uq.reshape(Q_LORA, MLA_HEADS, QK_NOPE + QK_ROPE)
    w_rope = jnp.pad(wq[:, :, QK_NOPE:], ((0, 0), (0, 0), (0, LANE - QK_ROPE)))
    prm["wuq"] = jnp.concatenate([wq[:, :, :QK_NOPE].reshape(Q_LORA, MLA_HEADS * QK_NOPE),
                                  w_rope.reshape(Q_LORA, MLA_HEADS * LANE)], axis=1).astype(BF16)
    prm["gmn"] = jnp.tile(g_q_mla[:QK_NOPE], MLA_HEADS).reshape(1, MLA_HEADS * QK_NOPE)
    prm["gmr"] = jnp.tile(jnp.pad(g_q_mla[QK_NOPE:], (0, LANE - QK_ROPE)), MLA_HEADS).reshape(1, MLA_HEADS * LANE)
    prm["gckv"] = g_ckv.reshape(1, KV_LORA)
    prm["gkr"] = jnp.pad(g_krope, (0, LANE - QK_ROPE)).reshape(1, LANE)
    prm["wuk"] = w_uk.astype(BF16)
    prm["wuv"] = w_uv.astype(BF16)
    e64 = _seg_indicator(nq, HEAD_DIM)
    e128 = _seg_indicator(MLA_HEADS * LANE, LANE)
    prm["e64"], prm["e64t"] = jnp.asarray(e64, BF16), jnp.asarray(e64.T, BF16)
    prm["e128"], prm["e128t"] = jnp.asarray(e128, BF16), jnp.asarray(e128.T, BF16)
    ph = jnp.repeat(phi_cmp, KV_W // 2, axis=0).T
    prm["wlo"], prm["whi"] = ph[:CMP_STRIDE], ph[CMP_STRIDE:]
    eye = jnp.eye(PAGE_SIZE // CMP_STRIDE, dtype=F32)
    prm["wt"] = jnp.concatenate([jnp.kron(eye, phi_cmp[e, part * CMP_STRIDE:(part + 1) * CMP_STRIDE][None, :])
                                 for e in range(2) for part in range(2)], axis=0)
    prm["wpn"], prm["wpm"], prm["wo"] = w_proj_nsa.astype(BF16), w_proj_mla.astype(BF16), w_out.astype(BF16)
    prm["gf"] = ffn_norm.reshape(1, d)
    prm["wg"], prm["wu"], prm["wd"] = w_gate.astype(BF16), w_up.astype(BF16), w_down.astype(BF16)
    return prm, zoff


def _rope_tables(pos):
    half = QK_ROPE // 2
    inv_freq = ROPE_THETA ** (-jnp.arange(half, dtype=jnp.float32) / half)
    ang = pos.astype(jnp.float32)[:, None] * inv_freq
    cos, sin = jnp.cos(ang), jnp.sin(ang)
    z = jnp.zeros((pos.shape[0], LANE - QK_ROPE), F32)
    zh = jnp.zeros_like(sin)
    return (jnp.concatenate([cos, cos, z], axis=1), jnp.concatenate([zh, sin, z], axis=1),
            jnp.concatenate([-sin, zh, z], axis=1))


def _layer(xp, xs, cache_cmp, cache_slc, cache_ckv, cache_kr, win_state, page_table, rel_bias_table, weights):
    b, s, d = xp.shape
    bd, t, _ = xs.shape
    n_pages = page_table.shape[1]
    past = n_pages * PAGE_SIZE
    n_pool = cache_cmp.shape[0]
    wl = win_state.shape[1]
    prm, zoff = _prep_params(d, *weights)
    pp = max(1, min(16, n_pages // 2))
    nsteps = n_pages // pp

    qb = Q_BLOCK
    ar = np.arange
    d_t01 = (ar(qb)[None, :, None] - ar(qb)[None, None, :]) + qb * ar(2)[:, None, None]
    bw = 32
    d_bwin = ar(qb)[:, None] - CMP_STRIDE * (ar(bw)[None, :] - (bw - qb // CMP_STRIDE)) - (CMP_BLOCK - 1)
    d_c31 = np.full((LANE,), MAX_DISTANCE)
    past_chunks = past // CMP_STRIDE
    nc_s = past_chunks + SEL_BLOCK // CMP_STRIDE
    nr_s = _round_up(nc_s, LANE)
    nsb = nc_s // (SEL_BLOCK // CMP_STRIDE)
    nbp = _round_up(nsb, LANE)
    d_cmp_s = past + ar(t)[:, None] - CMP_STRIDE * ar(nr_s)[None, :] - (CMP_BLOCK - 1)
    nl = min(2 * LANE, pp * PAGE_SIZE)
    d_last = nl + ar(t)[:, None] - ar(nl)[None, :]
    d_new = ar(t)[:, None] - ar(LANE)[None, :]
    nkp = _round_up(wl + t, LANE)
    d_win = wl + ar(t)[:, None] - ar(nkp)[None, :]
    t01, bwin, c31, bias_c, blast, bnew, bias_w = _bias_tables(
        rel_bias_table, [d_t01, d_bwin, d_c31, d_cmp_s, d_last, d_new, d_win])
    t01t = t01.transpose(1, 3, 0, 2).reshape(2, qb, NSA_HEADS * qb)
    bwint = bwin.transpose(2, 0, 1).reshape(bw, NSA_HEADS * qb)
    c31row = jnp.repeat(c31[:, 0], qb).reshape(1, NSA_HEADS * qb)
    rows_s = NSA_HEADS * t
    c31r = jnp.repeat(c31, t, axis=0)[:, :1] * jnp.ones((1, 1), F32)
    blast = blast.reshape(rows_s, nl)
    bnew = bnew.reshape(rows_s, LANE)
    bias_w = bias_w.reshape(rows_s, nkp)
    at = jnp.asarray(_score_matrix(s // CMP_STRIDE, s // SEL_BLOCK).T, BF16)
    a_s = np.zeros((nr_s, nbp), np.float32)
    a_s[:nc_s] = _score_matrix(nc_s, nbp)
    a_s = jnp.asarray(a_s, BF16)
    keys_pad = (nsteps + 1) * pp * PAGE_SIZE
    expand = np.zeros((nbp, keys_pad), np.float32)
    kk = ar(nsb * SEL_BLOCK)
    expand[kk // SEL_BLOCK, kk] = 1.0
    expand = jnp.asarray(expand, BF16)

    pos_p = jnp.arange(s)
    tabs_p = tuple(jnp.tile(a, (b, 1)) for a in _rope_tables(pos_p))
    zp = _proj(xp.reshape(b * s, d), prm["g_attn"], prm["w_cat"], prm["tn"])
    pp_ = _post(zp, zoff, tabs_p, prm, True)
    kvcg, kvcgt = _cmp_prompt(pp_["kvc"].reshape(b, s, KV_W), prm)
    o_nsa_p = _nsa_prompt(pp_["qnt"], pp_["misc"], kvcg, kvcgt, pp_["kvsg"], pp_["kvsgt"], pp_["kvwg"], pp_["kvwgt"],
                          t01t, bwint, c31row, at, b, s)
    o_mla_p = _mla_prompt(pp_["qa"], pp_["qr"], pp_["ckvb"], pp_["krp"], prm["wuv"], b, s)
    x1p, h2p = _outproj(xp.reshape(b * s, d), o_nsa_p.reshape(b * s, -1), o_mla_p.reshape(b * s, -1), zp, zoff, prm)
    yp = _ffn(h2p, x1p, prm).reshape(b, s, d)

    pos_s = past + jnp.arange(t)
    tabs_s = tuple(jnp.tile(a, (bd, 1)) for a in _rope_tables(pos_s))
    zs = _proj(xs.reshape(bd * t, d), prm["g_attn"], prm["w_cat"], prm["tn"])
    ps = _post(zs, zoff, tabs_s, prm, False)
    cc = _feature_major_pages(cache_cmp)
    cs = _feature_major_pages(cache_slc)
    cache_kr = _feature_major_pages(cache_kr)
    ocmp_s, score = _smp_cmp(page_table, cc, ps["kvc"].reshape(bd, t, KV_W), ps["qn"], bias_c, a_s, prm, pp)
    mask = _topk_mask(score.reshape(bd * NSA_KV_HEADS * t, nbp), expand, nsb, pp * PAGE_SIZE)
    osel_s = _smp_sel(page_table, cs, mask.reshape(bd, NSA_KV_HEADS * t, keys_pad), ps["qn"],
                      ps["kvs"].reshape(bd, t, KV_W), c31r, blast, bnew, pp)
    o_mla_s = _smp_mla(page_table, cache_ckv, cache_kr, ps["qa"], ps["qr"], ps["ckvb"].reshape(bd, t, KV_LORA),
                       ps["krp"].reshape(bd, t, LANE), prm["wuv"], pp)
    o_nsa_s, win_s = _smp_win(win_state.reshape(bd, wl, KV_W), ps["kvw"].reshape(bd, t, KV_W), ps["qn"], ps["misc"],
                              ocmp_s, osel_s, bias_w)
    x1s, h2s = _outproj(xs.reshape(bd * t, d), o_nsa_s, o_mla_s, zs, zoff, prm)
    ys = _ffn(h2s, x1s, prm).reshape(bd, t, d)

    kvh = (2, NSA_KV_HEADS, HEAD_DIM)
    wp = min(WINDOW, s)
    return (yp, ys,
            pp_["kvc"].reshape((b, s) + kvh), ps["kvc"].reshape((bd, t) + kvh),
            pp_["kvs"].reshape((b, s) + kvh), ps["kvs"].reshape((bd, t) + kvh),
            pp_["ckv"].reshape(b, s, KV_LORA), ps["ckv"].reshape(bd, t, KV_LORA),
            pp_["misc"][:, :QK_ROPE].reshape(b, s, QK_ROPE), ps["misc"][:, :QK_ROPE].reshape(bd, t, QK_ROPE),
            pp_["kvw"].reshape((b, s) + kvh)[:, s - wp:], win_s.reshape((bd, wl) + kvh))


def kernel(x_prompt, x_sample, cache_cmp_kv, cache_slc_kv, cache_mla_ckv, cache_mla_krope, state_win_kv, page_table, rel_bias_table, attn_norm, w_in, g_q_nsa, g_k_cmp, g_k_slc, g_k_win, phi_cmp, g_cq, w_uq, g_q_mla, g_ckv, g_krope, w_uk, w_uv, w_proj_nsa, w_proj_mla, w_out, ffn_norm, w_gate, w_up, w_down):
    depth = w_in.shape[0]
    xp, xs = x_prompt, x_sample
    per_layer = []
    for l in range(depth):
        weights = (attn_norm[l], w_in[l], g_q_nsa[l], g_k_cmp[l], g_k_slc[l], g_k_win[l], phi_cmp[l], g_cq[l],
                   w_uq[l], g_q_mla[l], g_ckv[l], g_krope[l], w_uk[l], w_uv[l], w_proj_nsa[l], w_proj_mla[l],
                   w_out[l], ffn_norm[l], w_gate[l], w_up[l], w_down[l])
        xp, xs, *st = _layer(xp, xs, cache_cmp_kv[l], cache_slc_kv[l], cache_mla_ckv[l], cache_mla_krope[l],
                             state_win_kv[l], page_table, rel_bias_table, weights)
        per_layer.append(st)
    stacked = [jnp.stack(v) for v in zip(*per_layer)]
    return (xp, xs, *stacked)
```

```python
import functools
import math

import numpy as np
import jax
import jax.numpy as jnp
from jax import lax
from jax.experimental import pallas as pl
from jax.experimental.pallas import tpu as pltpu

F32 = jnp.float32
BF16 = jnp.bfloat16

NSA_HEADS = 16
NSA_KV_HEADS = 2
NSA_GROUP = NSA_HEADS // NSA_KV_HEADS
HEAD_DIM = 64
CMP_BLOCK = 32
CMP_STRIDE = 16
SEL_BLOCK = 64
N_SEL = 16
WINDOW = 512
MLA_HEADS = 16
QK_NOPE = 64
QK_ROPE = 32
V_DIM = 64
KV_LORA = 256
Q_LORA = 512
ROPE_THETA = 10000.0
N_BUCKETS = 32
MAX_DISTANCE = 128
PAGE_SIZE = 128
Q_BLOCK = 128
EPS = 1e-6
NEG = -1e30
PAD_SCORE = -3e38
FORCE_BONUS = 1e6

LANE = 128
KV_W = 2 * NSA_KV_HEADS * HEAD_DIM
GATE_N_W = 3 * NSA_HEADS
MISC_GATE_OFF = QK_ROPE
VMEM_LIMIT = 56 * 1024 * 1024


def _round_up(a, b):
    return -(-a // b) * b


def _cparams(sem):
    return pltpu.CompilerParams(dimension_semantics=sem, vmem_limit_bytes=VMEM_LIMIT)


def _dot(a, b):
    return jnp.dot(a, b, preferred_element_type=F32)


def _dot_nt(a, b):
    return lax.dot_general(a, b, (((1,), (1,)), ((), ())), preferred_element_type=F32)


def _split3(x):
    hi = x.astype(BF16)
    r = x - hi.astype(F32)
    mid = r.astype(BF16)
    lo = (r - mid.astype(F32)).astype(BF16)
    return hi, mid, lo


def _dot_sel(x, m):
    hi, mid, lo = _split3(x)
    return _dot(hi, m) + _dot(mid, m) + _dot(lo, m)


def _dot_sel_nt(m, x):
    hi, mid, lo = _split3(x)
    return _dot_nt(m, hi) + _dot_nt(m, mid) + _dot_nt(m, lo)


def _seg_rinv(x, e, et, width):
    ss = _dot_sel(x * x, e)
    r = lax.rsqrt(ss / width + EPS)
    return _dot_sel(r, et)


def _t5_bucket_np(dist):
    n = np.maximum(dist, 0)
    exact = N_BUCKETS // 2
    log_ratio = np.log(np.maximum(n, 1).astype(np.float32) / exact) / math.log(MAX_DISTANCE / exact)
    large = np.minimum(exact + (log_ratio * (N_BUCKETS - exact)).astype(np.int32), N_BUCKETS - 1)
    return np.where(n < exact, n, large)


def _seg_indicator(width, seg):
    e = np.zeros((width, LANE), np.float32)
    e[np.arange(width), np.arange(width) // seg] = 1.0
    return e


def _score_matrix(nc, nb_pad):
    ratio = SEL_BLOCK // CMP_STRIDE
    a = np.zeros((nc, nb_pad), np.float32)
    c = np.arange(nc)
    a[c, c // ratio] = 1.0
    cc = c[(c % ratio == ratio - 1) & (c // ratio + 1 < nc // ratio)]
    a[cc, cc // ratio + 1] = 1.0
    return a


def _bias_expand_kernel(tt_ref, oh_ref, o_ref):
    hi, mid, lo = _split3(tt_ref[...])
    oh = oh_ref[...]
    o_ref[...] = _dot(hi, oh) + _dot(mid, oh) + _dot(lo, oh)


def _bias_tables(rel_bias_table, dist_list):
    sizes = [_round_up(d.size, LANE) for d in dist_list]
    tile = 2048
    total = _round_up(sum(sizes), tile)
    onehot = np.zeros((N_BUCKETS, total), np.float32)
    off = 0
    for d, sz in zip(dist_list, sizes):
        b = _t5_bucket_np(d.reshape(-1))
        onehot[b, off + np.arange(d.size)] = 1.0
        off += sz
    out = pl.pallas_call(
        _bias_expand_kernel,
        grid=(total // tile,),
        in_specs=[pl.BlockSpec((NSA_HEADS, N_BUCKETS), lambda i: (0, 0)),
                  pl.BlockSpec((N_BUCKETS, tile), lambda i: (0, i))],
        out_specs=pl.BlockSpec((NSA_HEADS, tile), lambda i: (0, i)),
        out_shape=jax.ShapeDtypeStruct((NSA_HEADS, total), F32),
        compiler_params=_cparams(("arbitrary",)),
        name="bias_expand",
    )(rel_bias_table.T, jnp.asarray(onehot, BF16))
    res, off = [], 0
    for d, sz in zip(dist_list, sizes):
        res.append(out[:, off:off + d.size].reshape((NSA_HEADS,) + d.shape))
        off += sz
    return res


def _z_layout(d_model):
    off = {}
    off["q"] = 0
    off["cq"] = NSA_HEADS * HEAD_DIM
    off["misc"] = off["cq"] + Q_LORA
    pos = _round_up(off["misc"] + LANE, d_model)
    off["ga"] = pos
    off["gb"] = pos + d_model
    pos = _round_up(off["gb"] + d_model, 3 * KV_W)
    off["kv"] = pos
    off["ckv"] = off["kv"] + 3 * KV_W
    off["end"] = off["ckv"] + KV_LORA
    return off


def _proj_kernel(x_ref, g_ref, w_ref, z_ref, h_ref):
    @pl.when(pl.program_id(1) == 0)
    def _():
        x = x_ref[...]
        ms = jnp.mean(x * x, axis=-1, keepdims=True)
        h_ref[...] = (x * lax.rsqrt(ms + EPS) * g_ref[...]).astype(BF16)

    z_ref[...] = _dot(h_ref[...], w_ref[...])


def _proj(x2d, g_norm, w_cat, tn):
    n, d = x2d.shape
    zw = w_cat.shape[1]
    tm = min(512, n)
    return pl.pallas_call(
        _proj_kernel,
        grid=(n // tm, zw // tn),
        in_specs=[pl.BlockSpec((tm, d), lambda i, j: (i, 0)),
                  pl.BlockSpec((1, d), lambda i, j: (0, 0)),
                  pl.BlockSpec((d, tn), lambda i, j: (0, j))],
        out_specs=pl.BlockSpec((tm, tn), lambda i, j: (i, j)),
        out_shape=jax.ShapeDtypeStruct((n, zw), F32),
        scratch_shapes=[pltpu.VMEM((tm, d), BF16)],
        compiler_params=_cparams(("parallel", "arbitrary")),
        name="in_proj",
    )(x2d, g_norm, w_cat)


def _post_kernel(zq_ref, zcq_ref, zkv_ref, zckv_ref, zmisc_ref, cos_ref, sina_ref, sinb_ref,
                 gq_ref, gks_ref, gkw_ref, gcq_ref, wuq_ref, gmn_ref, gmr_ref, gckv_ref, gkr_ref, wuk_ref,
                 e64_ref, e64t_ref, e128_ref, e128t_ref,
                 qn_ref, kvc_ref, kvs_ref, kvw_ref, kvsg_ref, kvwg_ref,
                 ckv_ref, ckvb_ref, misc_ref, krp_ref, qa_ref, qr_ref, *feature_major_refs):
    e64, e64t = e64_ref[...], e64t_ref[...]
    e128, e128t = e128_ref[...], e128t_ref[...]
    nq = NSA_HEADS * HEAD_DIM

    zq = zq_ref[...]
    qn = zq * _seg_rinv(zq, e64, e64t, HEAD_DIM) * gq_ref[...] * (HEAD_DIM ** -0.5)
    qn_ref[...] = qn.astype(BF16)
    if feature_major_refs:
        feature_major_refs[0][...] = qn.T.astype(BF16)

    zkv = zkv_ref[...]
    kvc_ref[...] = zkv[:, :KV_W]
    half = KV_W // 2
    for idx, (g_ref, out_ref, outg_ref) in enumerate(((gks_ref, kvs_ref, kvsg_ref), (gkw_ref, kvw_ref, kvwg_ref))):
        kv = zkv[:, (idx + 1) * KV_W:(idx + 2) * KV_W]
        k, v = kv[:, :half], kv[:, half:]
        kn = k * _seg_rinv(k, e64[:half], e64t[:, :half], HEAD_DIM) * g_ref[...]
        out_ref[...] = jnp.concatenate([kn, v], axis=1)
        for g in range(NSA_KV_HEADS):
            sl = slice(g * HEAD_DIM, (g + 1) * HEAD_DIM)
            kvg = jnp.concatenate([kn[:, sl], v[:, sl]], axis=1)
            outg_ref[g] = kvg.astype(BF16)
            if feature_major_refs:
                for sub in range(kvg.shape[0] // Q_BLOCK):
                    feature_major_refs[1 + idx][g, sub] = kvg[sub * Q_BLOCK:(sub + 1) * Q_BLOCK].T.astype(BF16)

    zc = zckv_ref[...]
    ckv = zc * lax.rsqrt(jnp.mean(zc * zc, axis=-1, keepdims=True) + EPS) * gckv_ref[...]
    ckv_ref[...] = ckv
    ckvb_ref[...] = ckv.astype(BF16)

    cos, sina, sinb = cos_ref[...], sina_ref[...], sinb_ref[...]
    zm = zmisc_ref[...]
    lane = lax.broadcasted_iota(jnp.int32, zm.shape, 1)
    is_kr = lane < QK_ROPE
    ms = jnp.sum(jnp.where(is_kr, zm * zm, 0.0), axis=-1, keepdims=True) / QK_ROPE
    krn = zm * lax.rsqrt(ms + EPS) * gkr_ref[...]
    half_r = QK_ROPE // 2
    kr = krn * cos + pltpu.roll(krn, half_r, 1) * sina + pltpu.roll(krn, LANE - half_r, 1) * sinb
    kr = jnp.where(is_kr, kr, 0.0)
    gates = jnp.where(lane < MISC_GATE_OFF + GATE_N_W, jax.nn.sigmoid(zm), 0.0)
    misc_ref[...] = jnp.where(is_kr, kr, gates)
    krp_ref[...] = kr.astype(BF16)

    zcq = zcq_ref[...]
    cqn = (zcq * lax.rsqrt(jnp.mean(zcq * zcq, axis=-1, keepdims=True) + EPS) * gcq_ref[...]).astype(BF16)
    qm = _dot(cqn, wuq_ref[...])
    nope, ropep = qm[:, :nq], qm[:, nq:]
    ss = _dot_sel(nope * nope, e64) + _dot_sel(ropep * ropep, e128)
    r = lax.rsqrt(ss / (QK_NOPE + QK_ROPE) + EPS)
    scale = (QK_NOPE + QK_ROPE) ** -0.5
    nope_n = nope * _dot_sel(r, e64t) * gmn_ref[...] * scale
    rope_n = ropep * _dot_sel(r, e128t) * gmr_ref[...] * scale
    wr = MLA_HEADS * LANE
    cos_t = jnp.concatenate([cos] * MLA_HEADS, axis=1)
    sina_t = jnp.concatenate([sina] * MLA_HEADS, axis=1)
    sinb_t = jnp.concatenate([sinb] * MLA_HEADS, axis=1)
    qr = rope_n * cos_t + pltpu.roll(rope_n, half_r, 1) * sina_t + pltpu.roll(rope_n, wr - half_r, 1) * sinb_t
    qr_ref[...] = qr.astype(BF16)
    nb = nope_n.astype(BF16)
    for h in range(MLA_HEADS):
        qa = _dot_nt(nb[:, h * QK_NOPE:(h + 1) * QK_NOPE], wuk_ref[h])
        qa_ref[:, h * KV_LORA:(h + 1) * KV_LORA] = qa.astype(BF16)


def _post(z, zoff, tabs, prm, feature_major):
    n = z.shape[0]
    tm = min(256, n)
    nq = NSA_HEADS * HEAD_DIM
    row = lambda w, c: pl.BlockSpec((tm, w), lambda i, c=c: (i, c))
    full = lambda a: pl.BlockSpec(a.shape, lambda i, nd=a.ndim: (0,) * nd)
    cos, sina, sinb = tabs
    consts = [prm["gq"], prm["gks"], prm["gkw"], prm["gcq"], prm["wuq"], prm["gmn"], prm["gmr"], prm["gckv"],
              prm["gkr"], prm["wuk"], prm["e64"], prm["e64t"], prm["e128"], prm["e128t"]]
    in_specs = [row(nq, zoff["q"] // nq), row(Q_LORA, zoff["cq"] // Q_LORA), row(3 * KV_W, zoff["kv"] // (3 * KV_W)),
                row(KV_LORA, zoff["ckv"] // KV_LORA), row(LANE, zoff["misc"] // LANE),
                row(LANE, 0), row(LANE, 0), row(LANE, 0)] + [full(a) for a in consts]
    sds = jax.ShapeDtypeStruct
    out_shape = [sds((n, nq), BF16), sds((n, KV_W), F32), sds((n, KV_W), F32), sds((n, KV_W), F32),
                 sds((NSA_KV_HEADS, n, LANE), BF16), sds((NSA_KV_HEADS, n, LANE), BF16),
                 sds((n, KV_LORA), F32), sds((n, KV_LORA), BF16), sds((n, LANE), F32), sds((n, LANE), BF16),
                 sds((n, MLA_HEADS * KV_LORA), BF16), sds((n, MLA_HEADS * LANE), BF16)]
    grp = pl.BlockSpec((NSA_KV_HEADS, tm, LANE), lambda i: (0, i, 0))
    out_specs = [row(nq, 0), row(KV_W, 0), row(KV_W, 0), row(KV_W, 0), grp, grp,
                 row(KV_LORA, 0), row(KV_LORA, 0), row(LANE, 0), row(LANE, 0),
                 row(MLA_HEADS * KV_LORA, 0), row(MLA_HEADS * LANE, 0)]
    names = ["qn", "kvc", "kvs", "kvw", "kvsg", "kvwg", "ckv", "ckvb", "misc", "krp", "qa", "qr"]
    if feature_major:
        tiles = (NSA_KV_HEADS, n // Q_BLOCK, LANE, Q_BLOCK)
        grpt = pl.BlockSpec((NSA_KV_HEADS, tm // Q_BLOCK, LANE, Q_BLOCK), lambda i: (0, i, 0, 0))
        out_shape += [sds((nq, n), BF16), sds(tiles, BF16), sds(tiles, BF16)]
        out_specs += [pl.BlockSpec((nq, tm), lambda i: (0, i)), grpt, grpt]
        names += ["qnt", "kvsgt", "kvwgt"]
    outs = pl.pallas_call(
        _post_kernel, grid=(n // tm,), in_specs=in_specs, out_specs=out_specs, out_shape=out_shape,
        compiler_params=_cparams(("parallel",)), name="post_proj",
    )(z, z, z, z, z, cos, sina, sinb, *consts)
    return dict(zip(names, outs))


def _chunk_partials(rows, wlo, whi):
    ch = rows.reshape(rows.shape[0] // CMP_STRIDE, CMP_STRIDE, rows.shape[1])
    return jnp.sum(ch * wlo[None], axis=1), jnp.sum(ch * whi[None], axis=1)


def _blocks_to_groups(lo, hi, gkc, e64, e64t):
    nr = lo.shape[0]
    rid = lax.broadcasted_iota(jnp.int32, hi.shape, 0)
    hi_next = jnp.where(rid < nr - 1, pltpu.roll(hi, nr - 1, 0), 0.0)
    blk = lo + hi_next
    half = KV_W // 2
    k, v = blk[:, :half], blk[:, half:]
    kn = k * _seg_rinv(k, e64[:half], e64t[:, :half], HEAD_DIM) * gkc
    out = []
    for g in range(NSA_KV_HEADS):
        sl = slice(g * HEAD_DIM, (g + 1) * HEAD_DIM)
        out.append(jnp.concatenate([kn[:, sl], v[:, sl]], axis=1))
    return out


def _cmp_prompt_kernel(kvc_ref, wlo_ref, whi_ref, gkc_ref, e64_ref, e64t_ref, o_ref, ot_ref):
    lo, hi = _chunk_partials(kvc_ref[0], wlo_ref[...], whi_ref[...])
    groups = _blocks_to_groups(lo, hi, gkc_ref[...], e64_ref[...], e64t_ref[...])
    for g in range(NSA_KV_HEADS):
        o_ref[0, g] = groups[g].astype(BF16)
        ot_ref[0, g] = groups[g].T.astype(BF16)


def _cmp_prompt(kvc, prm):
    b, s, _ = kvc.shape
    nc = s // CMP_STRIDE
    full = lambda a: pl.BlockSpec(a.shape, lambda i, nd=a.ndim: (0,) * nd)
    consts = [prm["wlo"], prm["whi"], prm["gkc"], prm["e64"], prm["e64t"]]
    return pl.pallas_call(
        _cmp_prompt_kernel, grid=(b,),
        in_specs=[pl.BlockSpec((1, s, KV_W), lambda i: (i, 0, 0))] + [full(a) for a in consts],
        out_specs=[pl.BlockSpec((1, NSA_KV_HEADS, nc, LANE), lambda i: (i, 0, 0, 0)),
                   pl.BlockSpec((1, NSA_KV_HEADS, LANE, nc), lambda i: (i, 0, 0, 0))],
        out_shape=[jax.ShapeDtypeStruct((b, NSA_KV_HEADS, nc, LANE), BF16),
                   jax.ShapeDtypeStruct((b, NSA_KV_HEADS, LANE, nc), BF16)],
        compiler_params=_cparams(("parallel",)), name="cmp_blocks_prompt",
    )(kvc, *consts)


def _softmax_update(s, mask, v, m_ref, l_ref, acc_ref, v_feature_major=False):
    m_old = m_ref[...]
    m_new = jnp.maximum(m_old, jnp.max(s, axis=-1, keepdims=True))
    p = jnp.exp(s - m_new)
    if mask is not None:
        p = jnp.where(mask, p, 0.0)
    alpha = jnp.exp(m_old - m_new)
    l_ref[...] = alpha * l_ref[...] + jnp.sum(p, axis=-1, keepdims=True)
    pv = _dot_nt(p.astype(BF16), v) if v_feature_major else _dot(p.astype(BF16), v)
    acc_ref[...] = alpha * acc_ref[...] + pv
    m_ref[...] = m_new


def _softmax_init(m_ref, l_ref, acc_ref):
    m_ref[...] = jnp.full(m_ref.shape, NEG, F32)
    l_ref[...] = jnp.zeros(l_ref.shape, F32)
    acc_ref[...] = jnp.zeros(acc_ref.shape, F32)


def _col_softmax_update(s, mask, kvt, m_ref, l_ref, acc_ref):
    m_old = m_ref[...]
    m_new = jnp.maximum(m_old, jnp.max(s, axis=0, keepdims=True))
    p = jnp.exp(s - m_new)
    if mask is not None:
        p = jnp.where(mask, p, 0.0)
    alpha = jnp.exp(m_old - m_new)
    l_ref[...] = alpha * l_ref[...] + jnp.sum(p, axis=0, keepdims=True)
    acc_ref[...] = alpha * acc_ref[...] + _dot(kvt, p.astype(BF16))
    m_ref[...] = m_new


def _nsa_prompt_kernel(qt_ref, misc_ref, kvc_ref, kvct_ref, kvs_ref, kvst_ref, kvw_ref, kvwt_ref,
                       t01_ref, bwin_ref, c31_ref, at_ref, o_ref,
                       qt_s, sel_s, ocmp_s, osel_s, m_ref, l_ref, acc_ref, sc_ref):
    i = pl.program_id(1)
    qb = Q_BLOCK
    nc = kvc_ref.shape[2]
    nb = at_ref.shape[0]
    bw = bwin_ref.shape[0]
    gate_t = misc_ref[0].T
    tk2 = lax.broadcasted_iota(jnp.int32, (qb, qb), 0)
    tq2 = lax.broadcasted_iota(jnp.int32, (qb, qb), 1)
    causal = tk2 <= tq2
    anti = tk2 > tq2
    tile8 = lambda m: jnp.concatenate([m] * NSA_GROUP, axis=1)
    lanes = lambda g: slice(g * NSA_GROUP * qb, (g + 1) * NSA_GROUP * qb)
    for g in range(NSA_KV_HEADS):
        h0 = g * NSA_GROUP
        ls = lanes(g)
        qt = jnp.concatenate([qt_ref[(h0 + h) * HEAD_DIM:(h0 + h + 1) * HEAD_DIM, :] for h in range(NSA_GROUP)], axis=1)
        qt = jnp.concatenate([qt, jnp.zeros_like(qt)], axis=0)
        qt_s[g] = qt
        c31 = c31_ref[:, ls]

        kvc = kvc_ref[0, g]
        s = _dot(kvc, qt)
        shift_c = lax.broadcasted_iota(jnp.int32, (nc, bw), 0)
        shift_j = lax.broadcasted_iota(jnp.int32, (nc, bw), 1)
        first = (qb // CMP_STRIDE) * i - (bw - qb // CMP_STRIDE)
        shift = (shift_c - shift_j == first).astype(BF16)
        hi, mid, lo = _split3(bwin_ref[:, ls])
        placed = _dot(shift, hi) + _dot(shift, mid) + _dot(shift, lo)
        bias = jnp.where(lax.broadcasted_iota(jnp.int32, (nc, 1), 0) < first, c31, placed)
        c_id = lax.broadcasted_iota(jnp.int32, (nc, qb), 0)
        t_id = lax.broadcasted_iota(jnp.int32, (nc, qb), 1)
        mask_c = tile8((qb * i + t_id - CMP_STRIDE * c_id - (CMP_BLOCK - 1)) >= 0)
        s = jnp.where(mask_c, s + bias, NEG)
        e = jnp.exp(s - jnp.max(s, axis=0, keepdims=True))
        p = jnp.where(mask_c, e * (1.0 / jnp.sum(e, axis=0, keepdims=True)), 0.0)
        ocmp_s[g] = _dot(kvct_ref[0, g], p.astype(BF16))[HEAD_DIM:]

        pg = p[:, :qb]
        for h in range(1, NSA_GROUP):
            pg = pg + p[:, h * qb:(h + 1) * qb]
        hi, mid, lo = _split3(pg)
        at = at_ref[...]
        sc = _dot(at, hi) + _dot(at, mid) + _dot(at, lo)
        j_id = lax.broadcasted_iota(jnp.int32, (nb, qb), 0)
        qpos = qb * i + lax.broadcasted_iota(jnp.int32, (nb, qb), 1)
        cur = qpos // SEL_BLOCK
        forced = (j_id == 0) | (j_id == cur) | (j_id == cur - 1)
        valid = j_id * SEL_BLOCK <= qpos
        sc = jnp.where(valid, jnp.where(forced, sc + FORCE_BONUS, sc), NEG)
        sc_ref[...] = sc

        def rank_body(r, rank):
            sr = sc_ref[pl.ds(r, 1), :]
            ahead = (sr > sc) | ((sr == sc) & (r < j_id))
            return rank + ahead.astype(jnp.int32)

        rank = lax.fori_loop(0, nb, rank_body, jnp.zeros((nb, qb), jnp.int32))
        sel_s[g] = (rank < min(N_SEL, nb)).astype(F32).astype(BF16)

    def sel_mask(g, kt):
        ek = lax.broadcasted_iota(jnp.int32, (qb, nb), 0)
        ej = lax.broadcasted_iota(jnp.int32, (qb, nb), 1)
        expand = (ej == (qb // SEL_BLOCK) * kt + ek // SEL_BLOCK).astype(BF16)
        return _dot(expand, sel_s[g]) > 0.5

    def tile(g, kv_ref, kvt_ref, kt, near, mask):
        kv = kv_ref[g, 0, pl.ds(pl.multiple_of(kt * qb, qb), qb), :]
        bias = c31_ref[:, lanes(g)] if near is None else t01_ref[near, :, lanes(g)]
        st = _dot(kv, qt_s[g]) + bias
        state = (m_ref.at[g], l_ref.at[g], acc_ref.at[g])
        if mask is None:
            _col_softmax_update(st, None, kvt_ref[g, 0, kt], *state)
        else:
            m8 = tile8(mask)
            _col_softmax_update(jnp.where(m8, st, NEG), m8, kvt_ref[g, 0, kt], *state)

    groups = range(NSA_KV_HEADS)

    _softmax_init(m_ref, l_ref, acc_ref)

    def far_body(kt, carry):
        for g in groups:
            tile(g, kvs_ref, kvst_ref, kt, None, sel_mask(g, kt))
        return carry

    lax.fori_loop(0, jnp.maximum(i - 1, 0), far_body, 0)

    @pl.when(i >= 1)
    def _():
        for g in groups:
            tile(g, kvs_ref, kvst_ref, i - 1, 1, sel_mask(g, i - 1))

    for g in groups:
        tile(g, kvs_ref, kvst_ref, i, 0, sel_mask(g, i) & causal)
    for g in groups:
        osel_s[g] = (acc_ref[g] * (1.0 / l_ref[g]))[HEAD_DIM:]

    _softmax_init(m_ref, l_ref, acc_ref)
    nwin = WINDOW // qb
    for diff in range(nwin, -1, -1):
        near = diff if diff <= 1 else None
        mask = causal if diff == 0 else (anti if diff == nwin else None)
        if diff == 0:
            for g in groups:
                tile(g, kvw_ref, kvwt_ref, i, near, mask)
        else:
            @pl.when(i >= diff)
            def _(diff=diff, near=near, mask=mask):
                for g in groups:
                    tile(g, kvw_ref, kvwt_ref, i - diff, near, mask)

    pieces = []
    for g in groups:
        o_win = (acc_ref[g] * (1.0 / l_ref[g]))[HEAD_DIM:]
        for h in range(NSA_GROUP):
            cs = slice(h * qb, (h + 1) * qb)
            r0 = MISC_GATE_OFF + g * NSA_GROUP + h
            gc = gate_t[r0:r0 + 1, :]
            gs = gate_t[r0 + NSA_HEADS:r0 + NSA_HEADS + 1, :]
            gw = gate_t[r0 + 2 * NSA_HEADS:r0 + 2 * NSA_HEADS + 1, :]
            oh = gc * ocmp_s[g, :, cs] + gs * osel_s[g, :, cs] + gw * o_win[:, cs]
            pieces.append(oh.T)
    o_ref[0] = jnp.concatenate(pieces, axis=1).astype(BF16)


def _nsa_prompt(qnt, misc, kvcg, kvcgt, kvsg, kvsgt, kvwg, kvwgt, t01t, bwint, c31row, at, b, s):
    nqb = s // Q_BLOCK
    nq = NSA_HEADS * HEAD_DIM
    nc = s // CMP_STRIDE
    nb = s // SEL_BLOCK
    cols = NSA_GROUP * Q_BLOCK
    full = lambda a: pl.BlockSpec(a.shape, lambda bi, i, nd=a.ndim: (0,) * nd)
    rowm = pl.BlockSpec((NSA_KV_HEADS, 1, s, LANE), lambda bi, i: (0, bi, 0, 0))
    featm = pl.BlockSpec((NSA_KV_HEADS, 1, nqb, LANE, Q_BLOCK), lambda bi, i: (0, bi, 0, 0, 0))
    return pl.pallas_call(
        _nsa_prompt_kernel, grid=(b, nqb),
        in_specs=[pl.BlockSpec((nq, Q_BLOCK), lambda bi, i: (0, bi * nqb + i)),
                  pl.BlockSpec((1, Q_BLOCK, LANE), lambda bi, i: (bi, i, 0)),
                  pl.BlockSpec((1, NSA_KV_HEADS, nc, LANE), lambda bi, i: (bi, 0, 0, 0)),
                  pl.BlockSpec((1, NSA_KV_HEADS, LANE, nc), lambda bi, i: (bi, 0, 0, 0)),
                  rowm, featm, rowm, featm, full(t01t), full(bwint), full(c31row), full(at)],
        out_specs=pl.BlockSpec((1, Q_BLOCK, nq), lambda bi, i: (bi, i, 0)),
        out_shape=jax.ShapeDtypeStruct((b, s, nq), BF16),
        scratch_shapes=[pltpu.VMEM((NSA_KV_HEADS, LANE, cols), BF16), pltpu.VMEM((NSA_KV_HEADS, nb, Q_BLOCK), BF16),
                        pltpu.VMEM((NSA_KV_HEADS, HEAD_DIM, cols), F32), pltpu.VMEM((NSA_KV_HEADS, HEAD_DIM, cols), F32),
                        pltpu.VMEM((NSA_KV_HEADS, 1, cols), F32), pltpu.VMEM((NSA_KV_HEADS, 1, cols), F32),
                        pltpu.VMEM((NSA_KV_HEADS, LANE, cols), F32), pltpu.VMEM((nb, Q_BLOCK), F32)],
        compiler_params=_cparams(("parallel", "arbitrary")), name="nsa_prompt",
    )(qnt, misc.reshape(b, s, LANE), kvcg, kvcgt,
      kvsg.reshape(NSA_KV_HEADS, b, s, LANE), kvsgt.reshape(NSA_KV_HEADS, b, nqb, LANE, Q_BLOCK),
      kvwg.reshape(NSA_KV_HEADS, b, s, LANE), kvwgt.reshape(NSA_KV_HEADS, b, nqb, LANE, Q_BLOCK),
      t01t, bwint, c31row, at)


def _mla_prompt_kernel(qa_ref, qr_ref, ckv_ref, krp_ref, wuv_ref, o_ref, qa_s, qr_s, m_ref, l_ref, acc_ref, *, tk):
    i = pl.program_id(1)
    qb = Q_BLOCK
    nsplit = qa_s.shape[0]
    hps = MLA_HEADS // nsplit
    rows = hps * qb
    for sp in range(nsplit):
        hs = range(sp * hps, (sp + 1) * hps)
        qa_s[sp] = jnp.concatenate([qa_ref[0, :, h * KV_LORA:(h + 1) * KV_LORA] for h in hs], axis=0)
        qr_s[sp] = jnp.concatenate([qr_ref[0, :, h * LANE:(h + 1) * LANE] for h in hs], axis=0)
    _softmax_init(m_ref, l_ref, acc_ref)

    def tile(kt, masked):
        ks = pl.ds(pl.multiple_of(kt * tk, tk), tk)
        ck = ckv_ref[0, ks, :]
        kr = krp_ref[0, ks, :]
        if masked:
            tq = qb * i + lax.broadcasted_iota(jnp.int32, (qb, tk), 0)
            kp = kt * tk + lax.broadcasted_iota(jnp.int32, (qb, tk), 1)
            m3 = jnp.broadcast_to((kp <= tq)[None], (hps, qb, tk))
        for sp in range(nsplit):
            s = _dot_nt(qa_s[sp], ck) + _dot_nt(qr_s[sp], kr)
            state = (m_ref.at[sp], l_ref.at[sp], acc_ref.at[sp])
            if masked:
                s = jnp.where(m3, s.reshape(hps, qb, tk), NEG).reshape(rows, tk)
                _softmax_update(s, m3.reshape(rows, tk), ck, *state)
            else:
                _softmax_update(s, None, ck, *state)

    nfull = (i * qb) // tk

    def body(kt, carry):
        tile(kt, False)
        return carry

    lax.fori_loop(0, nfull, body, 0)
    tile(nfull, True)
    pieces = []
    for sp in range(nsplit):
        o_lat = (acc_ref[sp] / l_ref[sp]).astype(BF16)
        pieces += [_dot(o_lat[h * qb:(h + 1) * qb], wuv_ref[sp * hps + h]) for h in range(hps)]
    o_ref[0] = jnp.concatenate(pieces, axis=1).astype(BF16)


def _mla_prompt(qa, qr, ckvb, krp, wuv, b, s):
    nqb = s // Q_BLOCK
    tk = min(512, s)
    nsplit = 4
    rows = MLA_HEADS * Q_BLOCK // nsplit
    wa, wr = MLA_HEADS * KV_LORA, MLA_HEADS * LANE
    return pl.pallas_call(
        functools.partial(_mla_prompt_kernel, tk=tk), grid=(b, nqb),
        in_specs=[pl.BlockSpec((1, Q_BLOCK, wa), lambda bi, i: (bi, i, 0)),
                  pl.BlockSpec((1, Q_BLOCK, wr), lambda bi, i: (bi, i, 0)),
                  pl.BlockSpec((1, s, KV_LORA), lambda bi, i: (bi, 0, 0)),
                  pl.BlockSpec((1, s, LANE), lambda bi, i: (bi, 0, 0)),
                  pl.BlockSpec(wuv.shape, lambda bi, i: (0, 0, 0))],
        out_specs=pl.BlockSpec((1, Q_BLOCK, MLA_HEADS * V_DIM), lambda bi, i: (bi, i, 0)),
        out_shape=jax.ShapeDtypeStruct((b, s, MLA_HEADS * V_DIM), BF16),
        scratch_shapes=[pltpu.VMEM((nsplit, rows, KV_LORA), BF16), pltpu.VMEM((nsplit, rows, LANE), BF16),
                        pltpu.VMEM((nsplit, rows, 1), F32), pltpu.VMEM((nsplit, rows, 1), F32),
                        pltpu.VMEM((nsplit, rows, KV_LORA), F32)],
        compiler_params=_cparams(("parallel", "arbitrary")), name="mla_prompt",
    )(qa.reshape(b, s, wa), qr.reshape(b, s, wr), ckvb.reshape(b, s, KV_LORA), krp.reshape(b, s, LANE), wuv)


def _outproj_kernel(x_ref, on_ref, om_ref, ga_ref, gb_ref, wpn_ref, wpm_ref, wo_ref, gf_ref, x1_ref, h2_ref):
    a = _dot(on_ref[...], wpn_ref[...])
    b = _dot(om_ref[...], wpm_ref[...])
    mix = jax.nn.sigmoid(ga_ref[...]) * a + jax.nn.sigmoid(gb_ref[...]) * b
    x1 = x_ref[...] + _dot(mix.astype(BF16), wo_ref[...])
    x1_ref[...] = x1
    ms = jnp.mean(x1 * x1, axis=-1, keepdims=True)
    h2_ref[...] = (x1 * lax.rsqrt(ms + EPS) * gf_ref[...]).astype(BF16)


def _outproj(x2d, o_nsa, o_mla, z, zoff, prm):
    n, d = x2d.shape
    tm = min(256, n)
    row = lambda w, c: pl.BlockSpec((tm, w), lambda i, c=c: (i, c))
    once = lambda a: pl.BlockSpec(a.shape, lambda i, nd=a.ndim: (0,) * nd, pipeline_mode=pl.Buffered(1))
    return pl.pallas_call(
        _outproj_kernel, grid=(n // tm,),
        in_specs=[row(d, 0), row(o_nsa.shape[1], 0), row(o_mla.shape[1], 0), row(d, zoff["ga"] // d),
                  row(d, zoff["gb"] // d), once(prm["wpn"]), once(prm["wpm"]), once(prm["wo"]), once(prm["gf"])],
        out_specs=[row(d, 0), row(d, 0)],
        out_shape=[jax.ShapeDtypeStruct((n, d), F32), jax.ShapeDtypeStruct((n, d), BF16)],
        compiler_params=_cparams(("parallel",)), name="out_proj",
    )(x2d, o_nsa, o_mla, z, z, prm["wpn"], prm["wpm"], prm["wo"], prm["gf"])


def _ffn_kernel(h_ref, x1_ref, wg_ref, wu_ref, wd_ref, y_ref, acc_ref):
    f = pl.program_id(1)

    @pl.when(f == 0)
    def _():
        acc_ref[...] = jnp.zeros(acc_ref.shape, F32)

    h = h_ref[...]
    t = jax.nn.silu(_dot(h, wg_ref[...])) * _dot(h, wu_ref[...])
    acc_ref[...] += _dot(t.astype(BF16), wd_ref[...])

    @pl.when(f == pl.num_programs(1) - 1)
    def _():
        y_ref[...] = x1_ref[...] + acc_ref[...]


def _ffn(h2, x1, prm):
    n, d = x1.shape
    dff = prm["wg"].shape[1]
    tm = min(512, n)
    tf = 512 if dff % 512 == 0 else 256
    return pl.pallas_call(
        _ffn_kernel, grid=(n // tm, dff // tf),
        in_specs=[pl.BlockSpec((tm, d), lambda i, f: (i, 0)), pl.BlockSpec((tm, d), lambda i, f: (i, 0)),
                  pl.BlockSpec((d, tf), lambda i, f: (0, f)), pl.BlockSpec((d, tf), lambda i, f: (0, f)),
                  pl.BlockSpec((tf, d), lambda i, f: (f, 0))],
        out_specs=pl.BlockSpec((tm, d), lambda i, f: (i, 0)),
        out_shape=jax.ShapeDtypeStruct((n, d), F32),
        scratch_shapes=[pltpu.VMEM((tm, d), F32)],
        compiler_params=_cparams(("parallel", "arbitrary")), name="ffn",
    )(h2, x1, prm["wg"], prm["wu"], prm["wd"])


def _page_specs(pp, rows, cols):
    return [pl.BlockSpec((1, rows, cols), functools.partial(lambda b, s, pt, j: (pt[b, s * pp + j], 0, 0), j=j))
            for j in range(pp)]


def _n_chunks(pp):
    return 1


def _feature_major_pages(cache):
    n_pool = cache.shape[0]
    return jnp.swapaxes(cache.reshape(n_pool, PAGE_SIZE, -1), 1, 2)


def _group_queries(q, lanes):
    t = q.shape[0]
    blocks = []
    for g in range(NSA_KV_HEADS):
        qg = jnp.concatenate([q[:, (g * NSA_GROUP + h) * HEAD_DIM:(g * NSA_GROUP + h + 1) * HEAD_DIM]
                              for h in range(NSA_GROUP)], axis=0)
        parts = []
        if g > 0:
            parts.append(jnp.zeros((NSA_GROUP * t, g * HEAD_DIM), q.dtype))
        parts.append(qg)
        parts.append(jnp.zeros((NSA_GROUP * t, lanes - (g + 1) * HEAD_DIM), q.dtype))
        blocks.append(jnp.concatenate(parts, axis=1))
    return jnp.concatenate(blocks, axis=0)


def _smp_cmp_kernel(pt_ref, *refs, pp, past_chunks, nsb):
    pages = refs[:pp]
    (new_ref, q_ref, wlo_ref, whi_ref, wt_ref, gkc_ref, e64_ref, e64t_ref, bias_ref, a_ref,
     ocmp_ref, score_ref, lo_ref, hi_ref) = refs[pp:]
    s = pl.program_id(1)
    nr = lo_ref.shape[0]
    t = q_ref.shape[0]
    wlo, whi = wlo_ref[...], whi_ref[...]
    cpp = PAGE_SIZE // CMP_STRIDE
    half = KV_W // 2

    @pl.when(s == 0)
    def _():
        lo_ref[past_chunks:, :] = jnp.zeros((nr - past_chunks, KV_W), F32)
        hi_ref[past_chunks:, :] = jnp.zeros((nr - past_chunks, KV_W), F32)

    x = jnp.concatenate([pages[j][0] for j in range(pp)], axis=0)
    x_hi = x.astype(BF16)
    x_lo = (x - x_hi.astype(F32)).astype(BF16)
    wt = wt_ref[...]
    w_hi = wt.astype(BF16)
    w_lo = (wt - w_hi.astype(F32)).astype(BF16)
    r = _dot_nt(w_hi, x_hi) + _dot_nt(w_lo, x_hi) + _dot_nt(w_hi, x_lo)
    for j in range(pp):
        c0 = j * KV_W
        r0 = pl.multiple_of((s * pp + j) * cpp, cpp)
        lo_ref[pl.ds(r0, cpp), :] = jnp.concatenate([r[0:cpp, c0:c0 + half],
                                                     r[2 * cpp:3 * cpp, c0 + half:c0 + KV_W]], axis=1)
        hi_ref[pl.ds(r0, cpp), :] = jnp.concatenate([r[cpp:2 * cpp, c0:c0 + half],
                                                     r[3 * cpp:4 * cpp, c0 + half:c0 + KV_W]], axis=1)

    @pl.when(s == pl.num_programs(1) - 1)
    def _():
        new = jnp.concatenate([new_ref[0], jnp.zeros((CMP_STRIDE - t, KV_W), F32)], axis=0)
        lo_n, hi_n = _chunk_partials(new, wlo, whi)
        pad = jnp.zeros((cpp - 1, KV_W), F32)
        lo_ref[past_chunks:past_chunks + cpp, :] = jnp.concatenate([lo_n, pad], axis=0)
        hi_ref[past_chunks:past_chunks + cpp, :] = jnp.concatenate([hi_n, pad], axis=0)
        groups = _blocks_to_groups(lo_ref[...], hi_ref[...], gkc_ref[...], e64_ref[...], e64t_ref[...])
        q = q_ref[...]
        nbp = a_ref.shape[1]
        past = past_chunks * CMP_STRIDE
        c_id = lax.broadcasted_iota(jnp.int32, (t, nr), 1)
        t_id = lax.broadcasted_iota(jnp.int32, (t, nr), 0)
        mask_c = (past + t_id - CMP_STRIDE * c_id - (CMP_BLOCK - 1)) >= 0
        j_id = lax.broadcasted_iota(jnp.int32, (t, nbp), 1)
        qpos = past + lax.broadcasted_iota(jnp.int32, (t, nbp), 0)
        cur = qpos // SEL_BLOCK
        forced = (j_id == 0) | (j_id == cur) | (j_id == cur - 1)
        valid = j_id * SEL_BLOCK <= qpos
        for g in range(NSA_KV_HEADS):
            h0 = g * NSA_GROUP
            qg = jnp.concatenate([q[:, (h0 + h) * HEAD_DIM:(h0 + h + 1) * HEAD_DIM] for h in range(NSA_GROUP)], axis=0)
            qp = jnp.concatenate([qg, jnp.zeros_like(qg)], axis=1)
            kv = groups[g].astype(BF16)
            sc = _dot_nt(qp, kv).reshape(NSA_GROUP, t, nr) + bias_ref[h0:h0 + NSA_GROUP]
            sc = jnp.where(mask_c[None], sc, NEG)
            e = jnp.exp(sc - jnp.max(sc, axis=-1, keepdims=True))
            p = jnp.where(mask_c[None], e / jnp.sum(e, axis=-1, keepdims=True), 0.0)
            ocmp_ref[0, g * NSA_GROUP * t:(g + 1) * NSA_GROUP * t, :] = _dot(
                p.reshape(NSA_GROUP * t, nr).astype(BF16), kv)
            score = _dot_sel(jnp.sum(p, axis=0), a_ref[...])
            score = jnp.where(valid, jnp.where(forced, score + FORCE_BONUS, score), NEG)
            score_ref[0, g * t:(g + 1) * t, :] = jnp.where(j_id < nsb, score, PAD_SCORE)


def _smp_cmp(page_table, cache_cmp, kvc_new, qn, bias_c, a_mat, prm, pp):
    bd, n_pages = page_table.shape
    t = kvc_new.shape[1]
    past_chunks = n_pages * PAGE_SIZE // CMP_STRIDE
    nr = a_mat.shape[0]
    nbp = a_mat.shape[1]
    nsb = (past_chunks + SEL_BLOCK // CMP_STRIDE) // (SEL_BLOCK // CMP_STRIDE)
    rows = NSA_HEADS * t
    full = lambda a: pl.BlockSpec(a.shape, lambda b, s, pt, nd=a.ndim: (0,) * nd)
    consts = [prm["wlo"], prm["whi"], prm["wt"], prm["gkc"], prm["e64"], prm["e64t"], bias_c, a_mat]
    gs = pltpu.PrefetchScalarGridSpec(
        num_scalar_prefetch=1, grid=(bd, n_pages // pp),
        in_specs=_page_specs(pp, KV_W, PAGE_SIZE) + [pl.BlockSpec((1, t, KV_W), lambda b, s, pt: (b, 0, 0)),
                                          pl.BlockSpec((t, qn.shape[1]), lambda b, s, pt: (b, 0))]
        + [full(a) for a in consts],
        out_specs=[pl.BlockSpec((1, rows, LANE), lambda b, s, pt: (b, 0, 0)),
                   pl.BlockSpec((1, NSA_KV_HEADS * t, nbp), lambda b, s, pt: (b, 0, 0))],
        scratch_shapes=[pltpu.VMEM((nr, KV_W), F32), pltpu.VMEM((nr, KV_W), F32)])
    return pl.pallas_call(
        functools.partial(_smp_cmp_kernel, pp=pp, past_chunks=past_chunks, nsb=nsb), grid_spec=gs,
        out_shape=[jax.ShapeDtypeStruct((bd, rows, LANE), F32),
                   jax.ShapeDtypeStruct((bd, NSA_KV_HEADS * t, nbp), F32)],
        compiler_params=_cparams(("parallel", "arbitrary")), name="sample_cmp",
    )(page_table, *([cache_cmp] * pp), kvc_new, qn, *consts)


def _topk_mask_kernel(score_ref, exp_ref, o_ref, sel_ref, sc_ref, *, nsb):
    @pl.when(pl.program_id(0) == 0)
    def _():
        sc = score_ref[...].T
        sc_ref[...] = sc
        j_id = lax.broadcasted_iota(jnp.int32, sc.shape, 0)

        def rank_body(r, rank):
            sr = sc_ref[pl.ds(r, 1), :]
            ahead = (sr > sc) | ((sr == sc) & (r < j_id))
            return rank + ahead.astype(jnp.int32)

        rank = lax.fori_loop(0, nsb, rank_body, jnp.zeros(sc.shape, jnp.int32))
        sel_ref[...] = ((rank < min(N_SEL, nsb)) & (j_id < nsb)).astype(F32).T.astype(BF16)

    rows = sel_ref.shape[0]
    rc = min(512, rows)
    for r in range(0, rows, rc):
        o_ref[r:r + rc, :] = _dot(sel_ref[r:r + rc, :], exp_ref[...]).astype(BF16)


def _topk_mask(score2d, expand, nsb, tile):
    rows, nbp = score2d.shape
    keys = expand.shape[1]
    return pl.pallas_call(
        functools.partial(_topk_mask_kernel, nsb=nsb), grid=(keys // tile,),
        in_specs=[pl.BlockSpec((rows, nbp), lambda k: (0, 0)), pl.BlockSpec((nbp, tile), lambda k: (0, k))],
        out_specs=pl.BlockSpec((rows, tile), lambda k: (0, k)),
        out_shape=jax.ShapeDtypeStruct((rows, keys), BF16),
        scratch_shapes=[pltpu.VMEM((rows, nbp), BF16), pltpu.VMEM((nbp, rows), F32)],
        compiler_params=_cparams(("arbitrary",)), name="sample_topk_mask",
    )(score2d, expand)


def _rows_pad(a, rows):
    return jnp.concatenate([a, jnp.zeros((rows - a.shape[0], a.shape[1]), a.dtype)], axis=0)


def _smp_sel_kernel(pt_ref, *refs, pp):
    pages = refs[:pp]
    (mask_ref, maskt_ref, q_ref, new_ref, c31_ref, blast_ref, bnew_ref, o_ref, qp_ref, m_ref, l_ref, acc_ref) = refs[pp:]
    s = pl.program_id(1)
    t = q_ref.shape[0]
    rows = NSA_HEADS * t
    last = pl.num_programs(1) - 1

    @pl.when(s == 0)
    def _():
        qp_ref[...] = _group_queries(q_ref[...], KV_W)
        _softmax_init(m_ref, l_ref, acc_ref)

    qp = qp_ref[...]

    def expand_mask(mk, nk):
        m4 = jnp.broadcast_to(mk.astype(F32).reshape(NSA_KV_HEADS, 1, t, nk), (NSA_KV_HEADS, NSA_GROUP, t, nk))
        return m4.reshape(rows, nk) > 0.5

    nl = blast_ref.shape[1]
    nch = _n_chunks(pp)
    cp = pp // nch
    nk = cp * PAGE_SIZE

    def chunk(c, bias):
        keys = jnp.concatenate([pages[c * cp + j][0] for j in range(cp)], axis=1).astype(BF16)
        mask = expand_mask(mask_ref[0, :, c * nk:(c + 1) * nk], nk)
        st = jnp.where(mask, _dot(qp, keys) + bias, NEG)
        _softmax_update(st, mask, keys, m_ref, l_ref, acc_ref, v_feature_major=True)

    for c in range(nch - 1):
        chunk(c, c31_ref[...])

    @pl.when(s != last)
    def _():
        chunk(nch - 1, c31_ref[...])

    @pl.when(s == last)
    def _():
        far = [jnp.broadcast_to(c31_ref[...], (rows, nk - nl))] if nk > nl else []
        chunk(nch - 1, jnp.concatenate(far + [blast_ref[...]], axis=1))
        kn = _rows_pad(new_ref[0], LANE).astype(BF16)
        j_id = lax.broadcasted_iota(jnp.int32, (rows, LANE), 1)
        t_id = lax.broadcasted_iota(jnp.int32, (rows, LANE), 0) % t
        mt = expand_mask(maskt_ref[0][:, :LANE], LANE) & (j_id <= t_id) & (j_id < t)
        st = jnp.where(mt, _dot_nt(qp, kn) + bnew_ref[...], NEG)
        _softmax_update(st, mt, kn, m_ref, l_ref, acc_ref)
        o = acc_ref[...] / l_ref[...]
        half = KV_W // 2
        hr = NSA_GROUP * t
        o_ref[0] = jnp.concatenate([o[g * hr:(g + 1) * hr, half + g * HEAD_DIM:half + (g + 1) * HEAD_DIM]
                                    for g in range(NSA_KV_HEADS)], axis=0)


def _smp_sel(page_table, cache_slc, mask3, qn, kvs_new, c31r, blast, bnew, pp):
    bd, n_pages = page_table.shape
    t = kvs_new.shape[1]
    rows = NSA_HEADS * t
    nk = pp * PAGE_SIZE
    nsteps = n_pages // pp
    full = lambda a: pl.BlockSpec(a.shape, lambda b, s, pt, nd=a.ndim: (0,) * nd)
    gs = pltpu.PrefetchScalarGridSpec(
        num_scalar_prefetch=1, grid=(bd, nsteps),
        in_specs=_page_specs(pp, KV_W, PAGE_SIZE) + [
            pl.BlockSpec((1, NSA_KV_HEADS * t, nk), lambda b, s, pt: (b, 0, s)),
            pl.BlockSpec((1, NSA_KV_HEADS * t, nk), lambda b, s, pt: (b, 0, nsteps)),
            pl.BlockSpec((t, qn.shape[1]), lambda b, s, pt: (b, 0)),
            pl.BlockSpec((1, t, KV_W), lambda b, s, pt: (b, 0, 0)),
            full(c31r), full(blast), full(bnew)],
        out_specs=pl.BlockSpec((1, rows, HEAD_DIM), lambda b, s, pt: (b, 0, 0)),
        scratch_shapes=[pltpu.VMEM((rows, KV_W), BF16), pltpu.VMEM((rows, 1), F32), pltpu.VMEM((rows, 1), F32),
                        pltpu.VMEM((rows, KV_W), F32)])
    return pl.pallas_call(
        functools.partial(_smp_sel_kernel, pp=pp), grid_spec=gs,
        out_shape=jax.ShapeDtypeStruct((bd, rows, HEAD_DIM), F32),
        compiler_params=_cparams(("parallel", "arbitrary")), name="sample_sel",
    )(page_table, *([cache_slc] * pp), mask3, mask3, qn, kvs_new, c31r, blast, bnew)


def _smp_mla_kernel(pt_ref, *refs, pp):
    cpages = refs[:pp]
    rpages = refs[pp:2 * pp]
    (qa_ref, qr_ref, cnew_ref, rnew_ref, wuv_ref, o_ref, qa_s, qr_s, m_ref, l_ref, acc_ref) = refs[2 * pp:]
    s = pl.program_id(1)
    t = qa_ref.shape[0]
    rows = MLA_HEADS * t

    @pl.when(s == 0)
    def _():
        qa_s[...] = jnp.concatenate([qa_ref[:, h * KV_LORA:(h + 1) * KV_LORA] for h in range(MLA_HEADS)], axis=0)
        qr_s[...] = jnp.concatenate([qr_ref[:, h * LANE:(h + 1) * LANE] for h in range(MLA_HEADS)], axis=0)
        _softmax_init(m_ref, l_ref, acc_ref)

    qa, qr = qa_s[...], qr_s[...]
    nch = _n_chunks(pp)
    cp = pp // nch
    for c in range(nch):
        js = range(c * cp, (c + 1) * cp)
        ck = jnp.concatenate([cpages[j][0] for j in js], axis=0).astype(BF16)
        kr = jnp.concatenate([rpages[j][0] for j in js], axis=1).astype(BF16)
        st = _dot_nt(qa, ck) + _dot(qr[:, :QK_ROPE], kr)
        _softmax_update(st, None, ck, m_ref, l_ref, acc_ref)

    @pl.when(s == pl.num_programs(1) - 1)
    def _():
        cn = _rows_pad(cnew_ref[0], LANE)
        rn = _rows_pad(rnew_ref[0], LANE)
        j_id = lax.broadcasted_iota(jnp.int32, (rows, LANE), 1)
        t_id = lax.broadcasted_iota(jnp.int32, (rows, LANE), 0) % t
        mt = (j_id <= t_id) & (j_id < t)
        sn = jnp.where(mt, _dot_nt(qa, cn) + _dot_nt(qr, rn), NEG)
        _softmax_update(sn, mt, cn, m_ref, l_ref, acc_ref)
        o_lat = (acc_ref[...] / l_ref[...]).astype(BF16)
        o_ref[...] = jnp.concatenate([_dot(o_lat[h * t:(h + 1) * t], wuv_ref[h]) for h in range(MLA_HEADS)],
                                     axis=1).astype(BF16)


def _smp_mla(page_table, cache_ckv, cache_kr, qa, qr, ckvb_new, krp_new, wuv, pp):
    bd, n_pages = page_table.shape
    t = ckvb_new.shape[1]
    rows = MLA_HEADS * t
    gs = pltpu.PrefetchScalarGridSpec(
        num_scalar_prefetch=1, grid=(bd, n_pages // pp),
        in_specs=_page_specs(pp, PAGE_SIZE, KV_LORA) + _page_specs(pp, QK_ROPE, PAGE_SIZE) + [
            pl.BlockSpec((t, qa.shape[1]), lambda b, s, pt: (b, 0)),
            pl.BlockSpec((t, qr.shape[1]), lambda b, s, pt: (b, 0)),
            pl.BlockSpec((1, t, KV_LORA), lambda b, s, pt: (b, 0, 0)),
            pl.BlockSpec((1, t, LANE), lambda b, s, pt: (b, 0, 0)),
            pl.BlockSpec(wuv.shape, lambda b, s, pt: (0, 0, 0))],
        out_specs=pl.BlockSpec((t, MLA_HEADS * V_DIM), lambda b, s, pt: (b, 0)),
        scratch_shapes=[pltpu.VMEM((rows, KV_LORA), BF16), pltpu.VMEM((rows, LANE), BF16),
                        pltpu.VMEM((rows, 1), F32), pltpu.VMEM((rows, 1), F32), pltpu.VMEM((rows, KV_LORA), F32)])
    return pl.pallas_call(
        functools.partial(_smp_mla_kernel, pp=pp), grid_spec=gs,
        out_shape=jax.ShapeDtypeStruct((bd * t, MLA_HEADS * V_DIM), BF16),
        compiler_params=_cparams(("parallel", "arbitrary")), name="sample_mla",
    )(page_table, *([cache_ckv] * pp), *([cache_kr] * pp), qa, qr, ckvb_new, krp_new, wuv)


def _smp_win_kernel(state_ref, new_ref, q_ref, misc_ref, ocmp_ref, osel_ref, bias_ref, o_ref, win_ref):
    t = q_ref.shape[0]
    wl = state_ref.shape[1]
    rows = NSA_HEADS * t
    state = state_ref[0]
    new = new_ref[0]
    win_ref[0, :wl - t, :] = state[t:, :]
    win_ref[0, wl - t:, :] = new
    nkp = bias_ref.shape[1]
    ctx = jnp.concatenate([state, new, jnp.zeros((nkp - wl - t, KV_W), F32)], axis=0).astype(BF16)
    qp = _group_queries(q_ref[...], KV_W)
    j_id = lax.broadcasted_iota(jnp.int32, (rows, nkp), 1)
    t_id = lax.broadcasted_iota(jnp.int32, (rows, nkp), 0) % t
    dist = wl + t_id - j_id
    mask = (dist >= 0) & (dist < WINDOW) & (j_id < wl + t)
    s = jnp.where(mask, _dot_nt(qp, ctx) + bias_ref[...], NEG)
    e = jnp.exp(s - jnp.max(s, axis=-1, keepdims=True))
    p = jnp.where(mask, e / jnp.sum(e, axis=-1, keepdims=True), 0.0)
    o_win = _dot(p.astype(BF16), ctx)
    misc = misc_ref[...]
    ocmp = ocmp_ref[0]
    osel = osel_ref[0]
    half = KV_W // 2
    pieces = []
    for hh in range(NSA_HEADS):
        g = hh // NSA_GROUP
        rs = slice(hh * t, (hh + 1) * t)
        gc = misc[:, MISC_GATE_OFF + hh:MISC_GATE_OFF + hh + 1]
        gsel = misc[:, MISC_GATE_OFF + NSA_HEADS + hh:MISC_GATE_OFF + NSA_HEADS + hh + 1]
        gw = misc[:, MISC_GATE_OFF + 2 * NSA_HEADS + hh:MISC_GATE_OFF + 2 * NSA_HEADS + hh + 1]
        pieces.append(gc * ocmp[rs, HEAD_DIM:] + gsel * osel[rs, :]
                      + gw * o_win[rs, half + g * HEAD_DIM:half + (g + 1) * HEAD_DIM])
    o_ref[...] = jnp.concatenate(pieces, axis=1).astype(BF16)


def _smp_win(state, kvw_new, qn, misc, ocmp, osel, bias_w):
    bd, wl, _ = state.shape
    t = kvw_new.shape[1]
    rows = NSA_HEADS * t
    nq = NSA_HEADS * HEAD_DIM
    return pl.pallas_call(
        _smp_win_kernel, grid=(bd,),
        in_specs=[pl.BlockSpec((1, wl, KV_W), lambda b: (b, 0, 0)), pl.BlockSpec((1, t, KV_W), lambda b: (b, 0, 0)),
                  pl.BlockSpec((t, nq), lambda b: (b, 0)), pl.BlockSpec((t, LANE), lambda b: (b, 0)),
                  pl.BlockSpec((1, rows, LANE), lambda b: (b, 0, 0)), pl.BlockSpec((1, rows, HEAD_DIM), lambda b: (b, 0, 0)),
                  pl.BlockSpec(bias_w.shape, lambda b: (0, 0))],
        out_specs=[pl.BlockSpec((t, nq), lambda b: (b, 0)), pl.BlockSpec((1, wl, KV_W), lambda b: (b, 0, 0))],
        out_shape=[jax.ShapeDtypeStruct((bd * t, nq), BF16), jax.ShapeDtypeStruct((bd, wl, KV_W), F32)],
        compiler_params=_cparams(("parallel",)), name="sample_window_merge",
    )(state, kvw_new, qn, misc, ocmp, osel, bias_w)


def _prep_params(d, attn_norm, w_in, g_q_nsa, g_k_cmp, g_k_slc, g_k_win, phi_cmp, g_cq, w_uq, g_q_mla, g_ckv,
                 g_krope, w_uk, w_uv, w_proj_nsa, w_proj_mla, w_out, ffn_norm, w_gate, w_up, w_down):
    nq = NSA_HEADS * HEAD_DIM
    widths = (nq, KV_W, KV_W, KV_W, GATE_N_W, Q_LORA, KV_LORA, QK_ROPE, d, d)
    names = ("q", "kvc", "kvs", "kvw", "gn", "cq", "ckv", "kr", "ga", "gb")
    splits = np.cumsum(widths)[:-1]
    seg = dict(zip(names, jnp.split(w_in, splits, axis=1)))
    zoff = _z_layout(d)
    tn = 1024 if d >= 1024 else 256
    zw = _round_up(zoff["end"], tn)
    zc = lambda w: jnp.zeros((d, w), w_in.dtype)
    cols = [seg["q"], seg["cq"], seg["kr"], seg["gn"], zc(zoff["ga"] - zoff["misc"] - QK_ROPE - GATE_N_W),
            seg["ga"], seg["gb"], zc(zoff["kv"] - zoff["gb"] - d), seg["kvc"], seg["kvs"], seg["kvw"], seg["ckv"],
            zc(zw - zoff["end"])]
    prm = {"w_cat": jnp.concatenate(cols, axis=1).astype(BF16), "tn": tn}
    prm["g_attn"] = attn_norm.reshape(1, d)
    prm["gq"] = jnp.tile(g_q_nsa, NSA_HEADS).reshape(1, nq)
    prm["gks"] = jnp.tile(g_k_slc, NSA_KV_HEADS).reshape(1, LANE)
    prm["gkw"] = jnp.tile(g_k_win, NSA_KV_HEADS).reshape(1, LANE)
    prm["gkc"] = jnp.tile(g_k_cmp, NSA_KV_HEADS).reshape(1, LANE)
    prm["gcq"] = g_cq.reshape(1, Q_LORA)
    wq = w_uq.reshape(Q_LORA, MLA_HEADS, QK_NOPE + QK_ROPE)
    w_rope = jnp.pad(wq[:, :, QK_NOPE:], ((0, 0), (0, 0), (0, LANE - QK_ROPE)))
    prm["wuq"] = jnp.concatenate([wq[:, :, :QK_NOPE].reshape(Q_LORA, MLA_HEADS * QK_NOPE),
                                  w_rope.reshape(Q_LORA, MLA_HEADS * LANE)], axis=1).astype(BF16)
    prm["gmn"] = jnp.tile(g_q_mla[:QK_NOPE], MLA_HEADS).reshape(1, MLA_HEADS * QK_NOPE)
    prm["gmr"] = jnp.tile(jnp.pad(g_q_mla[QK_NOPE:], (0, LANE - QK_ROPE)), MLA_HEADS).reshape(1, MLA_HEADS * LANE)
    prm["gckv"] = g_ckv.reshape(1, KV_LORA)
    prm["gkr"] = jnp.pad(g_krope, (0, LANE - QK_ROPE)).reshape(1, LANE)
    prm["wuk"] = w_uk.astype(BF16)
    prm["wuv"] = w_uv.astype(BF16)
    e64 = _seg_indicator(nq, HEAD_DIM)
    e128 = _seg_indicator(MLA_HEADS * LANE, LANE)
    prm["e64"], prm["e64t"] = jnp.asarray(e64, BF16), jnp.asarray(e64.T, BF16)
    prm["e128"], prm["e128t"] = jnp.asarray(e128, BF16), jnp.asarray(e128.T, BF16)
    ph = jnp.repeat(phi_cmp, KV_W // 2, axis=0).T
    prm["wlo"], prm["whi"] = ph[:CMP_STRIDE], ph[CMP_STRIDE:]
    eye = jnp.eye(PAGE_SIZE // CMP_STRIDE, dtype=F32)
    prm["wt"] = jnp.concatenate([jnp.kron(eye, phi_cmp[e, part * CMP_STRIDE:(part + 1) * CMP_STRIDE][None, :])
                                 for e in range(2) for part in range(2)], axis=0)
    prm["wpn"], prm["wpm"], prm["wo"] = w_proj_nsa.astype(BF16), w_proj_mla.astype(BF16), w_out.astype(BF16)
    prm["gf"] = ffn_norm.reshape(1, d)
    prm["wg"], prm["wu"], prm["wd"] = w_gate.astype(BF16), w_up.astype(BF16), w_down.astype(BF16)
    return prm, zoff


def _rope_tables(pos):
    half = QK_ROPE // 2
    inv_freq = ROPE_THETA ** (-jnp.arange(half, dtype=jnp.float32) / half)
    ang = pos.astype(jnp.float32)[:, None] * inv_freq
    cos, sin = jnp.cos(ang), jnp.sin(ang)
    z = jnp.zeros((pos.shape[0], LANE - QK_ROPE), F32)
    zh = jnp.zeros_like(sin)
    return (jnp.concatenate([cos, cos, z], axis=1), jnp.concatenate([zh, sin, z], axis=1),
            jnp.concatenate([-sin, zh, z], axis=1))


def _layer(xp, xs, cache_cmp, cache_slc, cache_ckv, cache_kr, win_state, page_table, rel_bias_table, weights):
    b, s, d = xp.shape
    bd, t, _ = xs.shape
    n_pages = page_table.shape[1]
    past = n_pages * PAGE_SIZE
    n_pool = cache_cmp.shape[0]
    wl = win_state.shape[1]
    prm, zoff = _prep_params(d, *weights)
    pp = max(1, min(32, n_pages // 2))
    nsteps = n_pages // pp

    qb = Q_BLOCK
    ar = np.arange
    d_t01 = (ar(qb)[None, :, None] - ar(qb)[None, None, :]) + qb * ar(2)[:, None, None]
    bw = 32
    d_bwin = ar(qb)[:, None] - CMP_STRIDE * (ar(bw)[None, :] - (bw - qb // CMP_STRIDE)) - (CMP_BLOCK - 1)
    d_c31 = np.full((LANE,), MAX_DISTANCE)
    past_chunks = past // CMP_STRIDE
    nc_s = past_chunks + SEL_BLOCK // CMP_STRIDE
    nr_s = _round_up(nc_s, LANE)
    nsb = nc_s // (SEL_BLOCK // CMP_STRIDE)
    nbp = _round_up(nsb, LANE)
    d_cmp_s = past + ar(t)[:, None] - CMP_STRIDE * ar(nr_s)[None, :] - (CMP_BLOCK - 1)
    nl = min(2 * LANE, pp * PAGE_SIZE)
    d_last = nl + ar(t)[:, None] - ar(nl)[None, :]
    d_new = ar(t)[:, None] - ar(LANE)[None, :]
    nkp = _round_up(wl + t, LANE)
    d_win = wl + ar(t)[:, None] - ar(nkp)[None, :]
    t01, bwin, c31, bias_c, blast, bnew, bias_w = _bias_tables(
        rel_bias_table, [d_t01, d_bwin, d_c31, d_cmp_s, d_last, d_new, d_win])
    t01t = t01.transpose(1, 3, 0, 2).reshape(2, qb, NSA_HEADS * qb)
    bwint = bwin.transpose(2, 0, 1).reshape(bw, NSA_HEADS * qb)
    c31row = jnp.repeat(c31[:, 0], qb).reshape(1, NSA_HEADS * qb)
    rows_s = NSA_HEADS * t
    c31r = jnp.repeat(c31, t, axis=0)[:, :1] * jnp.ones((1, 1), F32)
    blast = blast.reshape(rows_s, nl)
    bnew = bnew.reshape(rows_s, LANE)
    bias_w = bias_w.reshape(rows_s, nkp)
    at = jnp.asarray(_score_matrix(s // CMP_STRIDE, s // SEL_BLOCK).T, BF16)
    a_s = np.zeros((nr_s, nbp), np.float32)
    a_s[:nc_s] = _score_matrix(nc_s, nbp)
    a_s = jnp.asarray(a_s, BF16)
    keys_pad = (nsteps + 1) * pp * PAGE_SIZE
    expand = np.zeros((nbp, keys_pad), np.float32)
    kk = ar(nsb * SEL_BLOCK)
    expand[kk // SEL_BLOCK, kk] = 1.0
    expand = jnp.asarray(expand, BF16)

    pos_p = jnp.arange(s)
    tabs_p = tuple(jnp.tile(a, (b, 1)) for a in _rope_tables(pos_p))
    zp = _proj(xp.reshape(b * s, d), prm["g_attn"], prm["w_cat"], prm["tn"])
    pp_ = _post(zp, zoff, tabs_p, prm, True)
    kvcg, kvcgt = _cmp_prompt(pp_["kvc"].reshape(b, s, KV_W), prm)
    o_nsa_p = _nsa_prompt(pp_["qnt"], pp_["misc"], kvcg, kvcgt, pp_["kvsg"], pp_["kvsgt"], pp_["kvwg"], pp_["kvwgt"],
                          t01t, bwint, c31row, at, b, s)
    o_mla_p = _mla_prompt(pp_["qa"], pp_["qr"], pp_["ckvb"], pp_["krp"], prm["wuv"], b, s)
    x1p, h2p = _outproj(xp.reshape(b * s, d), o_nsa_p.reshape(b * s, -1), o_mla_p.reshape(b * s, -1), zp, zoff, prm)
    yp = _ffn(h2p, x1p, prm).reshape(b, s, d)

    pos_s = past + jnp.arange(t)
    tabs_s = tuple(jnp.tile(a, (bd, 1)) for a in _rope_tables(pos_s))
    zs = _proj(xs.reshape(bd * t, d), prm["g_attn"], prm["w_cat"], prm["tn"])
    ps = _post(zs, zoff, tabs_s, prm, False)
    cc = _feature_major_pages(cache_cmp)
    cs = _feature_major_pages(cache_slc)
    cache_kr = _feature_major_pages(cache_kr)
    ocmp_s, score = _smp_cmp(page_table, cc, ps["kvc"].reshape(bd, t, KV_W), ps["qn"], bias_c, a_s, prm, pp)
    mask = _topk_mask(score.reshape(bd * NSA_KV_HEADS * t, nbp), expand, nsb, pp * PAGE_SIZE)
    osel_s = _smp_sel(page_table, cs, mask.reshape(bd, NSA_KV_HEADS * t, keys_pad), ps["qn"],
                      ps["kvs"].reshape(bd, t, KV_W), c31r, blast, bnew, pp)
    o_mla_s = _smp_mla(page_table, cache_ckv, cache_kr, ps["qa"], ps["qr"], ps["ckvb"].reshape(bd, t, KV_LORA),
                       ps["krp"].reshape(bd, t, LANE), prm["wuv"], pp)
    o_nsa_s, win_s = _smp_win(win_state.reshape(bd, wl, KV_W), ps["kvw"].reshape(bd, t, KV_W), ps["qn"], ps["misc"],
                              ocmp_s, osel_s, bias_w)
    x1s, h2s = _outproj(xs.reshape(bd * t, d), o_nsa_s, o_mla_s, zs, zoff, prm)
    ys = _ffn(h2s, x1s, prm).reshape(bd, t, d)

    kvh = (2, NSA_KV_HEADS, HEAD_DIM)
    wp = min(WINDOW, s)
    return (yp, ys,
            pp_["kvc"].reshape((b, s) + kvh), ps["kvc"].reshape((bd, t) + kvh),
            pp_["kvs"].reshape((b, s) + kvh), ps["kvs"].reshape((bd, t) + kvh),
            pp_["ckv"].reshape(b, s, KV_LORA), ps["ckv"].reshape(bd, t, KV_LORA),
            pp_["misc"][:, :QK_ROPE].reshape(b, s, QK_ROPE), ps["misc"][:, :QK_ROPE].reshape(bd, t, QK_ROPE),
            pp_["kvw"].reshape((b, s) + kvh)[:, s - wp:], win_s.reshape((bd, wl) + kvh))


def kernel(x_prompt, x_sample, cache_cmp_kv, cache_slc_kv, cache_mla_ckv, cache_mla_krope, state_win_kv, page_table, rel_bias_table, attn_norm, w_in, g_q_nsa, g_k_cmp, g_k_slc, g_k_win, phi_cmp, g_cq, w_uq, g_q_mla, g_ckv, g_krope, w_uk, w_uv, w_proj_nsa, w_proj_mla, w_out, ffn_norm, w_gate, w_up, w_down):
    depth = w_in.shape[0]
    xp, xs = x_prompt, x_sample
    per_layer = []
    for l in range(depth):
        weights = (attn_norm[l], w_in[l], g_q_nsa[l], g_k_cmp[l], g_k_slc[l], g_k_win[l], phi_cmp[l], g_cq[l],
                   w_uq[l], g_q_mla[l], g_ckv[l], g_krope[l], w_uk[l], w_uv[l], w_proj_nsa[l], w_proj_mla[l],
                   w_out[l], ffn_norm[l], w_gate[l], w_up[l], w_down[l])
        xp, xs, *st = _layer(xp, xs, cache_cmp_kv[l], cache_slc_kv[l], cache_mla_ckv[l], cache_mla_krope[l],
                             state_win_kv[l], page_table, rel_bias_table, weights)
        per_layer.append(st)
    stacked = [jnp.stack(v) for v in zip(*per_layer)]
    return (xp, xs, *stacked)
```

```python
import functools
import math

import numpy as np
import jax
import jax.numpy as jnp
from jax import lax
from jax.experimental import pallas as pl
from jax.experimental.pallas import tpu as pltpu

F32 = jnp.float32
BF16 = jnp.bfloat16

NSA_HEADS = 16
NSA_KV_HEADS = 2
NSA_GROUP = NSA_HEADS // NSA_KV_HEADS
HEAD_DIM = 64
CMP_BLOCK = 32
CMP_STRIDE = 16
SEL_BLOCK = 64
N_SEL = 16
WINDOW = 512
MLA_HEADS = 16
QK_NOPE = 64
QK_ROPE = 32
V_DIM = 64
KV_LORA = 256
Q_LORA = 512
ROPE_THETA = 10000.0
N_BUCKETS = 32
MAX_DISTANCE = 128
PAGE_SIZE = 128
Q_BLOCK = 128
EPS = 1e-6
NEG = -1e30
NEG_CLAMP = -1e29
PAD_SCORE = -3e38
FORCE_BONUS = 1e6

LANE = 128
KV_W = 2 * NSA_KV_HEADS * HEAD_DIM
GATE_N_W = 3 * NSA_HEADS
MISC_GATE_OFF = QK_ROPE
VMEM_LIMIT = 56 * 1024 * 1024


def _round_up(a, b):
    return -(-a // b) * b


def _cparams(sem):
    return pltpu.CompilerParams(dimension_semantics=sem, vmem_limit_bytes=VMEM_LIMIT)


def _dot(a, b):
    return jnp.dot(a, b, preferred_element_type=F32)


def _dot_nt(a, b):
    return lax.dot_general(a, b, (((1,), (1,)), ((), ())), preferred_element_type=F32)


def _split3(x):
    hi = x.astype(BF16)
    r = x - hi.astype(F32)
    mid = r.astype(BF16)
    lo = (r - mid.astype(F32)).astype(BF16)
    return hi, mid, lo


def _dot_sel(x, m):
    hi, mid, lo = _split3(x)
    return _dot(hi, m) + _dot(mid, m) + _dot(lo, m)


def _dot_sel_nt(m, x):
    hi, mid, lo = _split3(x)
    return _dot_nt(m, hi) + _dot_nt(m, mid) + _dot_nt(m, lo)


def _seg_rinv(x, e, et, width):
    ss = _dot_sel(x * x, e)
    r = lax.rsqrt(ss / width + EPS)
    return _dot_sel(r, et)


def _t5_bucket_np(dist):
    n = np.maximum(dist, 0)
    exact = N_BUCKETS // 2
    log_ratio = np.log(np.maximum(n, 1).astype(np.float32) / exact) / math.log(MAX_DISTANCE / exact)
    large = np.minimum(exact + (log_ratio * (N_BUCKETS - exact)).astype(np.int32), N_BUCKETS - 1)
    return np.where(n < exact, n, large)


def _seg_indicator(width, seg):
    e = np.zeros((width, LANE), np.float32)
    e[np.arange(width), np.arange(width) // seg] = 1.0
    return e


def _score_matrix(nc, nb_pad):
    ratio = SEL_BLOCK // CMP_STRIDE
    a = np.zeros((nc, nb_pad), np.float32)
    c = np.arange(nc)
    a[c, c // ratio] = 1.0
    cc = c[(c % ratio == ratio - 1) & (c // ratio + 1 < nc // ratio)]
    a[cc, cc // ratio + 1] = 1.0
    return a


def _bias_expand_kernel(tt_ref, oh_ref, o_ref):
    hi, mid, lo = _split3(tt_ref[...])
    oh = oh_ref[...]
    o_ref[...] = _dot(hi, oh) + _dot(mid, oh) + _dot(lo, oh)


def _bias_tables(rel_bias_table, dist_list):
    sizes = [_round_up(d.size, LANE) for d in dist_list]
    tile = 2048
    total = _round_up(sum(sizes), tile)
    onehot = np.zeros((N_BUCKETS, total), np.float32)
    off = 0
    for d, sz in zip(dist_list, sizes):
        b = _t5_bucket_np(d.reshape(-1))
        onehot[b, off + np.arange(d.size)] = 1.0
        off += sz
    out = pl.pallas_call(
        _bias_expand_kernel,
        grid=(total // tile,),
        in_specs=[pl.BlockSpec((NSA_HEADS, N_BUCKETS), lambda i: (0, 0)),
                  pl.BlockSpec((N_BUCKETS, tile), lambda i: (0, i))],
        out_specs=pl.BlockSpec((NSA_HEADS, tile), lambda i: (0, i)),
        out_shape=jax.ShapeDtypeStruct((NSA_HEADS, total), F32),
        compiler_params=_cparams(("arbitrary",)),
        name="bias_expand",
    )(rel_bias_table.T, jnp.asarray(onehot, BF16))
    res, off = [], 0
    for d, sz in zip(dist_list, sizes):
        res.append(out[:, off:off + d.size].reshape((NSA_HEADS,) + d.shape))
        off += sz
    return res


def _z_layout(d_model):
    off = {}
    off["q"] = 0
    off["cq"] = NSA_HEADS * HEAD_DIM
    off["misc"] = off["cq"] + Q_LORA
    pos = _round_up(off["misc"] + LANE, d_model)
    off["ga"] = pos
    off["gb"] = pos + d_model
    pos = _round_up(off["gb"] + d_model, 3 * KV_W)
    off["kv"] = pos
    off["ckv"] = off["kv"] + 3 * KV_W
    off["end"] = off["ckv"] + KV_LORA
    return off


def _proj_kernel(x_ref, g_ref, w_ref, z_ref, h_ref):
    @pl.when(pl.program_id(1) == 0)
    def _():
        x = x_ref[...]
        ms = jnp.mean(x * x, axis=-1, keepdims=True)
        h_ref[...] = (x * lax.rsqrt(ms + EPS) * g_ref[...]).astype(BF16)

    z_ref[...] = _dot(h_ref[...], w_ref[...])


def _proj(x2d, g_norm, w_cat, tn):
    n, d = x2d.shape
    zw = w_cat.shape[1]
    tm = min(512, n)
    return pl.pallas_call(
        _proj_kernel,
        grid=(n // tm, zw // tn),
        in_specs=[pl.BlockSpec((tm, d), lambda i, j: (i, 0)),
                  pl.BlockSpec((1, d), lambda i, j: (0, 0)),
                  pl.BlockSpec((d, tn), lambda i, j: (0, j))],
        out_specs=pl.BlockSpec((tm, tn), lambda i, j: (i, j)),
        out_shape=jax.ShapeDtypeStruct((n, zw), F32),
        scratch_shapes=[pltpu.VMEM((tm, d), BF16)],
        compiler_params=_cparams(("parallel", "arbitrary")),
        name="in_proj",
    )(x2d, g_norm, w_cat)


def _post_kernel(zq_ref, zcq_ref, zkv_ref, zckv_ref, zmisc_ref, cos_ref, sina_ref, sinb_ref,
                 gq_ref, gks_ref, gkw_ref, gcq_ref, wuq_ref, gmn_ref, gmr_ref, gckv_ref, gkr_ref, wuk_ref,
                 e64_ref, e64t_ref, e128_ref, e128t_ref,
                 qn_ref, kvc_ref, kvs_ref, kvw_ref, kvsg_ref, kvwg_ref,
                 ckv_ref, ckvb_ref, misc_ref, krp_ref, qa_ref, qr_ref, *feature_major_refs):
    e64, e64t = e64_ref[...], e64t_ref[...]
    e128, e128t = e128_ref[...], e128t_ref[...]
    nq = NSA_HEADS * HEAD_DIM

    zq = zq_ref[...]
    qn = zq * _seg_rinv(zq, e64, e64t, HEAD_DIM) * gq_ref[...] * (HEAD_DIM ** -0.5)
    qn_ref[...] = qn.astype(BF16)
    if feature_major_refs:
        feature_major_refs[0][...] = qn.T.astype(BF16)

    zkv = zkv_ref[...]
    kvc_ref[...] = zkv[:, :KV_W]
    half = KV_W // 2
    for idx, (g_ref, out_ref, outg_ref) in enumerate(((gks_ref, kvs_ref, kvsg_ref), (gkw_ref, kvw_ref, kvwg_ref))):
        kv = zkv[:, (idx + 1) * KV_W:(idx + 2) * KV_W]
        k, v = kv[:, :half], kv[:, half:]
        kn = k * _seg_rinv(k, e64[:half], e64t[:, :half], HEAD_DIM) * g_ref[...]
        out_ref[...] = jnp.concatenate([kn, v], axis=1)
        for g in range(NSA_KV_HEADS):
            sl = slice(g * HEAD_DIM, (g + 1) * HEAD_DIM)
            kvg = jnp.concatenate([kn[:, sl], v[:, sl]], axis=1)
            outg_ref[g] = kvg.astype(BF16)
            if feature_major_refs:
                for sub in range(kvg.shape[0] // Q_BLOCK):
                    feature_major_refs[1 + idx][g, sub] = kvg[sub * Q_BLOCK:(sub + 1) * Q_BLOCK].T.astype(BF16)

    zc = zckv_ref[...]
    ckv = zc * lax.rsqrt(jnp.mean(zc * zc, axis=-1, keepdims=True) + EPS) * gckv_ref[...]
    ckv_ref[...] = ckv
    ckvb_ref[...] = ckv.astype(BF16)

    cos, sina, sinb = cos_ref[...], sina_ref[...], sinb_ref[...]
    zm = zmisc_ref[...]
    lane = lax.broadcasted_iota(jnp.int32, zm.shape, 1)
    is_kr = lane < QK_ROPE
    ms = jnp.sum(jnp.where(is_kr, zm * zm, 0.0), axis=-1, keepdims=True) / QK_ROPE
    krn = zm * lax.rsqrt(ms + EPS) * gkr_ref[...]
    half_r = QK_ROPE // 2
    kr = krn * cos + pltpu.roll(krn, half_r, 1) * sina + pltpu.roll(krn, LANE - half_r, 1) * sinb
    kr = jnp.where(is_kr, kr, 0.0)
    gates = jnp.where(lane < MISC_GATE_OFF + GATE_N_W, jax.nn.sigmoid(zm), 0.0)
    misc_ref[...] = jnp.where(is_kr, kr, gates)
    krp_ref[...] = kr.astype(BF16)

    zcq = zcq_ref[...]
    cqn = (zcq * lax.rsqrt(jnp.mean(zcq * zcq, axis=-1, keepdims=True) + EPS) * gcq_ref[...]).astype(BF16)
    qm = _dot(cqn, wuq_ref[...])
    nope, ropep = qm[:, :nq], qm[:, nq:]
    ss = _dot_sel(nope * nope, e64) + _dot_sel(ropep * ropep, e128)
    r = lax.rsqrt(ss / (QK_NOPE + QK_ROPE) + EPS)
    scale = (QK_NOPE + QK_ROPE) ** -0.5
    nope_n = nope * _dot_sel(r, e64t) * gmn_ref[...] * scale
    rope_n = ropep * _dot_sel(r, e128t) * gmr_ref[...] * scale
    wr = MLA_HEADS * LANE
    cos_t = jnp.concatenate([cos] * MLA_HEADS, axis=1)
    sina_t = jnp.concatenate([sina] * MLA_HEADS, axis=1)
    sinb_t = jnp.concatenate([sinb] * MLA_HEADS, axis=1)
    qr = rope_n * cos_t + pltpu.roll(rope_n, half_r, 1) * sina_t + pltpu.roll(rope_n, wr - half_r, 1) * sinb_t
    qr_ref[...] = qr.astype(BF16)
    nb = nope_n.astype(BF16)
    for h in range(MLA_HEADS):
        qa = _dot_nt(nb[:, h * QK_NOPE:(h + 1) * QK_NOPE], wuk_ref[h])
        qa_ref[:, h * KV_LORA:(h + 1) * KV_LORA] = qa.astype(BF16)


def _post(z, zoff, tabs, prm, feature_major):
    n = z.shape[0]
    tm = min(256, n)
    nq = NSA_HEADS * HEAD_DIM
    row = lambda w, c: pl.BlockSpec((tm, w), lambda i, c=c: (i, c))
    full = lambda a: pl.BlockSpec(a.shape, lambda i, nd=a.ndim: (0,) * nd)
    cos, sina, sinb = tabs
    consts = [prm["gq"], prm["gks"], prm["gkw"], prm["gcq"], prm["wuq"], prm["gmn"], prm["gmr"], prm["gckv"],
              prm["gkr"], prm["wuk"], prm["e64"], prm["e64t"], prm["e128"], prm["e128t"]]
    in_specs = [row(nq, zoff["q"] // nq), row(Q_LORA, zoff["cq"] // Q_LORA), row(3 * KV_W, zoff["kv"] // (3 * KV_W)),
                row(KV_LORA, zoff["ckv"] // KV_LORA), row(LANE, zoff["misc"] // LANE),
                row(LANE, 0), row(LANE, 0), row(LANE, 0)] + [full(a) for a in consts]
    sds = jax.ShapeDtypeStruct
    out_shape = [sds((n, nq), BF16), sds((n, KV_W), F32), sds((n, KV_W), F32), sds((n, KV_W), F32),
                 sds((NSA_KV_HEADS, n, LANE), BF16), sds((NSA_KV_HEADS, n, LANE), BF16),
                 sds((n, KV_LORA), F32), sds((n, KV_LORA), BF16), sds((n, LANE), F32), sds((n, LANE), BF16),
                 sds((n, MLA_HEADS * KV_LORA), BF16), sds((n, MLA_HEADS * LANE), BF16)]
    grp = pl.BlockSpec((NSA_KV_HEADS, tm, LANE), lambda i: (0, i, 0))
    out_specs = [row(nq, 0), row(KV_W, 0), row(KV_W, 0), row(KV_W, 0), grp, grp,
                 row(KV_LORA, 0), row(KV_LORA, 0), row(LANE, 0), row(LANE, 0),
                 row(MLA_HEADS * KV_LORA, 0), row(MLA_HEADS * LANE, 0)]
    names = ["qn", "kvc", "kvs", "kvw", "kvsg", "kvwg", "ckv", "ckvb", "misc", "krp", "qa", "qr"]
    if feature_major:
        tiles = (NSA_KV_HEADS, n // Q_BLOCK, LANE, Q_BLOCK)
        grpt = pl.BlockSpec((NSA_KV_HEADS, tm // Q_BLOCK, LANE, Q_BLOCK), lambda i: (0, i, 0, 0))
        out_shape += [sds((nq, n), BF16), sds(tiles, BF16), sds(tiles, BF16)]
        out_specs += [pl.BlockSpec((nq, tm), lambda i: (0, i)), grpt, grpt]
        names += ["qnt", "kvsgt", "kvwgt"]
    outs = pl.pallas_call(
        _post_kernel, grid=(n // tm,), in_specs=in_specs, out_specs=out_specs, out_shape=out_shape,
        compiler_params=_cparams(("parallel",)), name="post_proj",
    )(z, z, z, z, z, cos, sina, sinb, *consts)
    return dict(zip(names, outs))


def _chunk_partials(rows, wlo, whi):
    ch = rows.reshape(rows.shape[0] // CMP_STRIDE, CMP_STRIDE, rows.shape[1])
    return jnp.sum(ch * wlo[None], axis=1), jnp.sum(ch * whi[None], axis=1)


def _blocks_to_groups(lo, hi, gkc, e64, e64t):
    nr = lo.shape[0]
    rid = lax.broadcasted_iota(jnp.int32, hi.shape, 0)
    hi_next = jnp.where(rid < nr - 1, pltpu.roll(hi, nr - 1, 0), 0.0)
    blk = lo + hi_next
    half = KV_W // 2
    k, v = blk[:, :half], blk[:, half:]
    kn = k * _seg_rinv(k, e64[:half], e64t[:, :half], HEAD_DIM) * gkc
    out = []
    for g in range(NSA_KV_HEADS):
        sl = slice(g * HEAD_DIM, (g + 1) * HEAD_DIM)
        out.append(jnp.concatenate([kn[:, sl], v[:, sl]], axis=1))
    return out


def _cmp_prompt_kernel(kvc_ref, wlo_ref, whi_ref, gkc_ref, e64_ref, e64t_ref, o_ref, ot_ref):
    lo, hi = _chunk_partials(kvc_ref[0], wlo_ref[...], whi_ref[...])
    groups = _blocks_to_groups(lo, hi, gkc_ref[...], e64_ref[...], e64t_ref[...])
    for g in range(NSA_KV_HEADS):
        o_ref[0, g] = groups[g].astype(BF16)
        ot_ref[0, g] = groups[g].T.astype(BF16)


def _cmp_prompt(kvc, prm):
    b, s, _ = kvc.shape
    nc = s // CMP_STRIDE
    full = lambda a: pl.BlockSpec(a.shape, lambda i, nd=a.ndim: (0,) * nd)
    consts = [prm["wlo"], prm["whi"], prm["gkc"], prm["e64"], prm["e64t"]]
    return pl.pallas_call(
        _cmp_prompt_kernel, grid=(b,),
        in_specs=[pl.BlockSpec((1, s, KV_W), lambda i: (i, 0, 0))] + [full(a) for a in consts],
        out_specs=[pl.BlockSpec((1, NSA_KV_HEADS, nc, LANE), lambda i: (i, 0, 0, 0)),
                   pl.BlockSpec((1, NSA_KV_HEADS, LANE, nc), lambda i: (i, 0, 0, 0))],
        out_shape=[jax.ShapeDtypeStruct((b, NSA_KV_HEADS, nc, LANE), BF16),
                   jax.ShapeDtypeStruct((b, NSA_KV_HEADS, LANE, nc), BF16)],
        compiler_params=_cparams(("parallel",)), name="cmp_blocks_prompt",
    )(kvc, *consts)


def _softmax_update(s, v, m_ref, l_ref, acc_ref, v_feature_major=False):
    m_old = m_ref[...]
    m_new = jnp.maximum(m_old, jnp.max(s, axis=-1, keepdims=True))
    p = jnp.exp(s - jnp.maximum(m_new, NEG_CLAMP))
    alpha = jnp.exp(m_old - m_new)
    l_ref[...] = alpha * l_ref[...] + jnp.sum(p, axis=-1, keepdims=True)
    pv = _dot_nt(p.astype(BF16), v) if v_feature_major else _dot(p.astype(BF16), v)
    acc_ref[...] = alpha * acc_ref[...] + pv
    m_ref[...] = m_new


def _softmax_init(m_ref, l_ref, acc_ref):
    m_ref[...] = jnp.full(m_ref.shape, NEG, F32)
    l_ref[...] = jnp.zeros(l_ref.shape, F32)
    acc_ref[...] = jnp.zeros(acc_ref.shape, F32)


def _col_softmax_update(s, kvt, m_ref, l_ref, acc_ref):
    m_old = m_ref[...]
    m_new = jnp.maximum(m_old, jnp.max(s, axis=0, keepdims=True))
    p = jnp.exp(s - jnp.maximum(m_new, NEG_CLAMP))
    alpha = jnp.exp(m_old - m_new)
    l_ref[...] = alpha * l_ref[...] + jnp.sum(p, axis=0, keepdims=True)
    acc_ref[...] = alpha * acc_ref[...] + _dot(kvt, p.astype(BF16))
    m_ref[...] = m_new


def _nsa_prompt_kernel(qt_ref, misc_ref, kvc_ref, kvct_ref, kvs_ref, kvst_ref, kvw_ref, kvwt_ref,
                       t01_ref, bwin_ref, c31_ref, at_ref, o_ref,
                       qt_s, sel_s, ocmp_s, osel_s, m_ref, l_ref, acc_ref, sc_ref):
    i = pl.program_id(1)
    qb = Q_BLOCK
    nc = kvc_ref.shape[2]
    nb = at_ref.shape[0]
    bw = bwin_ref.shape[0]
    gate_t = misc_ref[0].T
    tk2 = lax.broadcasted_iota(jnp.int32, (qb, qb), 0)
    tq2 = lax.broadcasted_iota(jnp.int32, (qb, qb), 1)
    causal = tk2 <= tq2
    anti = tk2 > tq2
    tile8 = lambda m: jnp.concatenate([m] * NSA_GROUP, axis=1)
    lanes = lambda g: slice(g * NSA_GROUP * qb, (g + 1) * NSA_GROUP * qb)
    for g in range(NSA_KV_HEADS):
        h0 = g * NSA_GROUP
        ls = lanes(g)
        qt = jnp.concatenate([qt_ref[(h0 + h) * HEAD_DIM:(h0 + h + 1) * HEAD_DIM, :] for h in range(NSA_GROUP)], axis=1)
        qt = jnp.concatenate([qt, jnp.zeros_like(qt)], axis=0)
        qt_s[g] = qt
        c31 = c31_ref[:, ls]

        kvc = kvc_ref[0, g]
        s = _dot(kvc, qt)
        shift_c = lax.broadcasted_iota(jnp.int32, (nc, bw), 0)
        shift_j = lax.broadcasted_iota(jnp.int32, (nc, bw), 1)
        first = (qb // CMP_STRIDE) * i - (bw - qb // CMP_STRIDE)
        shift = (shift_c - shift_j == first).astype(BF16)
        hi, mid, lo = _split3(bwin_ref[:, ls])
        placed = _dot(shift, hi) + _dot(shift, mid) + _dot(shift, lo)
        bias = jnp.where(lax.broadcasted_iota(jnp.int32, (nc, 1), 0) < first, c31, placed)
        c_id = lax.broadcasted_iota(jnp.int32, (nc, qb), 0)
        t_id = lax.broadcasted_iota(jnp.int32, (nc, qb), 1)
        mask_c = tile8((qb * i + t_id - CMP_STRIDE * c_id - (CMP_BLOCK - 1)) >= 0)
        s = jnp.where(mask_c, s + bias, NEG)
        e = jnp.exp(s - jnp.max(s, axis=0, keepdims=True))
        p = jnp.where(mask_c, e * (1.0 / jnp.sum(e, axis=0, keepdims=True)), 0.0)
        ocmp_s[g] = _dot(kvct_ref[0, g], p.astype(BF16))[HEAD_DIM:]

        pg = p[:, :qb]
        for h in range(1, NSA_GROUP):
            pg = pg + p[:, h * qb:(h + 1) * qb]
        hi, mid, lo = _split3(pg)
        at = at_ref[...]
        sc = _dot(at, hi) + _dot(at, mid) + _dot(at, lo)
        j_id = lax.broadcasted_iota(jnp.int32, (nb, qb), 0)
        qpos = qb * i + lax.broadcasted_iota(jnp.int32, (nb, qb), 1)
        cur = qpos // SEL_BLOCK
        forced = (j_id == 0) | (j_id == cur) | (j_id == cur - 1)
        valid = j_id * SEL_BLOCK <= qpos
        sc = jnp.where(valid, jnp.where(forced, sc + FORCE_BONUS, sc), NEG)
        sc_ref[...] = sc

        def rank_body(r, rank):
            sr = sc_ref[pl.ds(r, 1), :]
            ahead = (sr > sc) | ((sr == sc) & (r < j_id))
            return rank + ahead.astype(jnp.int32)

        rank = lax.fori_loop(0, nb, rank_body, jnp.zeros((nb, qb), jnp.int32))
        sel_s[g] = (rank < min(N_SEL, nb)).astype(F32).astype(BF16)

    def sel_mask(g, kt):
        ek = lax.broadcasted_iota(jnp.int32, (qb, nb), 0)
        ej = lax.broadcasted_iota(jnp.int32, (qb, nb), 1)
        expand = (ej == (qb // SEL_BLOCK) * kt + ek // SEL_BLOCK).astype(BF16)
        return _dot(expand, sel_s[g]) > 0.5

    def tile(g, kv_ref, kvt_ref, kt, near, mask):
        kv = kv_ref[g, 0, pl.ds(pl.multiple_of(kt * qb, qb), qb), :]
        st = _dot(kv, qt_s[g])
        if near is not None:
            st = st + t01_ref[near, :, lanes(g)]
        if mask is not None:
            st = jnp.where(tile8(mask), st, NEG)
        _col_softmax_update(st, kvt_ref[g, 0, kt], m_ref.at[g], l_ref.at[g], acc_ref.at[g])

    groups = range(NSA_KV_HEADS)

    _softmax_init(m_ref, l_ref, acc_ref)

    def far_body(kt, carry):
        for g in groups:
            tile(g, kvs_ref, kvst_ref, kt, None, sel_mask(g, kt))
        return carry

    lax.fori_loop(0, jnp.maximum(i - 1, 0), far_body, 0)

    @pl.when(i >= 1)
    def _():
        for g in groups:
            tile(g, kvs_ref, kvst_ref, i - 1, 1, sel_mask(g, i - 1))

    for g in groups:
        tile(g, kvs_ref, kvst_ref, i, 0, sel_mask(g, i) & causal)
    for g in groups:
        osel_s[g] = (acc_ref[g] * (1.0 / l_ref[g]))[HEAD_DIM:]

    _softmax_init(m_ref, l_ref, acc_ref)
    nwin = WINDOW // qb
    for diff in range(nwin, -1, -1):
        near = diff if diff <= 1 else None
        mask = causal if diff == 0 else (anti if diff == nwin else None)
        if diff == 0:
            for g in groups:
                tile(g, kvw_ref, kvwt_ref, i, near, mask)
        else:
            @pl.when(i >= diff)
            def _(diff=diff, near=near, mask=mask):
                for g in groups:
                    tile(g, kvw_ref, kvwt_ref, i - diff, near, mask)

    pieces = []
    for g in groups:
        o_win = (acc_ref[g] * (1.0 / l_ref[g]))[HEAD_DIM:]
        for h in range(NSA_GROUP):
            cs = slice(h * qb, (h + 1) * qb)
            r0 = MISC_GATE_OFF + g * NSA_GROUP + h
            gc = gate_t[r0:r0 + 1, :]
            gs = gate_t[r0 + NSA_HEADS:r0 + NSA_HEADS + 1, :]
            gw = gate_t[r0 + 2 * NSA_HEADS:r0 + 2 * NSA_HEADS + 1, :]
            oh = gc * ocmp_s[g, :, cs] + gs * osel_s[g, :, cs] + gw * o_win[:, cs]
            pieces.append(oh.T)
    o_ref[0] = jnp.concatenate(pieces, axis=1).astype(BF16)


def _nsa_prompt(qnt, misc, kvcg, kvcgt, kvsg, kvsgt, kvwg, kvwgt, t01t, bwint, c31row, at, b, s):
    nqb = s // Q_BLOCK
    nq = NSA_HEADS * HEAD_DIM
    nc = s // CMP_STRIDE
    nb = s // SEL_BLOCK
    cols = NSA_GROUP * Q_BLOCK
    full = lambda a: pl.BlockSpec(a.shape, lambda bi, i, nd=a.ndim: (0,) * nd)
    rowm = pl.BlockSpec((NSA_KV_HEADS, 1, s, LANE), lambda bi, i: (0, bi, 0, 0))
    featm = pl.BlockSpec((NSA_KV_HEADS, 1, nqb, LANE, Q_BLOCK), lambda bi, i: (0, bi, 0, 0, 0))
    return pl.pallas_call(
        _nsa_prompt_kernel, grid=(b, nqb),
        in_specs=[pl.BlockSpec((nq, Q_BLOCK), lambda bi, i: (0, bi * nqb + i)),
                  pl.BlockSpec((1, Q_BLOCK, LANE), lambda bi, i: (bi, i, 0)),
                  pl.BlockSpec((1, NSA_KV_HEADS, nc, LANE), lambda bi, i: (bi, 0, 0, 0)),
                  pl.BlockSpec((1, NSA_KV_HEADS, LANE, nc), lambda bi, i: (bi, 0, 0, 0)),
                  rowm, featm, rowm, featm, full(t01t), full(bwint), full(c31row), full(at)],
        out_specs=pl.BlockSpec((1, Q_BLOCK, nq), lambda bi, i: (bi, i, 0)),
        out_shape=jax.ShapeDtypeStruct((b, s, nq), BF16),
        scratch_shapes=[pltpu.VMEM((NSA_KV_HEADS, LANE, cols), BF16), pltpu.VMEM((NSA_KV_HEADS, nb, Q_BLOCK), BF16),
                        pltpu.VMEM((NSA_KV_HEADS, HEAD_DIM, cols), F32), pltpu.VMEM((NSA_KV_HEADS, HEAD_DIM, cols), F32),
                        pltpu.VMEM((NSA_KV_HEADS, 1, cols), F32), pltpu.VMEM((NSA_KV_HEADS, 1, cols), F32),
                        pltpu.VMEM((NSA_KV_HEADS, LANE, cols), F32), pltpu.VMEM((nb, Q_BLOCK), F32)],
        compiler_params=_cparams(("parallel", "arbitrary")), name="nsa_prompt",
    )(qnt, misc.reshape(b, s, LANE), kvcg, kvcgt,
      kvsg.reshape(NSA_KV_HEADS, b, s, LANE), kvsgt.reshape(NSA_KV_HEADS, b, nqb, LANE, Q_BLOCK),
      kvwg.reshape(NSA_KV_HEADS, b, s, LANE), kvwgt.reshape(NSA_KV_HEADS, b, nqb, LANE, Q_BLOCK),
      t01t, bwint, c31row, at)


def _mla_prompt_kernel(qa_ref, qr_ref, ckv_ref, krp_ref, wuv_ref, o_ref, qa_s, qr_s, m_ref, l_ref, acc_ref, *, tk):
    i = pl.program_id(1)
    qb = Q_BLOCK
    nsplit = qa_s.shape[0]
    hps = MLA_HEADS // nsplit
    rows = hps * qb
    for sp in range(nsplit):
        hs = range(sp * hps, (sp + 1) * hps)
        qa_s[sp] = jnp.concatenate([qa_ref[0, :, h * KV_LORA:(h + 1) * KV_LORA] for h in hs], axis=0)
        qr_s[sp] = jnp.concatenate([qr_ref[0, :, h * LANE:(h + 1) * LANE] for h in hs], axis=0)
    _softmax_init(m_ref, l_ref, acc_ref)

    def tile(kt, masked):
        ks = pl.ds(pl.multiple_of(kt * tk, tk), tk)
        ck = ckv_ref[0, ks, :]
        kr = krp_ref[0, ks, :]
        if masked:
            tq = qb * i + lax.broadcasted_iota(jnp.int32, (qb, tk), 0)
            kp = kt * tk + lax.broadcasted_iota(jnp.int32, (qb, tk), 1)
            m3 = jnp.broadcast_to((kp <= tq)[None], (hps, qb, tk))
        for sp in range(nsplit):
            s = _dot_nt(qa_s[sp], ck) + _dot_nt(qr_s[sp], kr)
            state = (m_ref.at[sp], l_ref.at[sp], acc_ref.at[sp])
            if masked:
                s = jnp.where(m3, s.reshape(hps, qb, tk), NEG).reshape(rows, tk)
            _softmax_update(s, ck, *state)

    nfull = (i * qb) // tk

    def body(kt, carry):
        tile(kt, False)
        return carry

    lax.fori_loop(0, nfull, body, 0)
    tile(nfull, True)
    pieces = []
    for sp in range(nsplit):
        o_lat = (acc_ref[sp] / l_ref[sp]).astype(BF16)
        pieces += [_dot(o_lat[h * qb:(h + 1) * qb], wuv_ref[sp * hps + h]) for h in range(hps)]
    o_ref[0] = jnp.concatenate(pieces, axis=1).astype(BF16)


def _mla_prompt(qa, qr, ckvb, krp, wuv, b, s):
    nqb = s // Q_BLOCK
    tk = min(1024, s)
    nsplit = 4
    rows = MLA_HEADS * Q_BLOCK // nsplit
    wa, wr = MLA_HEADS * KV_LORA, MLA_HEADS * LANE
    return pl.pallas_call(
        functools.partial(_mla_prompt_kernel, tk=tk), grid=(b, nqb),
        in_specs=[pl.BlockSpec((1, Q_BLOCK, wa), lambda bi, i: (bi, i, 0)),
                  pl.BlockSpec((1, Q_BLOCK, wr), lambda bi, i: (bi, i, 0)),
                  pl.BlockSpec((1, s, KV_LORA), lambda bi, i: (bi, 0, 0)),
                  pl.BlockSpec((1, s, LANE), lambda bi, i: (bi, 0, 0)),
                  pl.BlockSpec(wuv.shape, lambda bi, i: (0, 0, 0))],
        out_specs=pl.BlockSpec((1, Q_BLOCK, MLA_HEADS * V_DIM), lambda bi, i: (bi, i, 0)),
        out_shape=jax.ShapeDtypeStruct((b, s, MLA_HEADS * V_DIM), BF16),
        scratch_shapes=[pltpu.VMEM((nsplit, rows, KV_LORA), BF16), pltpu.VMEM((nsplit, rows, LANE), BF16),
                        pltpu.VMEM((nsplit, rows, 1), F32), pltpu.VMEM((nsplit, rows, 1), F32),
                        pltpu.VMEM((nsplit, rows, KV_LORA), F32)],
        compiler_params=_cparams(("parallel", "arbitrary")), name="mla_prompt",
    )(qa.reshape(b, s, wa), qr.reshape(b, s, wr), ckvb.reshape(b, s, KV_LORA), krp.reshape(b, s, LANE), wuv)


def _outproj_kernel(x_ref, on_ref, om_ref, ga_ref, gb_ref, wpn_ref, wpm_ref, wo_ref, gf_ref, x1_ref, h2_ref):
    a = _dot(on_ref[...], wpn_ref[...])
    b = _dot(om_ref[...], wpm_ref[...])
    mix = jax.nn.sigmoid(ga_ref[...]) * a + jax.nn.sigmoid(gb_ref[...]) * b
    x1 = x_ref[...] + _dot(mix.astype(BF16), wo_ref[...])
    x1_ref[...] = x1
    ms = jnp.mean(x1 * x1, axis=-1, keepdims=True)
    h2_ref[...] = (x1 * lax.rsqrt(ms + EPS) * gf_ref[...]).astype(BF16)


def _outproj(x2d, o_nsa, o_mla, z, zoff, prm):
    n, d = x2d.shape
    tm = min(256, n)
    row = lambda w, c: pl.BlockSpec((tm, w), lambda i, c=c: (i, c))
    once = lambda a: pl.BlockSpec(a.shape, lambda i, nd=a.ndim: (0,) * nd, pipeline_mode=pl.Buffered(1))
    return pl.pallas_call(
        _outproj_kernel, grid=(n // tm,),
        in_specs=[row(d, 0), row(o_nsa.shape[1], 0), row(o_mla.shape[1], 0), row(d, zoff["ga"] // d),
                  row(d, zoff["gb"] // d), once(prm["wpn"]), once(prm["wpm"]), once(prm["wo"]), once(prm["gf"])],
        out_specs=[row(d, 0), row(d, 0)],
        out_shape=[jax.ShapeDtypeStruct((n, d), F32), jax.ShapeDtypeStruct((n, d), BF16)],
        compiler_params=_cparams(("parallel",)), name="out_proj",
    )(x2d, o_nsa, o_mla, z, z, prm["wpn"], prm["wpm"], prm["wo"], prm["gf"])


def _ffn_kernel(h_ref, x1_ref, wg_ref, wu_ref, wd_ref, y_ref, acc_ref):
    f = pl.program_id(1)

    @pl.when(f == 0)
    def _():
        acc_ref[...] = jnp.zeros(acc_ref.shape, F32)

    h = h_ref[...]
    t = jax.nn.silu(_dot(h, wg_ref[...])) * _dot(h, wu_ref[...])
    acc_ref[...] += _dot(t.astype(BF16), wd_ref[...])

    @pl.when(f == pl.num_programs(1) - 1)
    def _():
        y_ref[...] = x1_ref[...] + acc_ref[...]


def _ffn(h2, x1, prm):
    n, d = x1.shape
    dff = prm["wg"].shape[1]
    tm = min(512, n)
    tf = 512 if dff % 512 == 0 else 256
    return pl.pallas_call(
        _ffn_kernel, grid=(n // tm, dff // tf),
        in_specs=[pl.BlockSpec((tm, d), lambda i, f: (i, 0)), pl.BlockSpec((tm, d), lambda i, f: (i, 0)),
                  pl.BlockSpec((d, tf), lambda i, f: (0, f)), pl.BlockSpec((d, tf), lambda i, f: (0, f)),
                  pl.BlockSpec((tf, d), lambda i, f: (f, 0))],
        out_specs=pl.BlockSpec((tm, d), lambda i, f: (i, 0)),
        out_shape=jax.ShapeDtypeStruct((n, d), F32),
        scratch_shapes=[pltpu.VMEM((tm, d), F32)],
        compiler_params=_cparams(("parallel", "arbitrary")), name="ffn",
    )(h2, x1, prm["wg"], prm["wu"], prm["wd"])


def _page_specs(pp, rows, cols):
    return [pl.BlockSpec((1, rows, cols), functools.partial(lambda b, s, pt, j: (pt[b, s * pp + j], 0, 0), j=j))
            for j in range(pp)]


def _n_chunks(pp):
    return 1


def _feature_major_pages(cache):
    n_pool = cache.shape[0]
    return jnp.swapaxes(cache.reshape(n_pool, PAGE_SIZE, -1), 1, 2)


def _group_queries(q, lanes):
    t = q.shape[0]
    blocks = []
    for g in range(NSA_KV_HEADS):
        qg = jnp.concatenate([q[:, (g * NSA_GROUP + h) * HEAD_DIM:(g * NSA_GROUP + h + 1) * HEAD_DIM]
                              for h in range(NSA_GROUP)], axis=0)
        parts = []
        if g > 0:
            parts.append(jnp.zeros((NSA_GROUP * t, g * HEAD_DIM), q.dtype))
        parts.append(qg)
        parts.append(jnp.zeros((NSA_GROUP * t, lanes - (g + 1) * HEAD_DIM), q.dtype))
        blocks.append(jnp.concatenate(parts, axis=1))
    return jnp.concatenate(blocks, axis=0)


def _smp_cmp_kernel(pt_ref, *refs, pp, past_chunks, nsb):
    pages = refs[:pp]
    (new_ref, q_ref, wlo_ref, whi_ref, wt_ref, gkc_ref, e64_ref, e64t_ref, bias_ref, a_ref,
     ocmp_ref, score_ref, lo_ref, hi_ref) = refs[pp:]
    s = pl.program_id(1)
    nr = lo_ref.shape[0]
    t = q_ref.shape[0]
    wlo, whi = wlo_ref[...], whi_ref[...]
    cpp = PAGE_SIZE // CMP_STRIDE
    half = KV_W // 2

    @pl.when(s == 0)
    def _():
        lo_ref[past_chunks:, :] = jnp.zeros((nr - past_chunks, KV_W), F32)
        hi_ref[past_chunks:, :] = jnp.zeros((nr - past_chunks, KV_W), F32)

    x = jnp.concatenate([pages[j][0] for j in range(pp)], axis=0).astype(BF16)
    r = _dot_nt(wt_ref[...].astype(BF16), x)
    for j in range(pp):
        c0 = j * KV_W
        r0 = pl.multiple_of((s * pp + j) * cpp, cpp)
        lo_ref[pl.ds(r0, cpp), :] = jnp.concatenate([r[0:cpp, c0:c0 + half],
                                                     r[2 * cpp:3 * cpp, c0 + half:c0 + KV_W]], axis=1)
        hi_ref[pl.ds(r0, cpp), :] = jnp.concatenate([r[cpp:2 * cpp, c0:c0 + half],
                                                     r[3 * cpp:4 * cpp, c0 + half:c0 + KV_W]], axis=1)

    @pl.when(s == pl.num_programs(1) - 1)
    def _():
        new = jnp.concatenate([new_ref[0], jnp.zeros((CMP_STRIDE - t, KV_W), F32)], axis=0)
        lo_n, hi_n = _chunk_partials(new, wlo, whi)
        pad = jnp.zeros((cpp - 1, KV_W), F32)
        lo_ref[past_chunks:past_chunks + cpp, :] = jnp.concatenate([lo_n, pad], axis=0)
        hi_ref[past_chunks:past_chunks + cpp, :] = jnp.concatenate([hi_n, pad], axis=0)
        groups = _blocks_to_groups(lo_ref[...], hi_ref[...], gkc_ref[...], e64_ref[...], e64t_ref[...])
        q = q_ref[...]
        nbp = a_ref.shape[1]
        past = past_chunks * CMP_STRIDE
        c_id = lax.broadcasted_iota(jnp.int32, (t, nr), 1)
        t_id = lax.broadcasted_iota(jnp.int32, (t, nr), 0)
        mask_c = (past + t_id - CMP_STRIDE * c_id - (CMP_BLOCK - 1)) >= 0
        j_id = lax.broadcasted_iota(jnp.int32, (t, nbp), 1)
        qpos = past + lax.broadcasted_iota(jnp.int32, (t, nbp), 0)
        cur = qpos // SEL_BLOCK
        forced = (j_id == 0) | (j_id == cur) | (j_id == cur - 1)
        valid = j_id * SEL_BLOCK <= qpos
        for g in range(NSA_KV_HEADS):
            h0 = g * NSA_GROUP
            qg = jnp.concatenate([q[:, (h0 + h) * HEAD_DIM:(h0 + h + 1) * HEAD_DIM] for h in range(NSA_GROUP)], axis=0)
            qp = jnp.concatenate([qg, jnp.zeros_like(qg)], axis=1)
            kv = groups[g].astype(BF16)
            sc = _dot_nt(qp, kv).reshape(NSA_GROUP, t, nr) + bias_ref[h0:h0 + NSA_GROUP]
            sc = jnp.where(mask_c[None], sc, NEG)
            e = jnp.exp(sc - jnp.max(sc, axis=-1, keepdims=True))
            p = jnp.where(mask_c[None], e / jnp.sum(e, axis=-1, keepdims=True), 0.0)
            ocmp_ref[0, g * NSA_GROUP * t:(g + 1) * NSA_GROUP * t, :] = _dot(
                p.reshape(NSA_GROUP * t, nr).astype(BF16), kv)
            score = _dot_sel(jnp.sum(p, axis=0), a_ref[...])
            score = jnp.where(valid, jnp.where(forced, score + FORCE_BONUS, score), NEG)
            score_ref[0, g * t:(g + 1) * t, :] = jnp.where(j_id < nsb, score, PAD_SCORE)


def _smp_cmp(page_table, cache_cmp, kvc_new, qn, bias_c, a_mat, prm, pp):
    bd, n_pages = page_table.shape
    t = kvc_new.shape[1]
    past_chunks = n_pages * PAGE_SIZE // CMP_STRIDE
    nr = a_mat.shape[0]
    nbp = a_mat.shape[1]
    nsb = (past_chunks + SEL_BLOCK // CMP_STRIDE) // (SEL_BLOCK // CMP_STRIDE)
    rows = NSA_HEADS * t
    full = lambda a: pl.BlockSpec(a.shape, lambda b, s, pt, nd=a.ndim: (0,) * nd)
    consts = [prm["wlo"], prm["whi"], prm["wt"], prm["gkc"], prm["e64"], prm["e64t"], bias_c, a_mat]
    gs = pltpu.PrefetchScalarGridSpec(
        num_scalar_prefetch=1, grid=(bd, n_pages // pp),
        in_specs=_page_specs(pp, KV_W, PAGE_SIZE) + [pl.BlockSpec((1, t, KV_W), lambda b, s, pt: (b, 0, 0)),
                                          pl.BlockSpec((t, qn.shape[1]), lambda b, s, pt: (b, 0))]
        + [full(a) for a in consts],
        out_specs=[pl.BlockSpec((1, rows, LANE), lambda b, s, pt: (b, 0, 0)),
                   pl.BlockSpec((1, NSA_KV_HEADS * t, nbp), lambda b, s, pt: (b, 0, 0))],
        scratch_shapes=[pltpu.VMEM((nr, KV_W), F32), pltpu.VMEM((nr, KV_W), F32)])
    return pl.pallas_call(
        functools.partial(_smp_cmp_kernel, pp=pp, past_chunks=past_chunks, nsb=nsb), grid_spec=gs,
        out_shape=[jax.ShapeDtypeStruct((bd, rows, LANE), F32),
                   jax.ShapeDtypeStruct((bd, NSA_KV_HEADS * t, nbp), F32)],
        compiler_params=_cparams(("parallel", "arbitrary")), name="sample_cmp",
    )(page_table, *([cache_cmp] * pp), kvc_new, qn, *consts)


def _topk_mask_kernel(score_ref, exp_ref, o_ref, sel_ref, sc_ref, *, nsb):
    @pl.when(pl.program_id(0) == 0)
    def _():
        sc = score_ref[...].T
        sc_ref[...] = sc
        j_id = lax.broadcasted_iota(jnp.int32, sc.shape, 0)

        def rank_body(r, rank):
            sr = sc_ref[pl.ds(r, 1), :]
            ahead = (sr > sc) | ((sr == sc) & (r < j_id))
            return rank + ahead.astype(jnp.int32)

        rank = lax.fori_loop(0, nsb, rank_body, jnp.zeros(sc.shape, jnp.int32))
        sel_ref[...] = ((rank < min(N_SEL, nsb)) & (j_id < nsb)).astype(F32).T.astype(BF16)

    rows = sel_ref.shape[0]
    rc = min(512, rows)
    for r in range(0, rows, rc):
        o_ref[r:r + rc, :] = _dot(sel_ref[r:r + rc, :], exp_ref[...]).astype(BF16)


def _topk_mask(score2d, expand, nsb, tile):
    rows, nbp = score2d.shape
    keys = expand.shape[1]
    return pl.pallas_call(
        functools.partial(_topk_mask_kernel, nsb=nsb), grid=(keys // tile,),
        in_specs=[pl.BlockSpec((rows, nbp), lambda k: (0, 0)), pl.BlockSpec((nbp, tile), lambda k: (0, k))],
        out_specs=pl.BlockSpec((rows, tile), lambda k: (0, k)),
        out_shape=jax.ShapeDtypeStruct((rows, keys), BF16),
        scratch_shapes=[pltpu.VMEM((rows, nbp), BF16), pltpu.VMEM((nbp, rows), F32)],
        compiler_params=_cparams(("arbitrary",)), name="sample_topk_mask",
    )(score2d, expand)


def _rows_pad(a, rows):
    return jnp.concatenate([a, jnp.zeros((rows - a.shape[0], a.shape[1]), a.dtype)], axis=0)


def _smp_sel_kernel(pt_ref, *refs, pp):
    pages = refs[:pp]
    (mask_ref, maskt_ref, q_ref, new_ref, blast_ref, bnew_ref, o_ref, qp_ref, m_ref, l_ref, acc_ref) = refs[pp:]
    s = pl.program_id(1)
    t = q_ref.shape[0]
    rows = NSA_HEADS * t
    last = pl.num_programs(1) - 1

    @pl.when(s == 0)
    def _():
        qp_ref[...] = _group_queries(q_ref[...], KV_W)
        _softmax_init(m_ref, l_ref, acc_ref)

    qp = qp_ref[...]

    def expand_mask(mk, nk):
        m4 = jnp.broadcast_to(mk.astype(F32).reshape(NSA_KV_HEADS, 1, t, nk), (NSA_KV_HEADS, NSA_GROUP, t, nk))
        return m4.reshape(rows, nk) > 0.5

    nl = blast_ref.shape[1]
    nch = _n_chunks(pp)
    cp = pp // nch
    nk = cp * PAGE_SIZE

    def chunk(c, bias):
        keys = jnp.concatenate([pages[c * cp + j][0] for j in range(cp)], axis=1).astype(BF16)
        mask = expand_mask(mask_ref[0, :, c * nk:(c + 1) * nk], nk)
        raw = _dot(qp, keys)
        st = jnp.where(mask, raw if bias is None else raw + bias, NEG)
        _softmax_update(st, keys, m_ref, l_ref, acc_ref, v_feature_major=True)

    for c in range(nch - 1):
        chunk(c, None)

    @pl.when(s != last)
    def _():
        chunk(nch - 1, None)

    @pl.when(s == last)
    def _():
        far = [jnp.zeros((rows, nk - nl), F32)] if nk > nl else []
        chunk(nch - 1, jnp.concatenate(far + [blast_ref[...]], axis=1))
        kn = _rows_pad(new_ref[0], LANE).astype(BF16)
        j_id = lax.broadcasted_iota(jnp.int32, (rows, LANE), 1)
        t_id = lax.broadcasted_iota(jnp.int32, (rows, LANE), 0) % t
        mt = expand_mask(maskt_ref[0][:, :LANE], LANE) & (j_id <= t_id) & (j_id < t)
        st = jnp.where(mt, _dot_nt(qp, kn) + bnew_ref[...], NEG)
        _softmax_update(st, kn, m_ref, l_ref, acc_ref)
        o = acc_ref[...] / l_ref[...]
        half = KV_W // 2
        hr = NSA_GROUP * t
        o_ref[0] = jnp.concatenate([o[g * hr:(g + 1) * hr, half + g * HEAD_DIM:half + (g + 1) * HEAD_DIM]
                                    for g in range(NSA_KV_HEADS)], axis=0)


def _smp_sel(page_table, cache_slc, mask3, qn, kvs_new, blast, bnew, pp):
    bd, n_pages = page_table.shape
    t = kvs_new.shape[1]
    rows = NSA_HEADS * t
    nk = pp * PAGE_SIZE
    nsteps = n_pages // pp
    full = lambda a: pl.BlockSpec(a.shape, lambda b, s, pt, nd=a.ndim: (0,) * nd)
    gs = pltpu.PrefetchScalarGridSpec(
        num_scalar_prefetch=1, grid=(bd, nsteps),
        in_specs=_page_specs(pp, KV_W, PAGE_SIZE) + [
            pl.BlockSpec((1, NSA_KV_HEADS * t, nk), lambda b, s, pt: (b, 0, s)),
            pl.BlockSpec((1, NSA_KV_HEADS * t, nk), lambda b, s, pt: (b, 0, nsteps)),
            pl.BlockSpec((t, qn.shape[1]), lambda b, s, pt: (b, 0)),
            pl.BlockSpec((1, t, KV_W), lambda b, s, pt: (b, 0, 0)),
            full(blast), full(bnew)],
        out_specs=pl.BlockSpec((1, rows, HEAD_DIM), lambda b, s, pt: (b, 0, 0)),
        scratch_shapes=[pltpu.VMEM((rows, KV_W), BF16), pltpu.VMEM((rows, 1), F32), pltpu.VMEM((rows, 1), F32),
                        pltpu.VMEM((rows, KV_W), F32)])
    return pl.pallas_call(
        functools.partial(_smp_sel_kernel, pp=pp), grid_spec=gs,
        out_shape=jax.ShapeDtypeStruct((bd, rows, HEAD_DIM), F32),
        compiler_params=_cparams(("parallel", "arbitrary")), name="sample_sel",
    )(page_table, *([cache_slc] * pp), mask3, mask3, qn, kvs_new, blast, bnew)


def _smp_mla_kernel(pt_ref, *refs, pp):
    cpages = refs[:pp]
    rpages = refs[pp:2 * pp]
    (qa_ref, qr_ref, cnew_ref, rnew_ref, wuv_ref, o_ref, qa_s, qr_s, m_ref, l_ref, acc_ref) = refs[2 * pp:]
    s = pl.program_id(1)
    t = qa_ref.shape[0]
    rows = MLA_HEADS * t

    @pl.when(s == 0)
    def _():
        qa_s[...] = jnp.concatenate([qa_ref[:, h * KV_LORA:(h + 1) * KV_LORA] for h in range(MLA_HEADS)], axis=0)
        qr_s[...] = jnp.concatenate([qr_ref[:, h * LANE:(h + 1) * LANE] for h in range(MLA_HEADS)], axis=0)
        _softmax_init(m_ref, l_ref, acc_ref)

    qa, qr = qa_s[...], qr_s[...]
    nch = _n_chunks(pp)
    cp = pp // nch
    for c in range(nch):
        js = range(c * cp, (c + 1) * cp)
        ck = jnp.concatenate([cpages[j][0] for j in js], axis=0).astype(BF16)
        kr = jnp.concatenate([rpages[j][0] for j in js], axis=1).astype(BF16)
        st = _dot_nt(qa, ck) + _dot(qr[:, :QK_ROPE], kr)
        _softmax_update(st, ck, m_ref, l_ref, acc_ref)

    @pl.when(s == pl.num_programs(1) - 1)
    def _():
        cn = _rows_pad(cnew_ref[0], LANE)
        rn = _rows_pad(rnew_ref[0], LANE)
        j_id = lax.broadcasted_iota(jnp.int32, (rows, LANE), 1)
        t_id = lax.broadcasted_iota(jnp.int32, (rows, LANE), 0) % t
        mt = (j_id <= t_id) & (j_id < t)
        sn = jnp.where(mt, _dot_nt(qa, cn) + _dot_nt(qr, rn), NEG)
        _softmax_update(sn, cn, m_ref, l_ref, acc_ref)
        o_lat = (acc_ref[...] / l_ref[...]).astype(BF16)
        o_ref[...] = jnp.concatenate([_dot(o_lat[h * t:(h + 1) * t], wuv_ref[h]) for h in range(MLA_HEADS)],
                                     axis=1).astype(BF16)


def _smp_mla(page_table, cache_ckv, cache_kr, qa, qr, ckvb_new, krp_new, wuv, pp):
    bd, n_pages = page_table.shape
    t = ckvb_new.shape[1]
    rows = MLA_HEADS * t
    gs = pltpu.PrefetchScalarGridSpec(
        num_scalar_prefetch=1, grid=(bd, n_pages // pp),
        in_specs=_page_specs(pp, PAGE_SIZE, KV_LORA) + _page_specs(pp, QK_ROPE, PAGE_SIZE) + [
            pl.BlockSpec((t, qa.shape[1]), lambda b, s, pt: (b, 0)),
            pl.BlockSpec((t, qr.shape[1]), lambda b, s, pt: (b, 0)),
            pl.BlockSpec((1, t, KV_LORA), lambda b, s, pt: (b, 0, 0)),
            pl.BlockSpec((1, t, LANE), lambda b, s, pt: (b, 0, 0)),
            pl.BlockSpec(wuv.shape, lambda b, s, pt: (0, 0, 0))],
        out_specs=pl.BlockSpec((t, MLA_HEADS * V_DIM), lambda b, s, pt: (b, 0)),
        scratch_shapes=[pltpu.VMEM((rows, KV_LORA), BF16), pltpu.VMEM((rows, LANE), BF16),
                        pltpu.VMEM((rows, 1), F32), pltpu.VMEM((rows, 1), F32), pltpu.VMEM((rows, KV_LORA), F32)])
    return pl.pallas_call(
        functools.partial(_smp_mla_kernel, pp=pp), grid_spec=gs,
        out_shape=jax.ShapeDtypeStruct((bd * t, MLA_HEADS * V_DIM), BF16),
        compiler_params=_cparams(("parallel", "arbitrary")), name="sample_mla",
    )(page_table, *([cache_ckv] * pp), *([cache_kr] * pp), qa, qr, ckvb_new, krp_new, wuv)


def _smp_win_kernel(state_ref, new_ref, q_ref, misc_ref, ocmp_ref, osel_ref, bias_ref, o_ref, win_ref):
    t = q_ref.shape[0]
    wl = state_ref.shape[1]
    rows = NSA_HEADS * t
    state = state_ref[0]
    new = new_ref[0]
    win_ref[0, :wl - t, :] = state[t:, :]
    win_ref[0, wl - t:, :] = new
    nkp = bias_ref.shape[1]
    ctx = jnp.concatenate([state, new, jnp.zeros((nkp - wl - t, KV_W), F32)], axis=0).astype(BF16)
    qp = _group_queries(q_ref[...], KV_W)
    j_id = lax.broadcasted_iota(jnp.int32, (rows, nkp), 1)
    t_id = lax.broadcasted_iota(jnp.int32, (rows, nkp), 0) % t
    dist = wl + t_id - j_id
    mask = (dist >= 0) & (dist < WINDOW) & (j_id < wl + t)
    s = jnp.where(mask, _dot_nt(qp, ctx) + bias_ref[...], NEG)
    e = jnp.exp(s - jnp.max(s, axis=-1, keepdims=True))
    p = jnp.where(mask, e / jnp.sum(e, axis=-1, keepdims=True), 0.0)
    o_win = _dot(p.astype(BF16), ctx)
    misc = misc_ref[...]
    ocmp = ocmp_ref[0]
    osel = osel_ref[0]
    half = KV_W // 2
    pieces = []
    for hh in range(NSA_HEADS):
        g = hh // NSA_GROUP
        rs = slice(hh * t, (hh + 1) * t)
        gc = misc[:, MISC_GATE_OFF + hh:MISC_GATE_OFF + hh + 1]
        gsel = misc[:, MISC_GATE_OFF + NSA_HEADS + hh:MISC_GATE_OFF + NSA_HEADS + hh + 1]
        gw = misc[:, MISC_GATE_OFF + 2 * NSA_HEADS + hh:MISC_GATE_OFF + 2 * NSA_HEADS + hh + 1]
        pieces.append(gc * ocmp[rs, HEAD_DIM:] + gsel * osel[rs, :]
                      + gw * o_win[rs, half + g * HEAD_DIM:half + (g + 1) * HEAD_DIM])
    o_ref[...] = jnp.concatenate(pieces, axis=1).astype(BF16)


def _smp_win(state, kvw_new, qn, misc, ocmp, osel, bias_w):
    bd, wl, _ = state.shape
    t = kvw_new.shape[1]
    rows = NSA_HEADS * t
    nq = NSA_HEADS * HEAD_DIM
    return pl.pallas_call(
        _smp_win_kernel, grid=(bd,),
        in_specs=[pl.BlockSpec((1, wl, KV_W), lambda b: (b, 0, 0)), pl.BlockSpec((1, t, KV_W), lambda b: (b, 0, 0)),
                  pl.BlockSpec((t, nq), lambda b: (b, 0)), pl.BlockSpec((t, LANE), lambda b: (b, 0)),
                  pl.BlockSpec((1, rows, LANE), lambda b: (b, 0, 0)), pl.BlockSpec((1, rows, HEAD_DIM), lambda b: (b, 0, 0)),
                  pl.BlockSpec(bias_w.shape, lambda b: (0, 0))],
        out_specs=[pl.BlockSpec((t, nq), lambda b: (b, 0)), pl.BlockSpec((1, wl, KV_W), lambda b: (b, 0, 0))],
        out_shape=[jax.ShapeDtypeStruct((bd * t, nq), BF16), jax.ShapeDtypeStruct((bd, wl, KV_W), F32)],
        compiler_params=_cparams(("parallel",)), name="sample_window_merge",
    )(state, kvw_new, qn, misc, ocmp, osel, bias_w)


def _prep_params(d, attn_norm, w_in, g_q_nsa, g_k_cmp, g_k_slc, g_k_win, phi_cmp, g_cq, w_uq, g_q_mla, g_ckv,
                 g_krope, w_uk, w_uv, w_proj_nsa, w_proj_mla, w_out, ffn_norm, w_gate, w_up, w_down):
    nq = NSA_HEADS * HEAD_DIM
    widths = (nq, KV_W, KV_W, KV_W, GATE_N_W, Q_LORA, KV_LORA, QK_ROPE, d, d)
    names = ("q", "kvc", "kvs", "kvw", "gn", "cq", "ckv", "kr", "ga", "gb")
    splits = np.cumsum(widths)[:-1]
    seg = dict(zip(names, jnp.split(w_in, splits, axis=1)))
    zoff = _z_layout(d)
    tn = 1024 if d >= 1024 else 256
    zw = _round_up(zoff["end"], tn)
    zc = lambda w: jnp.zeros((d, w), w_in.dtype)
    cols = [seg["q"], seg["cq"], seg["kr"], seg["gn"], zc(zoff["ga"] - zoff["misc"] - QK_ROPE - GATE_N_W),
            seg["ga"], seg["gb"], zc(zoff["kv"] - zoff["gb"] - d), seg["kvc"], seg["kvs"], seg["kvw"], seg["ckv"],
            zc(zw - zoff["end"])]
    prm = {"w_cat": jnp.concatenate(cols, axis=1).astype(BF16), "tn": tn}
    prm["g_attn"] = attn_norm.reshape(1, d)
    prm["gq"] = jnp.tile(g_q_nsa, NSA_HEADS).reshape(1, nq)
    prm["gks"] = jnp.tile(g_k_slc, NSA_KV_HEADS).reshape(1, LANE)
    prm["gkw"] = jnp.tile(g_k_win, NSA_KV_HEADS).reshape(1, LANE)
    prm["gkc"] = jnp.tile(g_k_cmp, NSA_KV_HEADS).reshape(1, LANE)
    prm["gcq"] = g_cq.reshape(1, Q_LORA)
    wq = w_uq.reshape(Q_LORA, MLA_HEADS, QK_NOPE + QK_ROPE)
    w_rope = jnp.pad(wq[:, :, QK_NOPE:], ((0, 0), (0, 0), (0, LANE - QK_ROPE)))
    prm["wuq"] = jnp.concatenate([wq[:, :, :QK_NOPE].reshape(Q_LORA, MLA_HEADS * QK_NOPE),
                                  w_rope.reshape(Q_LORA, MLA_HEADS * LANE)], axis=1).astype(BF16)
    prm["gmn"] = jnp.tile(g_q_mla[:QK_NOPE], MLA_HEADS).reshape(1, MLA_HEADS * QK_NOPE)
    prm["gmr"] = jnp.tile(jnp.pad(g_q_mla[QK_NOPE:], (0, LANE - QK_ROPE)), MLA_HEADS).reshape(1, MLA_HEADS * LANE)
    prm["gckv"] = g_ckv.reshape(1, KV_LORA)
    prm["gkr"] = jnp.pad(g_krope, (0, LANE - QK_ROPE)).reshape(1, LANE)
    prm["wuk"] = w_uk.astype(BF16)
    prm["wuv"] = w_uv.astype(BF16)
    e64 = _seg_indicator(nq, HEAD_DIM)
    e128 = _seg_indicator(MLA_HEADS * LANE, LANE)
    prm["e64"], prm["e64t"] = jnp.asarray(e64, BF16), jnp.asarray(e64.T, BF16)
    prm["e128"], prm["e128t"] = jnp.asarray(e128, BF16), jnp.asarray(e128.T, BF16)
    ph = jnp.repeat(phi_cmp, KV_W // 2, axis=0).T
    prm["wlo"], prm["whi"] = ph[:CMP_STRIDE], ph[CMP_STRIDE:]
    eye = jnp.eye(PAGE_SIZE // CMP_STRIDE, dtype=F32)
    prm["wt"] = jnp.concatenate([jnp.kron(eye, phi_cmp[e, part * CMP_STRIDE:(part + 1) * CMP_STRIDE][None, :])
                                 for e in range(2) for part in range(2)], axis=0)
    prm["wpn"], prm["wpm"], prm["wo"] = w_proj_nsa.astype(BF16), w_proj_mla.astype(BF16), w_out.astype(BF16)
    prm["gf"] = ffn_norm.reshape(1, d)
    prm["wg"], prm["wu"], prm["wd"] = w_gate.astype(BF16), w_up.astype(BF16), w_down.astype(BF16)
    return prm, zoff


def _rope_tables(pos):
    half = QK_ROPE // 2
    inv_freq = ROPE_THETA ** (-jnp.arange(half, dtype=jnp.float32) / half)
    ang = pos.astype(jnp.float32)[:, None] * inv_freq
    cos, sin = jnp.cos(ang), jnp.sin(ang)
    z = jnp.zeros((pos.shape[0], LANE - QK_ROPE), F32)
    zh = jnp.zeros_like(sin)
    return (jnp.concatenate([cos, cos, z], axis=1), jnp.concatenate([zh, sin, z], axis=1),
            jnp.concatenate([-sin, zh, z], axis=1))


def _layer(xp, xs, cache_cmp, cache_slc, cache_ckv, cache_kr, win_state, page_table, rel_bias_table, weights):
    b, s, d = xp.shape
    bd, t, _ = xs.shape
    n_pages = page_table.shape[1]
    past = n_pages * PAGE_SIZE
    n_pool = cache_cmp.shape[0]
    wl = win_state.shape[1]
    prm, zoff = _prep_params(d, *weights)
    pp = max(1, min(32, n_pages // 2))
    nsteps = n_pages // pp

    qb = Q_BLOCK
    ar = np.arange
    d_t01 = (ar(qb)[None, :, None] - ar(qb)[None, None, :]) + qb * ar(2)[:, None, None]
    bw = 32
    d_bwin = ar(qb)[:, None] - CMP_STRIDE * (ar(bw)[None, :] - (bw - qb // CMP_STRIDE)) - (CMP_BLOCK - 1)
    d_c31 = np.full((LANE,), MAX_DISTANCE)
    past_chunks = past // CMP_STRIDE
    nc_s = past_chunks + SEL_BLOCK // CMP_STRIDE
    nr_s = _round_up(nc_s, LANE)
    nsb = nc_s // (SEL_BLOCK // CMP_STRIDE)
    nbp = _round_up(nsb, LANE)
    d_cmp_s = past + ar(t)[:, None] - CMP_STRIDE * ar(nr_s)[None, :] - (CMP_BLOCK - 1)
    nl = min(2 * LANE, pp * PAGE_SIZE)
    d_last = nl + ar(t)[:, None] - ar(nl)[None, :]
    d_new = ar(t)[:, None] - ar(LANE)[None, :]
    nkp = _round_up(wl + t, LANE)
    d_win = wl + ar(t)[:, None] - ar(nkp)[None, :]
    t01, bwin, c31, bias_c, blast, bnew, bias_w = _bias_tables(
        rel_bias_table, [d_t01, d_bwin, d_c31, d_cmp_s, d_last, d_new, d_win])
    c31row = jnp.repeat(c31[:, 0], qb).reshape(1, NSA_HEADS * qb)
    t01t = t01.transpose(1, 3, 0, 2).reshape(2, qb, NSA_HEADS * qb) - c31row[None]
    bwint = bwin.transpose(2, 0, 1).reshape(bw, NSA_HEADS * qb)
    rows_s = NSA_HEADS * t
    c31r = jnp.repeat(c31[:, :1], t, axis=0)
    blast = blast.reshape(rows_s, nl) - c31r
    bnew = bnew.reshape(rows_s, LANE) - c31r
    bias_w = bias_w.reshape(rows_s, nkp)
    at = jnp.asarray(_score_matrix(s // CMP_STRIDE, s // SEL_BLOCK).T, BF16)
    a_s = np.zeros((nr_s, nbp), np.float32)
    a_s[:nc_s] = _score_matrix(nc_s, nbp)
    a_s = jnp.asarray(a_s, BF16)
    keys_pad = (nsteps + 1) * pp * PAGE_SIZE
    expand = np.zeros((nbp, keys_pad), np.float32)
    kk = ar(nsb * SEL_BLOCK)
    expand[kk // SEL_BLOCK, kk] = 1.0
    expand = jnp.asarray(expand, BF16)

    pos_p = jnp.arange(s)
    tabs_p = tuple(jnp.tile(a, (b, 1)) for a in _rope_tables(pos_p))
    zp = _proj(xp.reshape(b * s, d), prm["g_attn"], prm["w_cat"], prm["tn"])
    pp_ = _post(zp, zoff, tabs_p, prm, True)
    kvcg, kvcgt = _cmp_prompt(pp_["kvc"].reshape(b, s, KV_W), prm)
    o_nsa_p = _nsa_prompt(pp_["qnt"], pp_["misc"], kvcg, kvcgt, pp_["kvsg"], pp_["kvsgt"], pp_["kvwg"], pp_["kvwgt"],
                          t01t, bwint, c31row, at, b, s)
    o_mla_p = _mla_prompt(pp_["qa"], pp_["qr"], pp_["ckvb"], pp_["krp"], prm["wuv"], b, s)
    x1p, h2p = _outproj(xp.reshape(b * s, d), o_nsa_p.reshape(b * s, -1), o_mla_p.reshape(b * s, -1), zp, zoff, prm)
    yp = _ffn(h2p, x1p, prm).reshape(b, s, d)

    pos_s = past + jnp.arange(t)
    tabs_s = tuple(jnp.tile(a, (bd, 1)) for a in _rope_tables(pos_s))
    zs = _proj(xs.reshape(bd * t, d), prm["g_attn"], prm["w_cat"], prm["tn"])
    ps = _post(zs, zoff, tabs_s, prm, False)
    cc = _feature_major_pages(cache_cmp)
    cs = _feature_major_pages(cache_slc)
    cache_kr = _feature_major_pages(cache_kr)
    ocmp_s, score = _smp_cmp(page_table, cc, ps["kvc"].reshape(bd, t, KV_W), ps["qn"], bias_c, a_s, prm, pp)
    mask = _topk_mask(score.reshape(bd * NSA_KV_HEADS * t, nbp), expand, nsb, pp * PAGE_SIZE)
    osel_s = _smp_sel(page_table, cs, mask.reshape(bd, NSA_KV_HEADS * t, keys_pad), ps["qn"],
                      ps["kvs"].reshape(bd, t, KV_W), blast, bnew, pp)
    o_mla_s = _smp_mla(page_table, cache_ckv, cache_kr, ps["qa"], ps["qr"], ps["ckvb"].reshape(bd, t, KV_LORA),
                       ps["krp"].reshape(bd, t, LANE), prm["wuv"], pp)
    o_nsa_s, win_s = _smp_win(win_state.reshape(bd, wl, KV_W), ps["kvw"].reshape(bd, t, KV_W), ps["qn"], ps["misc"],
                              ocmp_s, osel_s, bias_w)
    x1s, h2s = _outproj(xs.reshape(bd * t, d), o_nsa_s, o_mla_s, zs, zoff, prm)
    ys = _ffn(h2s, x1s, prm).reshape(bd, t, d)

    kvh = (2, NSA_KV_HEADS, HEAD_DIM)
    wp = min(WINDOW, s)
    return (yp, ys,
            pp_["kvc"].reshape((b, s) + kvh), ps["kvc"].reshape((bd, t) + kvh),
            pp_["kvs"].reshape((b, s) + kvh), ps["kvs"].reshape((bd, t) + kvh),
            pp_["ckv"].reshape(b, s, KV_LORA), ps["ckv"].reshape(bd, t, KV_LORA),
            pp_["misc"][:, :QK_ROPE].reshape(b, s, QK_ROPE), ps["misc"][:, :QK_ROPE].reshape(bd, t, QK_ROPE),
            pp_["kvw"].reshape((b, s) + kvh)[:, s - wp:], win_s.reshape((bd, wl) + kvh))


def kernel(x_prompt, x_sample, cache_cmp_kv, cache_slc_kv, cache_mla_ckv, cache_mla_krope, state_win_kv, page_table, rel_bias_table, attn_norm, w_in, g_q_nsa, g_k_cmp, g_k_slc, g_k_win, phi_cmp, g_cq, w_uq, g_q_mla, g_ckv, g_krope, w_uk, w_uv, w_proj_nsa, w_proj_mla, w_out, ffn_norm, w_gate, w_up, w_down):
    depth = w_in.shape[0]
    xp, xs = x_prompt, x_sample
    per_layer = []
    for l in range(depth):
        weights = (attn_norm[l], w_in[l], g_q_nsa[l], g_k_cmp[l], g_k_slc[l], g_k_win[l], phi_cmp[l], g_cq[l],
                   w_uq[l], g_q_mla[l], g_ckv[l], g_krope[l], w_uk[l], w_uv[l], w_proj_nsa[l], w_proj_mla[l],
                   w_out[l], ffn_norm[l], w_gate[l], w_up[l], w_down[l])
        xp, xs, *st = _layer(xp, xs, cache_cmp_kv[l], cache_slc_kv[l], cache_mla_ckv[l], cache_mla_krope[l],
                             state_win_kv[l], page_table, rel_bias_table, weights)
        per_layer.append(st)
    stacked = [jnp.stack(v) for v in zip(*per_layer)]
    return (xp, xs, *stacked)
```

```python
import functools
import math

import numpy as np
import jax
import jax.numpy as jnp
from jax import lax
from jax.experimental import pallas as pl
from jax.experimental.pallas import tpu as pltpu

F32 = jnp.float32
BF16 = jnp.bfloat16

NSA_HEADS = 16
NSA_KV_HEADS = 2
NSA_GROUP = NSA_HEADS // NSA_KV_HEADS
HEAD_DIM = 64
CMP_BLOCK = 32
CMP_STRIDE = 16
SEL_BLOCK = 64
N_SEL = 16
WINDOW = 512
MLA_HEADS = 16
QK_NOPE = 64
QK_ROPE = 32
V_DIM = 64
KV_LORA = 256
Q_LORA = 512
ROPE_THETA = 10000.0
N_BUCKETS = 32
MAX_DISTANCE = 128
PAGE_SIZE = 128
Q_BLOCK = 128
EPS = 1e-6
NEG = -1e30
NEG_CLAMP = -1e29
PAD_SCORE = -3e38
FORCE_BONUS = 1e6

LANE = 128
KV_W = 2 * NSA_KV_HEADS * HEAD_DIM
GATE_N_W = 3 * NSA_HEADS
MISC_GATE_OFF = QK_ROPE
VMEM_LIMIT = 56 * 1024 * 1024


def _round_up(a, b):
    return -(-a // b) * b


def _cparams(sem):
    return pltpu.CompilerParams(dimension_semantics=sem, vmem_limit_bytes=VMEM_LIMIT)


def _dot(a, b):
    return jnp.dot(a, b, preferred_element_type=F32)


def _dot_nt(a, b):
    return lax.dot_general(a, b, (((1,), (1,)), ((), ())), preferred_element_type=F32)


def _split3(x):
    hi = x.astype(BF16)
    r = x - hi.astype(F32)
    mid = r.astype(BF16)
    lo = (r - mid.astype(F32)).astype(BF16)
    return hi, mid, lo


def _dot_sel(x, m):
    hi, mid, lo = _split3(x)
    return _dot(hi, m) + _dot(mid, m) + _dot(lo, m)


def _dot_sel_nt(m, x):
    hi, mid, lo = _split3(x)
    return _dot_nt(m, hi) + _dot_nt(m, mid) + _dot_nt(m, lo)


def _seg_rinv(x, e, et, width):
    ss = _dot_sel(x * x, e)
    r = lax.rsqrt(ss / width + EPS)
    return _dot_sel(r, et)


def _t5_bucket_np(dist):
    n = np.maximum(dist, 0)
    exact = N_BUCKETS // 2
    log_ratio = np.log(np.maximum(n, 1).astype(np.float32) / exact) / math.log(MAX_DISTANCE / exact)
    large = np.minimum(exact + (log_ratio * (N_BUCKETS - exact)).astype(np.int32), N_BUCKETS - 1)
    return np.where(n < exact, n, large)


def _seg_indicator(width, seg):
    e = np.zeros((width, LANE), np.float32)
    e[np.arange(width), np.arange(width) // seg] = 1.0
    return e


def _score_matrix(nc, nb_pad):
    ratio = SEL_BLOCK // CMP_STRIDE
    a = np.zeros((nc, nb_pad), np.float32)
    c = np.arange(nc)
    a[c, c // ratio] = 1.0
    cc = c[(c % ratio == ratio - 1) & (c // ratio + 1 < nc // ratio)]
    a[cc, cc // ratio + 1] = 1.0
    return a


def _bias_expand_kernel(tt_ref, oh_ref, o_ref):
    hi, mid, lo = _split3(tt_ref[...])
    oh = oh_ref[...]
    o_ref[...] = _dot(hi, oh) + _dot(mid, oh) + _dot(lo, oh)


def _bias_tables(rel_bias_table, dist_list):
    sizes = [_round_up(d.size, LANE) for d in dist_list]
    tile = 2048
    total = _round_up(sum(sizes), tile)
    onehot = np.zeros((N_BUCKETS, total), np.float32)
    off = 0
    for d, sz in zip(dist_list, sizes):
        b = _t5_bucket_np(d.reshape(-1))
        onehot[b, off + np.arange(d.size)] = 1.0
        off += sz
    out = pl.pallas_call(
        _bias_expand_kernel,
        grid=(total // tile,),
        in_specs=[pl.BlockSpec((NSA_HEADS, N_BUCKETS), lambda i: (0, 0)),
                  pl.BlockSpec((N_BUCKETS, tile), lambda i: (0, i))],
        out_specs=pl.BlockSpec((NSA_HEADS, tile), lambda i: (0, i)),
        out_shape=jax.ShapeDtypeStruct((NSA_HEADS, total), F32),
        compiler_params=_cparams(("arbitrary",)),
        name="bias_expand",
    )(rel_bias_table.T, jnp.asarray(onehot, BF16))
    res, off = [], 0
    for d, sz in zip(dist_list, sizes):
        res.append(out[:, off:off + d.size].reshape((NSA_HEADS,) + d.shape))
        off += sz
    return res


def _z_layout(d_model):
    off = {}
    off["q"] = 0
    off["cq"] = NSA_HEADS * HEAD_DIM
    off["misc"] = off["cq"] + Q_LORA
    pos = _round_up(off["misc"] + LANE, d_model)
    off["ga"] = pos
    off["gb"] = pos + d_model
    pos = _round_up(off["gb"] + d_model, 3 * KV_W)
    off["kv"] = pos
    off["ckv"] = off["kv"] + 3 * KV_W
    off["end"] = off["ckv"] + KV_LORA
    return off


def _proj_kernel(x_ref, g_ref, w_ref, z_ref, h_ref):
    @pl.when(pl.program_id(1) == 0)
    def _():
        x = x_ref[...]
        ms = jnp.mean(x * x, axis=-1, keepdims=True)
        h_ref[...] = (x * lax.rsqrt(ms + EPS) * g_ref[...]).astype(BF16)

    z_ref[...] = _dot(h_ref[...], w_ref[...])


def _proj(x2d, g_norm, w_cat, tn):
    n, d = x2d.shape
    zw = w_cat.shape[1]
    tm = min(512, n)
    return pl.pallas_call(
        _proj_kernel,
        grid=(n // tm, zw // tn),
        in_specs=[pl.BlockSpec((tm, d), lambda i, j: (i, 0)),
                  pl.BlockSpec((1, d), lambda i, j: (0, 0)),
                  pl.BlockSpec((d, tn), lambda i, j: (0, j))],
        out_specs=pl.BlockSpec((tm, tn), lambda i, j: (i, j)),
        out_shape=jax.ShapeDtypeStruct((n, zw), F32),
        scratch_shapes=[pltpu.VMEM((tm, d), BF16)],
        compiler_params=_cparams(("parallel", "arbitrary")),
        name="in_proj",
    )(x2d, g_norm, w_cat)


def _post_kernel(zq_ref, zcq_ref, zkv_ref, zckv_ref, zmisc_ref, cos_ref, sina_ref, sinb_ref,
                 gq_ref, gks_ref, gkw_ref, gcq_ref, wuq_ref, gmn_ref, gmr_ref, gckv_ref, gkr_ref, wuk_ref,
                 e64_ref, e64t_ref, e128_ref, e128t_ref,
                 qn_ref, kvc_ref, kvs_ref, kvw_ref, kvsg_ref, kvwg_ref,
                 ckv_ref, ckvb_ref, misc_ref, krp_ref, qa_ref, qr_ref, *feature_major_refs):
    e64, e64t = e64_ref[...], e64t_ref[...]
    e128, e128t = e128_ref[...], e128t_ref[...]
    nq = NSA_HEADS * HEAD_DIM

    zq = zq_ref[...]
    qn = zq * _seg_rinv(zq, e64, e64t, HEAD_DIM) * gq_ref[...] * (HEAD_DIM ** -0.5)
    qn_ref[...] = qn.astype(BF16)
    if feature_major_refs:
        feature_major_refs[0][...] = qn.T.astype(BF16)

    zkv = zkv_ref[...]
    kvc_ref[...] = zkv[:, :KV_W]
    half = KV_W // 2
    for idx, (g_ref, out_ref, outg_ref) in enumerate(((gks_ref, kvs_ref, kvsg_ref), (gkw_ref, kvw_ref, kvwg_ref))):
        kv = zkv[:, (idx + 1) * KV_W:(idx + 2) * KV_W]
        k, v = kv[:, :half], kv[:, half:]
        kn = k * _seg_rinv(k, e64[:half], e64t[:, :half], HEAD_DIM) * g_ref[...]
        out_ref[...] = jnp.concatenate([kn, v], axis=1)
        for g in range(NSA_KV_HEADS):
            sl = slice(g * HEAD_DIM, (g + 1) * HEAD_DIM)
            kvg = jnp.concatenate([kn[:, sl], v[:, sl]], axis=1)
            outg_ref[g] = kvg.astype(BF16)
            if feature_major_refs:
                for sub in range(kvg.shape[0] // Q_BLOCK):
                    feature_major_refs[1 + idx][g, sub] = kvg[sub * Q_BLOCK:(sub + 1) * Q_BLOCK].T.astype(BF16)

    zc = zckv_ref[...]
    ckv = zc * lax.rsqrt(jnp.mean(zc * zc, axis=-1, keepdims=True) + EPS) * gckv_ref[...]
    ckv_ref[...] = ckv
    ckvb_ref[...] = ckv.astype(BF16)

    cos, sina, sinb = cos_ref[...], sina_ref[...], sinb_ref[...]
    zm = zmisc_ref[...]
    lane = lax.broadcasted_iota(jnp.int32, zm.shape, 1)
    is_kr = lane < QK_ROPE
    ms = jnp.sum(jnp.where(is_kr, zm * zm, 0.0), axis=-1, keepdims=True) / QK_ROPE
    krn = zm * lax.rsqrt(ms + EPS) * gkr_ref[...]
    half_r = QK_ROPE // 2
    kr = krn * cos + pltpu.roll(krn, half_r, 1) * sina + pltpu.roll(krn, LANE - half_r, 1) * sinb
    kr = jnp.where(is_kr, kr, 0.0)
    gates = jnp.where(lane < MISC_GATE_OFF + GATE_N_W, jax.nn.sigmoid(zm), 0.0)
    misc_ref[...] = jnp.where(is_kr, kr, gates)
    krp_ref[...] = kr.astype(BF16)

    zcq = zcq_ref[...]
    cqn = (zcq * lax.rsqrt(jnp.mean(zcq * zcq, axis=-1, keepdims=True) + EPS) * gcq_ref[...]).astype(BF16)
    qm = _dot(cqn, wuq_ref[...])
    nope, ropep = qm[:, :nq], qm[:, nq:]
    ss = _dot_sel(nope * nope, e64) + _dot_sel(ropep * ropep, e128)
    r = lax.rsqrt(ss / (QK_NOPE + QK_ROPE) + EPS)
    scale = (QK_NOPE + QK_ROPE) ** -0.5
    nope_n = nope * _dot_sel(r, e64t) * gmn_ref[...] * scale
    rope_n = ropep * _dot_sel(r, e128t) * gmr_ref[...] * scale
    wr = MLA_HEADS * LANE
    cos_t = jnp.concatenate([cos] * MLA_HEADS, axis=1)
    sina_t = jnp.concatenate([sina] * MLA_HEADS, axis=1)
    sinb_t = jnp.concatenate([sinb] * MLA_HEADS, axis=1)
    qr = rope_n * cos_t + pltpu.roll(rope_n, half_r, 1) * sina_t + pltpu.roll(rope_n, wr - half_r, 1) * sinb_t
    qr_ref[...] = qr.astype(BF16)
    nb = nope_n.astype(BF16)
    for h in range(MLA_HEADS):
        qa = _dot_nt(nb[:, h * QK_NOPE:(h + 1) * QK_NOPE], wuk_ref[h])
        qa_ref[:, h * KV_LORA:(h + 1) * KV_LORA] = qa.astype(BF16)


def _post(z, zoff, tabs, prm, feature_major):
    n = z.shape[0]
    tm = min(256, n)
    nq = NSA_HEADS * HEAD_DIM
    row = lambda w, c: pl.BlockSpec((tm, w), lambda i, c=c: (i, c))
    full = lambda a: pl.BlockSpec(a.shape, lambda i, nd=a.ndim: (0,) * nd)
    cos, sina, sinb = tabs
    consts = [prm["gq"], prm["gks"], prm["gkw"], prm["gcq"], prm["wuq"], prm["gmn"], prm["gmr"], prm["gckv"],
              prm["gkr"], prm["wuk"], prm["e64"], prm["e64t"], prm["e128"], prm["e128t"]]
    in_specs = [row(nq, zoff["q"] // nq), row(Q_LORA, zoff["cq"] // Q_LORA), row(3 * KV_W, zoff["kv"] // (3 * KV_W)),
                row(KV_LORA, zoff["ckv"] // KV_LORA), row(LANE, zoff["misc"] // LANE),
                row(LANE, 0), row(LANE, 0), row(LANE, 0)] + [full(a) for a in consts]
    sds = jax.ShapeDtypeStruct
    out_shape = [sds((n, nq), BF16), sds((n, KV_W), F32), sds((n, KV_W), F32), sds((n, KV_W), F32),
                 sds((NSA_KV_HEADS, n, LANE), BF16), sds((NSA_KV_HEADS, n, LANE), BF16),
                 sds((n, KV_LORA), F32), sds((n, KV_LORA), BF16), sds((n, LANE), F32), sds((n, LANE), BF16),
                 sds((n, MLA_HEADS * KV_LORA), BF16), sds((n, MLA_HEADS * LANE), BF16)]
    grp = pl.BlockSpec((NSA_KV_HEADS, tm, LANE), lambda i: (0, i, 0))
    out_specs = [row(nq, 0), row(KV_W, 0), row(KV_W, 0), row(KV_W, 0), grp, grp,
                 row(KV_LORA, 0), row(KV_LORA, 0), row(LANE, 0), row(LANE, 0),
                 row(MLA_HEADS * KV_LORA, 0), row(MLA_HEADS * LANE, 0)]
    names = ["qn", "kvc", "kvs", "kvw", "kvsg", "kvwg", "ckv", "ckvb", "misc", "krp", "qa", "qr"]
    if feature_major:
        tiles = (NSA_KV_HEADS, n // Q_BLOCK, LANE, Q_BLOCK)
        grpt = pl.BlockSpec((NSA_KV_HEADS, tm // Q_BLOCK, LANE, Q_BLOCK), lambda i: (0, i, 0, 0))
        out_shape += [sds((nq, n), BF16), sds(tiles, BF16), sds(tiles, BF16)]
        out_specs += [pl.BlockSpec((nq, tm), lambda i: (0, i)), grpt, grpt]
        names += ["qnt", "kvsgt", "kvwgt"]
    outs = pl.pallas_call(
        _post_kernel, grid=(n // tm,), in_specs=in_specs, out_specs=out_specs, out_shape=out_shape,
        compiler_params=_cparams(("parallel",)), name="post_proj",
    )(z, z, z, z, z, cos, sina, sinb, *consts)
    return dict(zip(names, outs))


def _chunk_partials(rows, wlo, whi):
    ch = rows.reshape(rows.shape[0] // CMP_STRIDE, CMP_STRIDE, rows.shape[1])
    return jnp.sum(ch * wlo[None], axis=1), jnp.sum(ch * whi[None], axis=1)


def _blocks_to_groups(lo, hi, gkc, e64, e64t):
    nr = lo.shape[0]
    rid = lax.broadcasted_iota(jnp.int32, hi.shape, 0)
    hi_next = jnp.where(rid < nr - 1, pltpu.roll(hi, nr - 1, 0), 0.0)
    blk = lo + hi_next
    half = KV_W // 2
    k, v = blk[:, :half], blk[:, half:]
    kn = k * _seg_rinv(k, e64[:half], e64t[:, :half], HEAD_DIM) * gkc
    out = []
    for g in range(NSA_KV_HEADS):
        sl = slice(g * HEAD_DIM, (g + 1) * HEAD_DIM)
        out.append(jnp.concatenate([kn[:, sl], v[:, sl]], axis=1))
    return out


def _cmp_prompt_kernel(kvc_ref, wlo_ref, whi_ref, gkc_ref, e64_ref, e64t_ref, o_ref, ot_ref):
    lo, hi = _chunk_partials(kvc_ref[0], wlo_ref[...], whi_ref[...])
    groups = _blocks_to_groups(lo, hi, gkc_ref[...], e64_ref[...], e64t_ref[...])
    for g in range(NSA_KV_HEADS):
        o_ref[0, g] = groups[g].astype(BF16)
        ot_ref[0, g] = groups[g].T.astype(BF16)


def _cmp_prompt(kvc, prm):
    b, s, _ = kvc.shape
    nc = s // CMP_STRIDE
    full = lambda a: pl.BlockSpec(a.shape, lambda i, nd=a.ndim: (0,) * nd)
    consts = [prm["wlo"], prm["whi"], prm["gkc"], prm["e64"], prm["e64t"]]
    return pl.pallas_call(
        _cmp_prompt_kernel, grid=(b,),
        in_specs=[pl.BlockSpec((1, s, KV_W), lambda i: (i, 0, 0))] + [full(a) for a in consts],
        out_specs=[pl.BlockSpec((1, NSA_KV_HEADS, nc, LANE), lambda i: (i, 0, 0, 0)),
                   pl.BlockSpec((1, NSA_KV_HEADS, LANE, nc), lambda i: (i, 0, 0, 0))],
        out_shape=[jax.ShapeDtypeStruct((b, NSA_KV_HEADS, nc, LANE), BF16),
                   jax.ShapeDtypeStruct((b, NSA_KV_HEADS, LANE, nc), BF16)],
        compiler_params=_cparams(("parallel",)), name="cmp_blocks_prompt",
    )(kvc, *consts)


def _softmax_update(s, v, m_ref, l_ref, acc_ref, v_feature_major=False):
    m_old = m_ref[...]
    m_new = jnp.maximum(m_old, jnp.max(s, axis=-1, keepdims=True))
    p = jnp.exp(s - jnp.maximum(m_new, NEG_CLAMP))
    alpha = jnp.exp(m_old - m_new)
    l_ref[...] = alpha * l_ref[...] + jnp.sum(p, axis=-1, keepdims=True)
    pv = _dot_nt(p.astype(BF16), v) if v_feature_major else _dot(p.astype(BF16), v)
    acc_ref[...] = alpha * acc_ref[...] + pv
    m_ref[...] = m_new


def _softmax_init(m_ref, l_ref, acc_ref):
    m_ref[...] = jnp.full(m_ref.shape, NEG, F32)
    l_ref[...] = jnp.zeros(l_ref.shape, F32)
    acc_ref[...] = jnp.zeros(acc_ref.shape, F32)


def _col_softmax_update(s, kvt, m_ref, l_ref, acc_ref):
    m_old = m_ref[...]
    m_new = jnp.maximum(m_old, jnp.max(s, axis=0, keepdims=True))
    p = jnp.exp(s - jnp.maximum(m_new, NEG_CLAMP))
    alpha = jnp.exp(m_old - m_new)
    l_ref[...] = alpha * l_ref[...] + jnp.sum(p, axis=0, keepdims=True)
    acc_ref[...] = alpha * acc_ref[...] + _dot(kvt, p.astype(BF16))
    m_ref[...] = m_new


def _nsa_prompt_kernel(qt_ref, misc_ref, kvc_ref, kvct_ref, kvs_ref, kvst_ref, kvw_ref, kvwt_ref,
                       t01_ref, bwin_ref, c31_ref, at_ref, o_ref,
                       qt_s, sel_s, ocmp_s, osel_s, m_ref, l_ref, acc_ref, sc_ref):
    i = pl.program_id(1)
    qb = Q_BLOCK
    nc = kvc_ref.shape[2]
    nb = at_ref.shape[0]
    bw = bwin_ref.shape[0]
    gate_t = misc_ref[0].T
    tk2 = lax.broadcasted_iota(jnp.int32, (qb, qb), 0)
    tq2 = lax.broadcasted_iota(jnp.int32, (qb, qb), 1)
    causal = tk2 <= tq2
    anti = tk2 > tq2
    tile8 = lambda m: jnp.concatenate([m] * NSA_GROUP, axis=1)
    lanes = lambda g: slice(g * NSA_GROUP * qb, (g + 1) * NSA_GROUP * qb)
    for g in range(NSA_KV_HEADS):
        h0 = g * NSA_GROUP
        ls = lanes(g)
        qt = jnp.concatenate([qt_ref[(h0 + h) * HEAD_DIM:(h0 + h + 1) * HEAD_DIM, :] for h in range(NSA_GROUP)], axis=1)
        qt = jnp.concatenate([qt, jnp.zeros_like(qt)], axis=0)
        qt_s[g] = qt
        c31 = c31_ref[:, ls]

        kvc = kvc_ref[0, g]
        s = _dot(kvc, qt)
        shift_c = lax.broadcasted_iota(jnp.int32, (nc, bw), 0)
        shift_j = lax.broadcasted_iota(jnp.int32, (nc, bw), 1)
        first = (qb // CMP_STRIDE) * i - (bw - qb // CMP_STRIDE)
        shift = (shift_c - shift_j == first).astype(BF16)
        hi, mid, lo = _split3(bwin_ref[:, ls])
        placed = _dot(shift, hi) + _dot(shift, mid) + _dot(shift, lo)
        bias = jnp.where(lax.broadcasted_iota(jnp.int32, (nc, 1), 0) < first, c31, placed)
        c_id = lax.broadcasted_iota(jnp.int32, (nc, qb), 0)
        t_id = lax.broadcasted_iota(jnp.int32, (nc, qb), 1)
        mask_c = tile8((qb * i + t_id - CMP_STRIDE * c_id - (CMP_BLOCK - 1)) >= 0)
        s = jnp.where(mask_c, s + bias, NEG)
        e = jnp.exp(s - jnp.max(s, axis=0, keepdims=True))
        p = jnp.where(mask_c, e * (1.0 / jnp.sum(e, axis=0, keepdims=True)), 0.0)
        ocmp_s[g] = _dot(kvct_ref[0, g], p.astype(BF16))[HEAD_DIM:]

        pg = p[:, :qb]
        for h in range(1, NSA_GROUP):
            pg = pg + p[:, h * qb:(h + 1) * qb]
        hi, mid, lo = _split3(pg)
        at = at_ref[...]
        sc = _dot(at, hi) + _dot(at, mid) + _dot(at, lo)
        j_id = lax.broadcasted_iota(jnp.int32, (nb, qb), 0)
        qpos = qb * i + lax.broadcasted_iota(jnp.int32, (nb, qb), 1)
        cur = qpos // SEL_BLOCK
        forced = (j_id == 0) | (j_id == cur) | (j_id == cur - 1)
        valid = j_id * SEL_BLOCK <= qpos
        sc = jnp.where(valid, jnp.where(forced, sc + FORCE_BONUS, sc), NEG)
        sc_ref[...] = sc

        def rank_body(r, rank):
            sr = sc_ref[pl.ds(r, 1), :]
            ahead = (sr > sc) | ((sr == sc) & (r < j_id))
            return rank + ahead.astype(jnp.int32)

        rank = lax.fori_loop(0, nb, rank_body, jnp.zeros((nb, qb), jnp.int32))
        sel_s[g] = (rank < min(N_SEL, nb)).astype(F32).astype(BF16)

    def sel_mask(g, kt, width=1):
        nk = width * qb
        ek = lax.broadcasted_iota(jnp.int32, (nk, nb), 0)
        ej = lax.broadcasted_iota(jnp.int32, (nk, nb), 1)
        expand = (ej == (qb // SEL_BLOCK) * kt + ek // SEL_BLOCK).astype(BF16)
        return _dot(expand, sel_s[g]) > 0.5

    def tile(g, kv_ref, kvt_ref, kt, near, mask, width=1):
        kv = kv_ref[g, 0, pl.ds(pl.multiple_of(kt * qb, qb), width * qb), :]
        kvt = jnp.concatenate([kvt_ref[g, 0, kt + w] for w in range(width)], axis=1)
        st = _dot(kv, qt_s[g])
        if near is not None:
            st = st + t01_ref[near, :, lanes(g)]
        if mask is not None:
            st = jnp.where(tile8(mask), st, NEG)
        _col_softmax_update(st, kvt, m_ref.at[g], l_ref.at[g], acc_ref.at[g])

    groups = range(NSA_KV_HEADS)

    _softmax_init(m_ref, l_ref, acc_ref)
    n_far = jnp.maximum(i - 1, 0)

    def far_body(pair, carry):
        for g in groups:
            tile(g, kvs_ref, kvst_ref, 2 * pair, None, sel_mask(g, 2 * pair, 2), 2)
        return carry

    lax.fori_loop(0, n_far // 2, far_body, 0)

    @pl.when(n_far % 2 == 1)
    def _():
        for g in groups:
            tile(g, kvs_ref, kvst_ref, n_far - 1, None, sel_mask(g, n_far - 1))

    @pl.when(i >= 1)
    def _():
        for g in groups:
            tile(g, kvs_ref, kvst_ref, i - 1, 1, sel_mask(g, i - 1))

    for g in groups:
        tile(g, kvs_ref, kvst_ref, i, 0, sel_mask(g, i) & causal)
    for g in groups:
        osel_s[g] = (acc_ref[g] * (1.0 / l_ref[g]))[HEAD_DIM:]

    _softmax_init(m_ref, l_ref, acc_ref)
    nwin = WINDOW // qb
    for diff in range(nwin, -1, -1):
        near = diff if diff <= 1 else None
        mask = causal if diff == 0 else (anti if diff == nwin else None)
        if diff == 0:
            for g in groups:
                tile(g, kvw_ref, kvwt_ref, i, near, mask)
        else:
            @pl.when(i >= diff)
            def _(diff=diff, near=near, mask=mask):
                for g in groups:
                    tile(g, kvw_ref, kvwt_ref, i - diff, near, mask)

    pieces = []
    for g in groups:
        o_win = (acc_ref[g] * (1.0 / l_ref[g]))[HEAD_DIM:]
        for h in range(NSA_GROUP):
            cs = slice(h * qb, (h + 1) * qb)
            r0 = MISC_GATE_OFF + g * NSA_GROUP + h
            gc = gate_t[r0:r0 + 1, :]
            gs = gate_t[r0 + NSA_HEADS:r0 + NSA_HEADS + 1, :]
            gw = gate_t[r0 + 2 * NSA_HEADS:r0 + 2 * NSA_HEADS + 1, :]
            oh = gc * ocmp_s[g, :, cs] + gs * osel_s[g, :, cs] + gw * o_win[:, cs]
            pieces.append(oh.T)
    o_ref[0] = jnp.concatenate(pieces, axis=1).astype(BF16)


def _nsa_prompt(qnt, misc, kvcg, kvcgt, kvsg, kvsgt, kvwg, kvwgt, t01t, bwint, c31row, at, b, s):
    nqb = s // Q_BLOCK
    nq = NSA_HEADS * HEAD_DIM
    nc = s // CMP_STRIDE
    nb = s // SEL_BLOCK
    cols = NSA_GROUP * Q_BLOCK
    full = lambda a: pl.BlockSpec(a.shape, lambda bi, i, nd=a.ndim: (0,) * nd)
    rowm = pl.BlockSpec((NSA_KV_HEADS, 1, s, LANE), lambda bi, i: (0, bi, 0, 0))
    featm = pl.BlockSpec((NSA_KV_HEADS, 1, nqb, LANE, Q_BLOCK), lambda bi, i: (0, bi, 0, 0, 0))
    return pl.pallas_call(
        _nsa_prompt_kernel, grid=(b, nqb),
        in_specs=[pl.BlockSpec((nq, Q_BLOCK), lambda bi, i: (0, bi * nqb + i)),
                  pl.BlockSpec((1, Q_BLOCK, LANE), lambda bi, i: (bi, i, 0)),
                  pl.BlockSpec((1, NSA_KV_HEADS, nc, LANE), lambda bi, i: (bi, 0, 0, 0)),
                  pl.BlockSpec((1, NSA_KV_HEADS, LANE, nc), lambda bi, i: (bi, 0, 0, 0)),
                  rowm, featm, rowm, featm, full(t01t), full(bwint), full(c31row), full(at)],
        out_specs=pl.BlockSpec((1, Q_BLOCK, nq), lambda bi, i: (bi, i, 0)),
        out_shape=jax.ShapeDtypeStruct((b, s, nq), BF16),
        scratch_shapes=[pltpu.VMEM((NSA_KV_HEADS, LANE, cols), BF16), pltpu.VMEM((NSA_KV_HEADS, nb, Q_BLOCK), BF16),
                        pltpu.VMEM((NSA_KV_HEADS, HEAD_DIM, cols), F32), pltpu.VMEM((NSA_KV_HEADS, HEAD_DIM, cols), F32),
                        pltpu.VMEM((NSA_KV_HEADS, 1, cols), F32), pltpu.VMEM((NSA_KV_HEADS, 1, cols), F32),
                        pltpu.VMEM((NSA_KV_HEADS, LANE, cols), F32), pltpu.VMEM((nb, Q_BLOCK), F32)],
        compiler_params=_cparams(("parallel", "arbitrary")), name="nsa_prompt",
    )(qnt, misc.reshape(b, s, LANE), kvcg, kvcgt,
      kvsg.reshape(NSA_KV_HEADS, b, s, LANE), kvsgt.reshape(NSA_KV_HEADS, b, nqb, LANE, Q_BLOCK),
      kvwg.reshape(NSA_KV_HEADS, b, s, LANE), kvwgt.reshape(NSA_KV_HEADS, b, nqb, LANE, Q_BLOCK),
      t01t, bwint, c31row, at)


def _mla_prompt_kernel(qa_ref, qr_ref, ckv_ref, krp_ref, wuv_ref, o_ref, qa_s, qr_s, m_ref, l_ref, acc_ref, *, tk):
    i = pl.program_id(1)
    qb = Q_BLOCK
    nsplit = qa_s.shape[0]
    hps = MLA_HEADS // nsplit
    rows = hps * qb
    for sp in range(nsplit):
        hs = range(sp * hps, (sp + 1) * hps)
        qa_s[sp] = jnp.concatenate([qa_ref[0, :, h * KV_LORA:(h + 1) * KV_LORA] for h in hs], axis=0)
        qr_s[sp] = jnp.concatenate([qr_ref[0, :, h * LANE:(h + 1) * LANE] for h in hs], axis=0)
    _softmax_init(m_ref, l_ref, acc_ref)

    def tile(kt, masked):
        ks = pl.ds(pl.multiple_of(kt * tk, tk), tk)
        ck = ckv_ref[0, ks, :]
        kr = krp_ref[0, ks, :]
        if masked:
            tq = qb * i + lax.broadcasted_iota(jnp.int32, (qb, tk), 0)
            kp = kt * tk + lax.broadcasted_iota(jnp.int32, (qb, tk), 1)
            m3 = jnp.broadcast_to((kp <= tq)[None], (hps, qb, tk))
        for sp in range(nsplit):
            s = _dot_nt(qa_s[sp], ck) + _dot_nt(qr_s[sp], kr)
            state = (m_ref.at[sp], l_ref.at[sp], acc_ref.at[sp])
            if masked:
                s = jnp.where(m3, s.reshape(hps, qb, tk), NEG).reshape(rows, tk)
            _softmax_update(s, ck, *state)

    nfull = (i * qb) // tk

    def body(kt, carry):
        tile(kt, False)
        return carry

    lax.fori_loop(0, nfull, body, 0)
    tile(nfull, True)
    pieces = []
    for sp in range(nsplit):
        o_lat = (acc_ref[sp] / l_ref[sp]).astype(BF16)
        pieces += [_dot(o_lat[h * qb:(h + 1) * qb], wuv_ref[sp * hps + h]) for h in range(hps)]
    o_ref[0] = jnp.concatenate(pieces, axis=1).astype(BF16)


def _mla_prompt(qa, qr, ckvb, krp, wuv, b, s):
    nqb = s // Q_BLOCK
    tk = min(1024, s)
    nsplit = 4
    rows = MLA_HEADS * Q_BLOCK // nsplit
    wa, wr = MLA_HEADS * KV_LORA, MLA_HEADS * LANE
    return pl.pallas_call(
        functools.partial(_mla_prompt_kernel, tk=tk), grid=(b, nqb),
        in_specs=[pl.BlockSpec((1, Q_BLOCK, wa), lambda bi, i: (bi, i, 0)),
                  pl.BlockSpec((1, Q_BLOCK, wr), lambda bi, i: (bi, i, 0)),
                  pl.BlockSpec((1, s, KV_LORA), lambda bi, i: (bi, 0, 0)),
                  pl.BlockSpec((1, s, LANE), lambda bi, i: (bi, 0, 0)),
                  pl.BlockSpec(wuv.shape, lambda bi, i: (0, 0, 0))],
        out_specs=pl.BlockSpec((1, Q_BLOCK, MLA_HEADS * V_DIM), lambda bi, i: (bi, i, 0)),
        out_shape=jax.ShapeDtypeStruct((b, s, MLA_HEADS * V_DIM), BF16),
        scratch_shapes=[pltpu.VMEM((nsplit, rows, KV_LORA), BF16), pltpu.VMEM((nsplit, rows, LANE), BF16),
                        pltpu.VMEM((nsplit, rows, 1), F32), pltpu.VMEM((nsplit, rows, 1), F32),
                        pltpu.VMEM((nsplit, rows, KV_LORA), F32)],
        compiler_params=_cparams(("parallel", "arbitrary")), name="mla_prompt",
    )(qa.reshape(b, s, wa), qr.reshape(b, s, wr), ckvb.reshape(b, s, KV_LORA), krp.reshape(b, s, LANE), wuv)


def _outproj_kernel(x_ref, on_ref, om_ref, ga_ref, gb_ref, wpn_ref, wpm_ref, wo_ref, gf_ref, x1_ref, h2_ref):
    a = _dot(on_ref[...], wpn_ref[...])
    b = _dot(om_ref[...], wpm_ref[...])
    mix = jax.nn.sigmoid(ga_ref[...]) * a + jax.nn.sigmoid(gb_ref[...]) * b
    x1 = x_ref[...] + _dot(mix.astype(BF16), wo_ref[...])
    x1_ref[...] = x1
    ms = jnp.mean(x1 * x1, axis=-1, keepdims=True)
    h2_ref[...] = (x1 * lax.rsqrt(ms + EPS) * gf_ref[...]).astype(BF16)


def _outproj(x2d, o_nsa, o_mla, z, zoff, prm):
    n, d = x2d.shape
    tm = min(256, n)
    row = lambda w, c: pl.BlockSpec((tm, w), lambda i, c=c: (i, c))
    once = lambda a: pl.BlockSpec(a.shape, lambda i, nd=a.ndim: (0,) * nd, pipeline_mode=pl.Buffered(1))
    return pl.pallas_call(
        _outproj_kernel, grid=(n // tm,),
        in_specs=[row(d, 0), row(o_nsa.shape[1], 0), row(o_mla.shape[1], 0), row(d, zoff["ga"] // d),
                  row(d, zoff["gb"] // d), once(prm["wpn"]), once(prm["wpm"]), once(prm["wo"]), once(prm["gf"])],
        out_specs=[row(d, 0), row(d, 0)],
        out_shape=[jax.ShapeDtypeStruct((n, d), F32), jax.ShapeDtypeStruct((n, d), BF16)],
        compiler_params=_cparams(("parallel",)), name="out_proj",
    )(x2d, o_nsa, o_mla, z, z, prm["wpn"], prm["wpm"], prm["wo"], prm["gf"])


def _ffn_kernel(h_ref, x1_ref, wg_ref, wu_ref, wd_ref, y_ref, acc_ref):
    f = pl.program_id(1)

    @pl.when(f == 0)
    def _():
        acc_ref[...] = jnp.zeros(acc_ref.shape, F32)

    h = h_ref[...]
    t = jax.nn.silu(_dot(h, wg_ref[...])) * _dot(h, wu_ref[...])
    acc_ref[...] += _dot(t.astype(BF16), wd_ref[...])

    @pl.when(f == pl.num_programs(1) - 1)
    def _():
        y_ref[...] = x1_ref[...] + acc_ref[...]


def _ffn(h2, x1, prm):
    n, d = x1.shape
    dff = prm["wg"].shape[1]
    tm = min(512, n)
    tf = 512 if dff % 512 == 0 else 256
    return pl.pallas_call(
        _ffn_kernel, grid=(n // tm, dff // tf),
        in_specs=[pl.BlockSpec((tm, d), lambda i, f: (i, 0)), pl.BlockSpec((tm, d), lambda i, f: (i, 0)),
                  pl.BlockSpec((d, tf), lambda i, f: (0, f)), pl.BlockSpec((d, tf), lambda i, f: (0, f)),
                  pl.BlockSpec((tf, d), lambda i, f: (f, 0))],
        out_specs=pl.BlockSpec((tm, d), lambda i, f: (i, 0)),
        out_shape=jax.ShapeDtypeStruct((n, d), F32),
        scratch_shapes=[pltpu.VMEM((tm, d), F32)],
        compiler_params=_cparams(("parallel", "arbitrary")), name="ffn",
    )(h2, x1, prm["wg"], prm["wu"], prm["wd"])


def _page_specs(pp, rows, cols):
    return [pl.BlockSpec((1, rows, cols), functools.partial(lambda b, s, pt, j: (pt[b, s * pp + j], 0, 0), j=j))
            for j in range(pp)]


def _n_chunks(pp):
    return 1


def _feature_major_pages(cache):
    n_pool = cache.shape[0]
    return jnp.swapaxes(cache.reshape(n_pool, PAGE_SIZE, -1), 1, 2)


def _group_queries(q, lanes):
    t = q.shape[0]
    blocks = []
    for g in range(NSA_KV_HEADS):
        qg = jnp.concatenate([q[:, (g * NSA_GROUP + h) * HEAD_DIM:(g * NSA_GROUP + h + 1) * HEAD_DIM]
                              for h in range(NSA_GROUP)], axis=0)
        parts = []
        if g > 0:
            parts.append(jnp.zeros((NSA_GROUP * t, g * HEAD_DIM), q.dtype))
        parts.append(qg)
        parts.append(jnp.zeros((NSA_GROUP * t, lanes - (g + 1) * HEAD_DIM), q.dtype))
        blocks.append(jnp.concatenate(parts, axis=1))
    return jnp.concatenate(blocks, axis=0)


def _smp_cmp_kernel(pt_ref, *refs, pp, past_chunks, nsb):
    pages = refs[:pp]
    (new_ref, q_ref, wlo_ref, whi_ref, wt_ref, gkc_ref, e64_ref, e64t_ref, bias_ref, a_ref,
     ocmp_ref, score_ref, lo_ref, hi_ref) = refs[pp:]
    s = pl.program_id(1)
    nr = lo_ref.shape[0]
    t = q_ref.shape[0]
    wlo, whi = wlo_ref[...], whi_ref[...]
    cpp = PAGE_SIZE // CMP_STRIDE
    half = KV_W // 2

    @pl.when(s == 0)
    def _():
        lo_ref[past_chunks:, :] = jnp.zeros((nr - past_chunks, KV_W), F32)
        hi_ref[past_chunks:, :] = jnp.zeros((nr - past_chunks, KV_W), F32)

    x = jnp.concatenate([pages[j][0] for j in range(pp)], axis=0).astype(BF16)
    r = _dot_nt(wt_ref[...].astype(BF16), x)
    for j in range(pp):
        c0 = j * KV_W
        r0 = pl.multiple_of((s * pp + j) * cpp, cpp)
        lo_ref[pl.ds(r0, cpp), :] = jnp.concatenate([r[0:cpp, c0:c0 + half],
                                                     r[2 * cpp:3 * cpp, c0 + half:c0 + KV_W]], axis=1)
        hi_ref[pl.ds(r0, cpp), :] = jnp.concatenate([r[cpp:2 * cpp, c0:c0 + half],
                                                     r[3 * cpp:4 * cpp, c0 + half:c0 + KV_W]], axis=1)

    @pl.when(s == pl.num_programs(1) - 1)
    def _():
        new = jnp.concatenate([new_ref[0], jnp.zeros((CMP_STRIDE - t, KV_W), F32)], axis=0)
        lo_n, hi_n = _chunk_partials(new, wlo, whi)
        pad = jnp.zeros((cpp - 1, KV_W), F32)
        lo_ref[past_chunks:past_chunks + cpp, :] = jnp.concatenate([lo_n, pad], axis=0)
        hi_ref[past_chunks:past_chunks + cpp, :] = jnp.concatenate([hi_n, pad], axis=0)
        groups = _blocks_to_groups(lo_ref[...], hi_ref[...], gkc_ref[...], e64_ref[...], e64t_ref[...])
        q = q_ref[...]
        nbp = a_ref.shape[1]
        past = past_chunks * CMP_STRIDE
        c_id = lax.broadcasted_iota(jnp.int32, (t, nr), 1)
        t_id = lax.broadcasted_iota(jnp.int32, (t, nr), 0)
        mask_c = (past + t_id - CMP_STRIDE * c_id - (CMP_BLOCK - 1)) >= 0
        j_id = lax.broadcasted_iota(jnp.int32, (t, nbp), 1)
        qpos = past + lax.broadcasted_iota(jnp.int32, (t, nbp), 0)
        cur = qpos // SEL_BLOCK
        forced = (j_id == 0) | (j_id == cur) | (j_id == cur - 1)
        valid = j_id * SEL_BLOCK <= qpos
        for g in range(NSA_KV_HEADS):
            h0 = g * NSA_GROUP
            qg = jnp.concatenate([q[:, (h0 + h) * HEAD_DIM:(h0 + h + 1) * HEAD_DIM] for h in range(NSA_GROUP)], axis=0)
            qp = jnp.concatenate([qg, jnp.zeros_like(qg)], axis=1)
            kv = groups[g].astype(BF16)
            sc = _dot_nt(qp, kv).reshape(NSA_GROUP, t, nr) + bias_ref[h0:h0 + NSA_GROUP]
            sc = jnp.where(mask_c[None], sc, NEG)
            e = jnp.exp(sc - jnp.max(sc, axis=-1, keepdims=True))
            p = jnp.where(mask_c[None], e / jnp.sum(e, axis=-1, keepdims=True), 0.0)
            ocmp_ref[0, g * NSA_GROUP * t:(g + 1) * NSA_GROUP * t, :] = _dot(
                p.reshape(NSA_GROUP * t, nr).astype(BF16), kv)
            score = _dot_sel(jnp.sum(p, axis=0), a_ref[...])
            score = jnp.where(valid, jnp.where(forced, score + FORCE_BONUS, score), NEG)
            score_ref[0, g * t:(g + 1) * t, :] = jnp.where(j_id < nsb, score, PAD_SCORE)


def _smp_cmp(page_table, cache_cmp, kvc_new, qn, bias_c, a_mat, prm, pp):
    bd, n_pages = page_table.shape
    t = kvc_new.shape[1]
    past_chunks = n_pages * PAGE_SIZE // CMP_STRIDE
    nr = a_mat.shape[0]
    nbp = a_mat.shape[1]
    nsb = (past_chunks + SEL_BLOCK // CMP_STRIDE) // (SEL_BLOCK // CMP_STRIDE)
    rows = NSA_HEADS * t
    full = lambda a: pl.BlockSpec(a.shape, lambda b, s, pt, nd=a.ndim: (0,) * nd)
    consts = [prm["wlo"], prm["whi"], prm["wt"], prm["gkc"], prm["e64"], prm["e64t"], bias_c, a_mat]
    gs = pltpu.PrefetchScalarGridSpec(
        num_scalar_prefetch=1, grid=(bd, n_pages // pp),
        in_specs=_page_specs(pp, KV_W, PAGE_SIZE) + [pl.BlockSpec((1, t, KV_W), lambda b, s, pt: (b, 0, 0)),
                                          pl.BlockSpec((t, qn.shape[1]), lambda b, s, pt: (b, 0))]
        + [full(a) for a in consts],
        out_specs=[pl.BlockSpec((1, rows, LANE), lambda b, s, pt: (b, 0, 0)),
                   pl.BlockSpec((1, NSA_KV_HEADS * t, nbp), lambda b, s, pt: (b, 0, 0))],
        scratch_shapes=[pltpu.VMEM((nr, KV_W), F32), pltpu.VMEM((nr, KV_W), F32)])
    return pl.pallas_call(
        functools.partial(_smp_cmp_kernel, pp=pp, past_chunks=past_chunks, nsb=nsb), grid_spec=gs,
        out_shape=[jax.ShapeDtypeStruct((bd, rows, LANE), F32),
                   jax.ShapeDtypeStruct((bd, NSA_KV_HEADS * t, nbp), F32)],
        compiler_params=_cparams(("parallel", "arbitrary")), name="sample_cmp",
    )(page_table, *([cache_cmp] * pp), kvc_new, qn, *consts)


def _topk_mask_kernel(score_ref, exp_ref, o_ref, sel_ref, sc_ref, *, nsb):
    @pl.when(pl.program_id(0) == 0)
    def _():
        nbp = score_ref.shape[1]
        nr = sc_ref.shape[0]
        sc = score_ref[...].T[:nr]
        sc_ref[...] = sc
        j_id = lax.broadcasted_iota(jnp.int32, sc.shape, 0)

        def rank_body(r, rank):
            sr = sc_ref[pl.ds(r, 1), :]
            ahead = (sr > sc) | ((sr == sc) & (r < j_id))
            return rank + ahead.astype(jnp.int32)

        rank = lax.fori_loop(0, nsb, rank_body, jnp.zeros(sc.shape, jnp.int32))
        sel = ((rank < min(N_SEL, nsb)) & (j_id < nsb)).astype(F32)
        if nbp > nr:
            sel = jnp.concatenate([sel, jnp.zeros((nbp - nr, sel.shape[1]), F32)], axis=0)
        sel_ref[...] = sel.T.astype(BF16)

    rows = sel_ref.shape[0]
    rc = min(512, rows)
    for r in range(0, rows, rc):
        o_ref[r:r + rc, :] = _dot(sel_ref[r:r + rc, :], exp_ref[...]).astype(BF16)


def _topk_mask(score2d, expand, nsb, tile):
    rows, nbp = score2d.shape
    keys = expand.shape[1]
    return pl.pallas_call(
        functools.partial(_topk_mask_kernel, nsb=nsb), grid=(keys // tile,),
        in_specs=[pl.BlockSpec((rows, nbp), lambda k: (0, 0)), pl.BlockSpec((nbp, tile), lambda k: (0, k))],
        out_specs=pl.BlockSpec((rows, tile), lambda k: (0, k)),
        out_shape=jax.ShapeDtypeStruct((rows, keys), BF16),
        scratch_shapes=[pltpu.VMEM((rows, nbp), BF16), pltpu.VMEM((_round_up(nsb, 8), rows), F32)],
        compiler_params=_cparams(("arbitrary",)), name="sample_topk_mask",
    )(score2d, expand)


def _rows_pad(a, rows):
    return jnp.concatenate([a, jnp.zeros((rows - a.shape[0], a.shape[1]), a.dtype)], axis=0)


def _smp_sel_kernel(pt_ref, *refs, pp):
    pages = refs[:pp]
    (mask_ref, maskt_ref, q_ref, new_ref, blast_ref, bnew_ref, o_ref, qp_ref, m_ref, l_ref, acc_ref) = refs[pp:]
    s = pl.program_id(1)
    t = q_ref.shape[0]
    rows = NSA_HEADS * t
    last = pl.num_programs(1) - 1

    @pl.when(s == 0)
    def _():
        qp_ref[...] = _group_queries(q_ref[...], KV_W)
        _softmax_init(m_ref, l_ref, acc_ref)

    qp = qp_ref[...]

    def expand_mask(mk, nk):
        m4 = jnp.broadcast_to(mk.astype(F32).reshape(NSA_KV_HEADS, 1, t, nk), (NSA_KV_HEADS, NSA_GROUP, t, nk))
        return m4.reshape(rows, nk) > 0.5

    nl = blast_ref.shape[1]
    nch = _n_chunks(pp)
    cp = pp // nch
    nk = cp * PAGE_SIZE

    def chunk(c, bias):
        keys = jnp.concatenate([pages[c * cp + j][0] for j in range(cp)], axis=1).astype(BF16)
        mask = expand_mask(mask_ref[0, :, c * nk:(c + 1) * nk], nk)
        raw = _dot(qp, keys)
        st = jnp.where(mask, raw if bias is None else raw + bias, NEG)
        _softmax_update(st, keys, m_ref, l_ref, acc_ref, v_feature_major=True)

    for c in range(nch - 1):
        chunk(c, None)

    @pl.when(s != last)
    def _():
        chunk(nch - 1, None)

    @pl.when(s == last)
    def _():
        far = [jnp.zeros((rows, nk - nl), F32)] if nk > nl else []
        chunk(nch - 1, jnp.concatenate(far + [blast_ref[...]], axis=1))
        kn = _rows_pad(new_ref[0], LANE).astype(BF16)
        j_id = lax.broadcasted_iota(jnp.int32, (rows, LANE), 1)
        t_id = lax.broadcasted_iota(jnp.int32, (rows, LANE), 0) % t
        mt = expand_mask(maskt_ref[0][:, :LANE], LANE) & (j_id <= t_id) & (j_id < t)
        st = jnp.where(mt, _dot_nt(qp, kn) + bnew_ref[...], NEG)
        _softmax_update(st, kn, m_ref, l_ref, acc_ref)
        o = acc_ref[...] / l_ref[...]
        half = KV_W // 2
        hr = NSA_GROUP * t
        o_ref[0] = jnp.concatenate([o[g * hr:(g + 1) * hr, half + g * HEAD_DIM:half + (g + 1) * HEAD_DIM]
                                    for g in range(NSA_KV_HEADS)], axis=0)


def _smp_sel(page_table, cache_slc, mask3, qn, kvs_new, blast, bnew, pp):
    bd, n_pages = page_table.shape
    t = kvs_new.shape[1]
    rows = NSA_HEADS * t
    nk = pp * PAGE_SIZE
    nsteps = n_pages // pp
    full = lambda a: pl.BlockSpec(a.shape, lambda b, s, pt, nd=a.ndim: (0,) * nd)
    gs = pltpu.PrefetchScalarGridSpec(
        num_scalar_prefetch=1, grid=(bd, nsteps),
        in_specs=_page_specs(pp, KV_W, PAGE_SIZE) + [
            pl.BlockSpec((1, NSA_KV_HEADS * t, nk), lambda b, s, pt: (b, 0, s)),
            pl.BlockSpec((1, NSA_KV_HEADS * t, nk), lambda b, s, pt: (b, 0, nsteps)),
            pl.BlockSpec((t, qn.shape[1]), lambda b, s, pt: (b, 0)),
            pl.BlockSpec((1, t, KV_W), lambda b, s, pt: (b, 0, 0)),
            full(blast), full(bnew)],
        out_specs=pl.BlockSpec((1, rows, HEAD_DIM), lambda b, s, pt: (b, 0, 0)),
        scratch_shapes=[pltpu.VMEM((rows, KV_W), BF16), pltpu.VMEM((rows, 1), F32), pltpu.VMEM((rows, 1), F32),
                        pltpu.VMEM((rows, KV_W), F32)])
    return pl.pallas_call(
        functools.partial(_smp_sel_kernel, pp=pp), grid_spec=gs,
        out_shape=jax.ShapeDtypeStruct((bd, rows, HEAD_DIM), F32),
        compiler_params=_cparams(("parallel", "arbitrary")), name="sample_sel",
    )(page_table, *([cache_slc] * pp), mask3, mask3, qn, kvs_new, blast, bnew)


def _smp_mla_kernel(pt_ref, *refs, pp):
    cpages = refs[:pp]
    rpages = refs[pp:2 * pp]
    (qa_ref, qr_ref, cnew_ref, rnew_ref, wuv_ref, o_ref, qa_s, qr_s, m_ref, l_ref, acc_ref) = refs[2 * pp:]
    s = pl.program_id(1)
    t = qa_ref.shape[0]
    rows = MLA_HEADS * t

    @pl.when(s == 0)
    def _():
        qa_s[...] = jnp.concatenate([qa_ref[:, h * KV_LORA:(h + 1) * KV_LORA] for h in range(MLA_HEADS)], axis=0)
        qr_s[...] = jnp.concatenate([qr_ref[:, h * LANE:(h + 1) * LANE] for h in range(MLA_HEADS)], axis=0)
        _softmax_init(m_ref, l_ref, acc_ref)

    qa, qr = qa_s[...], qr_s[...]
    nch = _n_chunks(pp)
    cp = pp // nch
    for c in range(nch):
        js = range(c * cp, (c + 1) * cp)
        ck = jnp.concatenate([cpages[j][0] for j in js], axis=0).astype(BF16)
        kr = jnp.concatenate([rpages[j][0] for j in js], axis=1).astype(BF16)
        st = _dot_nt(qa, ck) + _dot(qr[:, :QK_ROPE], kr)
        _softmax_update(st, ck, m_ref, l_ref, acc_ref)

    @pl.when(s == pl.num_programs(1) - 1)
    def _():
        cn = _rows_pad(cnew_ref[0], LANE)
        rn = _rows_pad(rnew_ref[0], LANE)
        j_id = lax.broadcasted_iota(jnp.int32, (rows, LANE), 1)
        t_id = lax.broadcasted_iota(jnp.int32, (rows, LANE), 0) % t
        mt = (j_id <= t_id) & (j_id < t)
        sn = jnp.where(mt, _dot_nt(qa, cn) + _dot_nt(qr, rn), NEG)
        _softmax_update(sn, cn, m_ref, l_ref, acc_ref)
        o_lat = (acc_ref[...] / l_ref[...]).astype(BF16)
        o_ref[...] = jnp.concatenate([_dot(o_lat[h * t:(h + 1) * t], wuv_ref[h]) for h in range(MLA_HEADS)],
                                     axis=1).astype(BF16)


def _smp_mla(page_table, cache_ckv, cache_kr, qa, qr, ckvb_new, krp_new, wuv, pp):
    bd, n_pages = page_table.shape
    t = ckvb_new.shape[1]
    rows = MLA_HEADS * t
    gs = pltpu.PrefetchScalarGridSpec(
        num_scalar_prefetch=1, grid=(bd, n_pages // pp),
        in_specs=_page_specs(pp, PAGE_SIZE, KV_LORA) + _page_specs(pp, QK_ROPE, PAGE_SIZE) + [
            pl.BlockSpec((t, qa.shape[1]), lambda b, s, pt: (b, 0)),
            pl.BlockSpec((t, qr.shape[1]), lambda b, s, pt: (b, 0)),
            pl.BlockSpec((1, t, KV_LORA), lambda b, s, pt: (b, 0, 0)),
            pl.BlockSpec((1, t, LANE), lambda b, s, pt: (b, 0, 0)),
            pl.BlockSpec(wuv.shape, lambda b, s, pt: (0, 0, 0))],
        out_specs=pl.BlockSpec((t, MLA_HEADS * V_DIM), lambda b, s, pt: (b, 0)),
        scratch_shapes=[pltpu.VMEM((rows, KV_LORA), BF16), pltpu.VMEM((rows, LANE), BF16),
                        pltpu.VMEM((rows, 1), F32), pltpu.VMEM((rows, 1), F32), pltpu.VMEM((rows, KV_LORA), F32)])
    return pl.pallas_call(
        functools.partial(_smp_mla_kernel, pp=pp), grid_spec=gs,
        out_shape=jax.ShapeDtypeStruct((bd * t, MLA_HEADS * V_DIM), BF16),
        compiler_params=_cparams(("parallel", "arbitrary")), name="sample_mla",
    )(page_table, *([cache_ckv] * pp), *([cache_kr] * pp), qa, qr, ckvb_new, krp_new, wuv)


def _smp_win_kernel(state_ref, new_ref, q_ref, misc_ref, ocmp_ref, osel_ref, bias_ref, o_ref, win_ref):
    t = q_ref.shape[0]
    wl = state_ref.shape[1]
    rows = NSA_HEADS * t
    state = state_ref[0]
    new = new_ref[0]
    win_ref[0, :wl - t, :] = state[t:, :]
    win_ref[0, wl - t:, :] = new
    nkp = bias_ref.shape[1]
    ctx = jnp.concatenate([state, new, jnp.zeros((nkp - wl - t, KV_W), F32)], axis=0).astype(BF16)
    qp = _group_queries(q_ref[...], KV_W)
    j_id = lax.broadcasted_iota(jnp.int32, (rows, nkp), 1)
    t_id = lax.broadcasted_iota(jnp.int32, (rows, nkp), 0) % t
    dist = wl + t_id - j_id
    mask = (dist >= 0) & (dist < WINDOW) & (j_id < wl + t)
    s = jnp.where(mask, _dot_nt(qp, ctx) + bias_ref[...], NEG)
    e = jnp.exp(s - jnp.max(s, axis=-1, keepdims=True))
    p = jnp.where(mask, e / jnp.sum(e, axis=-1, keepdims=True), 0.0)
    o_win = _dot(p.astype(BF16), ctx)
    misc = misc_ref[...]
    ocmp = ocmp_ref[0]
    osel = osel_ref[0]
    half = KV_W // 2
    pieces = []
    for hh in range(NSA_HEADS):
        g = hh // NSA_GROUP
        rs = slice(hh * t, (hh + 1) * t)
        gc = misc[:, MISC_GATE_OFF + hh:MISC_GATE_OFF + hh + 1]
        gsel = misc[:, MISC_GATE_OFF + NSA_HEADS + hh:MISC_GATE_OFF + NSA_HEADS + hh + 1]
        gw = misc[:, MISC_GATE_OFF + 2 * NSA_HEADS + hh:MISC_GATE_OFF + 2 * NSA_HEADS + hh + 1]
        pieces.append(gc * ocmp[rs, HEAD_DIM:] + gsel * osel[rs, :]
                      + gw * o_win[rs, half + g * HEAD_DIM:half + (g + 1) * HEAD_DIM])
    o_ref[...] = jnp.concatenate(pieces, axis=1).astype(BF16)


def _smp_win(state, kvw_new, qn, misc, ocmp, osel, bias_w):
    bd, wl, _ = state.shape
    t = kvw_new.shape[1]
    rows = NSA_HEADS * t
    nq = NSA_HEADS * HEAD_DIM
    return pl.pallas_call(
        _smp_win_kernel, grid=(bd,),
        in_specs=[pl.BlockSpec((1, wl, KV_W), lambda b: (b, 0, 0)), pl.BlockSpec((1, t, KV_W), lambda b: (b, 0, 0)),
                  pl.BlockSpec((t, nq), lambda b: (b, 0)), pl.BlockSpec((t, LANE), lambda b: (b, 0)),
                  pl.BlockSpec((1, rows, LANE), lambda b: (b, 0, 0)), pl.BlockSpec((1, rows, HEAD_DIM), lambda b: (b, 0, 0)),
                  pl.BlockSpec(bias_w.shape, lambda b: (0, 0))],
        out_specs=[pl.BlockSpec((t, nq), lambda b: (b, 0)), pl.BlockSpec((1, wl, KV_W), lambda b: (b, 0, 0))],
        out_shape=[jax.ShapeDtypeStruct((bd * t, nq), BF16), jax.ShapeDtypeStruct((bd, wl, KV_W), F32)],
        compiler_params=_cparams(("parallel",)), name="sample_window_merge",
    )(state, kvw_new, qn, misc, ocmp, osel, bias_w)


def _prep_params(d, attn_norm, w_in, g_q_nsa, g_k_cmp, g_k_slc, g_k_win, phi_cmp, g_cq, w_uq, g_q_mla, g_ckv,
                 g_krope, w_uk, w_uv, w_proj_nsa, w_proj_mla, w_out, ffn_norm, w_gate, w_up, w_down):
    nq = NSA_HEADS * HEAD_DIM
    widths = (nq, KV_W, KV_W, KV_W, GATE_N_W, Q_LORA, KV_LORA, QK_ROPE, d, d)
    names = ("q", "kvc", "kvs", "kvw", "gn", "cq", "ckv", "kr", "ga", "gb")
    splits = np.cumsum(widths)[:-1]
    seg = dict(zip(names, jnp.split(w_in, splits, axis=1)))
    zoff = _z_layout(d)
    tn = 1024 if d >= 1024 else 256
    zw = _round_up(zoff["end"], tn)
    zc = lambda w: jnp.zeros((d, w), w_in.dtype)
    cols = [seg["q"], seg["cq"], seg["kr"], seg["gn"], zc(zoff["ga"] - zoff["misc"] - QK_ROPE - GATE_N_W),
            seg["ga"], seg["gb"], zc(zoff["kv"] - zoff["gb"] - d), seg["kvc"], seg["kvs"], seg["kvw"], seg["ckv"],
            zc(zw - zoff["end"])]
    prm = {"w_cat": jnp.concatenate(cols, axis=1).astype(BF16), "tn": tn}
    prm["g_attn"] = attn_norm.reshape(1, d)
    prm["gq"] = jnp.tile(g_q_nsa, NSA_HEADS).reshape(1, nq)
    prm["gks"] = jnp.tile(g_k_slc, NSA_KV_HEADS).reshape(1, LANE)
    prm["gkw"] = jnp.tile(g_k_win, NSA_KV_HEADS).reshape(1, LANE)
    prm["gkc"] = jnp.tile(g_k_cmp, NSA_KV_HEADS).reshape(1, LANE)
    prm["gcq"] = g_cq.reshape(1, Q_LORA)
    wq = w_uq.reshape(Q_LORA, MLA_HEADS, QK_NOPE + QK_ROPE)
    w_rope = jnp.pad(wq[:, :, QK_NOPE:], ((0, 0), (0, 0), (0, LANE - QK_ROPE)))
    prm["wuq"] = jnp.concatenate([wq[:, :, :QK_NOPE].reshape(Q_LORA, MLA_HEADS * QK_NOPE),
                                  w_rope.reshape(Q_LORA, MLA_HEADS * LANE)], axis=1).astype(BF16)
    prm["gmn"] = jnp.tile(g_q_mla[:QK_NOPE], MLA_HEADS).reshape(1, MLA_HEADS * QK_NOPE)
    prm["gmr"] = jnp.tile(jnp.pad(g_q_mla[QK_NOPE:], (0, LANE - QK_ROPE)), MLA_HEADS).reshape(1, MLA_HEADS * LANE)
    prm["gckv"] = g_ckv.reshape(1, KV_LORA)
    prm["gkr"] = jnp.pad(g_krope, (0, LANE - QK_ROPE)).reshape(1, LANE)
    prm["wuk"] = w_uk.astype(BF16)
    prm["wuv"] = w_uv.astype(BF16)
    e64 = _seg_indicator(nq, HEAD_DIM)
    e128 = _seg_indicator(MLA_HEADS * LANE, LANE)
    prm["e64"], prm["e64t"] = jnp.asarray(e64, BF16), jnp.asarray(e64.T, BF16)
    prm["e128"], prm["e128t"] = jnp.asarray(e128, BF16), jnp.asarray(e128.T, BF16)
    ph = jnp.repeat(phi_cmp, KV_W // 2, axis=0).T
    prm["wlo"], prm["whi"] = ph[:CMP_STRIDE], ph[CMP_STRIDE:]
    eye = jnp.eye(PAGE_SIZE // CMP_STRIDE, dtype=F32)
    prm["wt"] = jnp.concatenate([jnp.kron(eye, phi_cmp[e, part * CMP_STRIDE:(part + 1) * CMP_STRIDE][None, :])
                                 for e in range(2) for part in range(2)], axis=0)
    prm["wpn"], prm["wpm"], prm["wo"] = w_proj_nsa.astype(BF16), w_proj_mla.astype(BF16), w_out.astype(BF16)
    prm["gf"] = ffn_norm.reshape(1, d)
    prm["wg"], prm["wu"], prm["wd"] = w_gate.astype(BF16), w_up.astype(BF16), w_down.astype(BF16)
    return prm, zoff


def _rope_tables(pos):
    half = QK_ROPE // 2
    inv_freq = ROPE_THETA ** (-jnp.arange(half, dtype=jnp.float32) / half)
    ang = pos.astype(jnp.float32)[:, None] * inv_freq
    cos, sin = jnp.cos(ang), jnp.sin(ang)
    z = jnp.zeros((pos.shape[0], LANE - QK_ROPE), F32)
    zh = jnp.zeros_like(sin)
    return (jnp.concatenate([cos, cos, z], axis=1), jnp.concatenate([zh, sin, z], axis=1),
            jnp.concatenate([-sin, zh, z], axis=1))


def _layer(xp, xs, cache_cmp, cache_slc, cache_ckv, cache_kr, win_state, page_table, rel_bias_table, weights):
    b, s, d = xp.shape
    bd, t, _ = xs.shape
    n_pages = page_table.shape[1]
    past = n_pages * PAGE_SIZE
    n_pool = cache_cmp.shape[0]
    wl = win_state.shape[1]
    prm, zoff = _prep_params(d, *weights)
    pp = max(1, min(32, n_pages // 2))
    nsteps = n_pages // pp

    qb = Q_BLOCK
    ar = np.arange
    d_t01 = (ar(qb)[None, :, None] - ar(qb)[None, None, :]) + qb * ar(2)[:, None, None]
    bw = 32
    d_bwin = ar(qb)[:, None] - CMP_STRIDE * (ar(bw)[None, :] - (bw - qb // CMP_STRIDE)) - (CMP_BLOCK - 1)
    d_c31 = np.full((LANE,), MAX_DISTANCE)
    past_chunks = past // CMP_STRIDE
    nc_s = past_chunks + SEL_BLOCK // CMP_STRIDE
    nr_s = _round_up(nc_s, LANE)
    nsb = nc_s // (SEL_BLOCK // CMP_STRIDE)
    nbp = _round_up(nsb, LANE)
    d_cmp_s = past + ar(t)[:, None] - CMP_STRIDE * ar(nr_s)[None, :] - (CMP_BLOCK - 1)
    nl = min(2 * LANE, pp * PAGE_SIZE)
    d_last = nl + ar(t)[:, None] - ar(nl)[None, :]
    d_new = ar(t)[:, None] - ar(LANE)[None, :]
    nkp = _round_up(wl + t, LANE)
    d_win = wl + ar(t)[:, None] - ar(nkp)[None, :]
    t01, bwin, c31, bias_c, blast, bnew, bias_w = _bias_tables(
        rel_bias_table, [d_t01, d_bwin, d_c31, d_cmp_s, d_last, d_new, d_win])
    c31row = jnp.repeat(c31[:, 0], qb).reshape(1, NSA_HEADS * qb)
    t01t = t01.transpose(1, 3, 0, 2).reshape(2, qb, NSA_HEADS * qb) - c31row[None]
    bwint = bwin.transpose(2, 0, 1).reshape(bw, NSA_HEADS * qb)
    rows_s = NSA_HEADS * t
    c31r = jnp.repeat(c31[:, :1], t, axis=0)
    blast = blast.reshape(rows_s, nl) - c31r
    bnew = bnew.reshape(rows_s, LANE) - c31r
    bias_w = bias_w.reshape(rows_s, nkp)
    at = jnp.asarray(_score_matrix(s // CMP_STRIDE, s // SEL_BLOCK).T, BF16)
    a_s = np.zeros((nr_s, nbp), np.float32)
    a_s[:nc_s] = _score_matrix(nc_s, nbp)
    a_s = jnp.asarray(a_s, BF16)
    keys_pad = (nsteps + 1) * pp * PAGE_SIZE
    expand = np.zeros((nbp, keys_pad), np.float32)
    kk = ar(nsb * SEL_BLOCK)
    expand[kk // SEL_BLOCK, kk] = 1.0
    expand = jnp.asarray(expand, BF16)

    pos_p = jnp.arange(s)
    tabs_p = tuple(jnp.tile(a, (b, 1)) for a in _rope_tables(pos_p))
    zp = _proj(xp.reshape(b * s, d), prm["g_attn"], prm["w_cat"], prm["tn"])
    pp_ = _post(zp, zoff, tabs_p, prm, True)
    kvcg, kvcgt = _cmp_prompt(pp_["kvc"].reshape(b, s, KV_W), prm)
    o_nsa_p = _nsa_prompt(pp_["qnt"], pp_["misc"], kvcg, kvcgt, pp_["kvsg"], pp_["kvsgt"], pp_["kvwg"], pp_["kvwgt"],
                          t01t, bwint, c31row, at, b, s)
    o_mla_p = _mla_prompt(pp_["qa"], pp_["qr"], pp_["ckvb"], pp_["krp"], prm["wuv"], b, s)
    x1p, h2p = _outproj(xp.reshape(b * s, d), o_nsa_p.reshape(b * s, -1), o_mla_p.reshape(b * s, -1), zp, zoff, prm)
    yp = _ffn(h2p, x1p, prm).reshape(b, s, d)

    pos_s = past + jnp.arange(t)
    tabs_s = tuple(jnp.tile(a, (bd, 1)) for a in _rope_tables(pos_s))
    zs = _proj(xs.reshape(bd * t, d), prm["g_attn"], prm["w_cat"], prm["tn"])
    ps = _post(zs, zoff, tabs_s, prm, False)
    cc = _feature_major_pages(cache_cmp)
    cs = _feature_major_pages(cache_slc)
    cache_kr = _feature_major_pages(cache_kr)
    ocmp_s, score = _smp_cmp(page_table, cc, ps["kvc"].reshape(bd, t, KV_W), ps["qn"], bias_c, a_s, prm, pp)
    mask = _topk_mask(score.reshape(bd * NSA_KV_HEADS * t, nbp), expand, nsb, pp * PAGE_SIZE)
    osel_s = _smp_sel(page_table, cs, mask.reshape(bd, NSA_KV_HEADS * t, keys_pad), ps["qn"],
                      ps["kvs"].reshape(bd, t, KV_W), blast, bnew, pp)
    o_mla_s = _smp_mla(page_table, cache_ckv, cache_kr, ps["qa"], ps["qr"], ps["ckvb"].reshape(bd, t, KV_LORA),
                       ps["krp"].reshape(bd, t, LANE), prm["wuv"], pp)
    o_nsa_s, win_s = _smp_win(win_state.reshape(bd, wl, KV_W), ps["kvw"].reshape(bd, t, KV_W), ps["qn"], ps["misc"],
                              ocmp_s, osel_s, bias_w)
    x1s, h2s = _outproj(xs.reshape(bd * t, d), o_nsa_s, o_mla_s, zs, zoff, prm)
    ys = _ffn(h2s, x1s, prm).reshape(bd, t, d)

    kvh = (2, NSA_KV_HEADS, HEAD_DIM)
    wp = min(WINDOW, s)
    return (yp, ys,
            pp_["kvc"].reshape((b, s) + kvh), ps["kvc"].reshape((bd, t) + kvh),
            pp_["kvs"].reshape((b, s) + kvh), ps["kvs"].reshape((bd, t) + kvh),
            pp_["ckv"].reshape(b, s, KV_LORA), ps["ckv"].reshape(bd, t, KV_LORA),
            pp_["misc"][:, :QK_ROPE].reshape(b, s, QK_ROPE), ps["misc"][:, :QK_ROPE].reshape(bd, t, QK_ROPE),
            pp_["kvw"].reshape((b, s) + kvh)[:, s - wp:], win_s.reshape((bd, wl) + kvh))


def kernel(x_prompt, x_sample, cache_cmp_kv, cache_slc_kv, cache_mla_ckv, cache_mla_krope, state_win_kv, page_table, rel_bias_table, attn_norm, w_in, g_q_nsa, g_k_cmp, g_k_slc, g_k_win, phi_cmp, g_cq, w_uq, g_q_mla, g_ckv, g_krope, w_uk, w_uv, w_proj_nsa, w_proj_mla, w_out, ffn_norm, w_gate, w_up, w_down):
    depth = w_in.shape[0]
    xp, xs = x_prompt, x_sample
    per_layer = []
    for l in range(depth):
        weights = (attn_norm[l], w_in[l], g_q_nsa[l], g_k_cmp[l], g_k_slc[l], g_k_win[l], phi_cmp[l], g_cq[l],
                   w_uq[l], g_q_mla[l], g_ckv[l], g_krope[l], w_uk[l], w_uv[l], w_proj_nsa[l], w_proj_mla[l],
                   w_out[l], ffn_norm[l], w_gate[l], w_up[l], w_down[l])
        xp, xs, *st = _layer(xp, xs, cache_cmp_kv[l], cache_slc_kv[l], cache_mla_ckv[l], cache_mla_krope[l],
                             state_win_kv[l], page_table, rel_bias_table, weights)
        per_layer.append(st)
    stacked = [jnp.stack(v) for v in zip(*per_layer)]
    return (xp, xs, *stacked)
```

```python
import functools
import math

import numpy as np
import jax
import jax.numpy as jnp
from jax import lax
from jax.experimental import pallas as pl
from jax.experimental.pallas import tpu as pltpu

F32 = jnp.float32
BF16 = jnp.bfloat16

NSA_HEADS = 16
NSA_KV_HEADS = 2
NSA_GROUP = NSA_HEADS // NSA_KV_HEADS
HEAD_DIM = 64
CMP_BLOCK = 32
CMP_STRIDE = 16
SEL_BLOCK = 64
N_SEL = 16
WINDOW = 512
MLA_HEADS = 16
QK_NOPE = 64
QK_ROPE = 32
V_DIM = 64
KV_LORA = 256
Q_LORA = 512
ROPE_THETA = 10000.0
N_BUCKETS = 32
MAX_DISTANCE = 128
PAGE_SIZE = 128
Q_BLOCK = 128
EPS = 1e-6
NEG = -1e30
NEG_CLAMP = -1e29
PAD_SCORE = -3e38
FORCE_BONUS = 1e6

LANE = 128
KV_W = 2 * NSA_KV_HEADS * HEAD_DIM
GATE_N_W = 3 * NSA_HEADS
MISC_GATE_OFF = QK_ROPE
VMEM_LIMIT = 56 * 1024 * 1024


def _round_up(a, b):
    return -(-a // b) * b


def _cparams(sem):
    return pltpu.CompilerParams(dimension_semantics=sem, vmem_limit_bytes=VMEM_LIMIT)


def _dot(a, b):
    return jnp.dot(a, b, preferred_element_type=F32)


def _dot_nt(a, b):
    return lax.dot_general(a, b, (((1,), (1,)), ((), ())), preferred_element_type=F32)


def _split3(x):
    hi = x.astype(BF16)
    r = x - hi.astype(F32)
    mid = r.astype(BF16)
    lo = (r - mid.astype(F32)).astype(BF16)
    return hi, mid, lo


def _dot_sel(x, m):
    hi, mid, lo = _split3(x)
    return _dot(hi, m) + _dot(mid, m) + _dot(lo, m)


def _dot_sel_nt(m, x):
    hi, mid, lo = _split3(x)
    return _dot_nt(m, hi) + _dot_nt(m, mid) + _dot_nt(m, lo)


def _seg_rinv(x, e, et, width):
    ss = _dot_sel(x * x, e)
    r = lax.rsqrt(ss / width + EPS)
    return _dot_sel(r, et)


def _t5_bucket_np(dist):
    n = np.maximum(dist, 0)
    exact = N_BUCKETS // 2
    log_ratio = np.log(np.maximum(n, 1).astype(np.float32) / exact) / math.log(MAX_DISTANCE / exact)
    large = np.minimum(exact + (log_ratio * (N_BUCKETS - exact)).astype(np.int32), N_BUCKETS - 1)
    return np.where(n < exact, n, large)


def _seg_indicator(width, seg):
    e = np.zeros((width, LANE), np.float32)
    e[np.arange(width), np.arange(width) // seg] = 1.0
    return e


def _score_matrix(nc, nb_pad):
    ratio = SEL_BLOCK // CMP_STRIDE
    a = np.zeros((nc, nb_pad), np.float32)
    c = np.arange(nc)
    a[c, c // ratio] = 1.0
    cc = c[(c % ratio == ratio - 1) & (c // ratio + 1 < nc // ratio)]
    a[cc, cc // ratio + 1] = 1.0
    return a


def _bias_expand_kernel(tt_ref, oh_ref, o_ref):
    hi, mid, lo = _split3(tt_ref[...])
    oh = oh_ref[...]
    o_ref[...] = _dot(hi, oh) + _dot(mid, oh) + _dot(lo, oh)


def _bias_tables(rel_bias_table, dist_list):
    sizes = [_round_up(d.size, LANE) for d in dist_list]
    tile = 2048
    total = _round_up(sum(sizes), tile)
    onehot = np.zeros((N_BUCKETS, total), np.float32)
    off = 0
    for d, sz in zip(dist_list, sizes):
        b = _t5_bucket_np(d.reshape(-1))
        onehot[b, off + np.arange(d.size)] = 1.0
        off += sz
    out = pl.pallas_call(
        _bias_expand_kernel,
        grid=(total // tile,),
        in_specs=[pl.BlockSpec((NSA_HEADS, N_BUCKETS), lambda i: (0, 0)),
                  pl.BlockSpec((N_BUCKETS, tile), lambda i: (0, i))],
        out_specs=pl.BlockSpec((NSA_HEADS, tile), lambda i: (0, i)),
        out_shape=jax.ShapeDtypeStruct((NSA_HEADS, total), F32),
        compiler_params=_cparams(("arbitrary",)),
        name="bias_expand",
    )(rel_bias_table.T, jnp.asarray(onehot, BF16))
    res, off = [], 0
    for d, sz in zip(dist_list, sizes):
        res.append(out[:, off:off + d.size].reshape((NSA_HEADS,) + d.shape))
        off += sz
    return res


def _z_layout(d_model):
    off = {}
    off["q"] = 0
    off["cq"] = NSA_HEADS * HEAD_DIM
    off["misc"] = off["cq"] + Q_LORA
    pos = _round_up(off["misc"] + LANE, d_model)
    off["ga"] = pos
    off["gb"] = pos + d_model
    pos = _round_up(off["gb"] + d_model, 3 * KV_W)
    off["kv"] = pos
    off["ckv"] = off["kv"] + 3 * KV_W
    off["end"] = off["ckv"] + KV_LORA
    return off


def _proj_kernel(x_ref, g_ref, w_ref, z_ref, h_ref):
    @pl.when(pl.program_id(1) == 0)
    def _():
        x = x_ref[...]
        ms = jnp.mean(x * x, axis=-1, keepdims=True)
        h_ref[...] = (x * lax.rsqrt(ms + EPS) * g_ref[...]).astype(BF16)

    z_ref[...] = _dot(h_ref[...], w_ref[...])


def _proj(x2d, g_norm, w_cat, tn):
    n, d = x2d.shape
    zw = w_cat.shape[1]
    tm = min(512, n)
    return pl.pallas_call(
        _proj_kernel,
        grid=(n // tm, zw // tn),
        in_specs=[pl.BlockSpec((tm, d), lambda i, j: (i, 0)),
                  pl.BlockSpec((1, d), lambda i, j: (0, 0)),
                  pl.BlockSpec((d, tn), lambda i, j: (0, j))],
        out_specs=pl.BlockSpec((tm, tn), lambda i, j: (i, j)),
        out_shape=jax.ShapeDtypeStruct((n, zw), F32),
        scratch_shapes=[pltpu.VMEM((tm, d), BF16)],
        compiler_params=_cparams(("parallel", "arbitrary")),
        name="in_proj",
    )(x2d, g_norm, w_cat)


def _post_kernel(zq_ref, zcq_ref, zkv_ref, zckv_ref, zmisc_ref, cos_ref, sina_ref, sinb_ref,
                 gq_ref, gks_ref, gkw_ref, gcq_ref, wuq_ref, gmn_ref, gmr_ref, gckv_ref, gkr_ref, wuk_ref,
                 e64_ref, e64t_ref, e128_ref, e128t_ref,
                 qn_ref, kvc_ref, kvs_ref, kvw_ref, kvsg_ref, kvwg_ref,
                 ckv_ref, ckvb_ref, misc_ref, krp_ref, qa_ref, qr_ref, *feature_major_refs):
    e64, e64t = e64_ref[...], e64t_ref[...]
    e128, e128t = e128_ref[...], e128t_ref[...]
    nq = NSA_HEADS * HEAD_DIM

    zq = zq_ref[...]
    qn = zq * _seg_rinv(zq, e64, e64t, HEAD_DIM) * gq_ref[...] * (HEAD_DIM ** -0.5)
    qn_ref[...] = qn.astype(BF16)
    if feature_major_refs:
        feature_major_refs[0][...] = qn.T.astype(BF16)

    zkv = zkv_ref[...]
    kvc_ref[...] = zkv[:, :KV_W]
    half = KV_W // 2
    for idx, (g_ref, out_ref, outg_ref) in enumerate(((gks_ref, kvs_ref, kvsg_ref), (gkw_ref, kvw_ref, kvwg_ref))):
        kv = zkv[:, (idx + 1) * KV_W:(idx + 2) * KV_W]
        k, v = kv[:, :half], kv[:, half:]
        kn = k * _seg_rinv(k, e64[:half], e64t[:, :half], HEAD_DIM) * g_ref[...]
        out_ref[...] = jnp.concatenate([kn, v], axis=1)
        for g in range(NSA_KV_HEADS):
            sl = slice(g * HEAD_DIM, (g + 1) * HEAD_DIM)
            kvg = jnp.concatenate([kn[:, sl], v[:, sl]], axis=1)
            outg_ref[g] = kvg.astype(BF16)
            if feature_major_refs:
                for sub in range(kvg.shape[0] // Q_BLOCK):
                    feature_major_refs[1 + idx][g, sub] = kvg[sub * Q_BLOCK:(sub + 1) * Q_BLOCK].T.astype(BF16)

    zc = zckv_ref[...]
    ckv = zc * lax.rsqrt(jnp.mean(zc * zc, axis=-1, keepdims=True) + EPS) * gckv_ref[...]
    ckv_ref[...] = ckv
    ckvb_ref[...] = ckv.astype(BF16)

    cos, sina, sinb = cos_ref[...], sina_ref[...], sinb_ref[...]
    zm = zmisc_ref[...]
    lane = lax.broadcasted_iota(jnp.int32, zm.shape, 1)
    is_kr = lane < QK_ROPE
    ms = jnp.sum(jnp.where(is_kr, zm * zm, 0.0), axis=-1, keepdims=True) / QK_ROPE
    krn = zm * lax.rsqrt(ms + EPS) * gkr_ref[...]
    half_r = QK_ROPE // 2
    kr = krn * cos + pltpu.roll(krn, half_r, 1) * sina + pltpu.roll(krn, LANE - half_r, 1) * sinb
    kr = jnp.where(is_kr, kr, 0.0)
    gates = jnp.where(lane < MISC_GATE_OFF + GATE_N_W, jax.nn.sigmoid(zm), 0.0)
    misc_ref[...] = jnp.where(is_kr, kr, gates)
    krp_ref[...] = kr.astype(BF16)

    zcq = zcq_ref[...]
    cqn = (zcq * lax.rsqrt(jnp.mean(zcq * zcq, axis=-1, keepdims=True) + EPS) * gcq_ref[...]).astype(BF16)
    qm = _dot(cqn, wuq_ref[...])
    nope, ropep = qm[:, :nq], qm[:, nq:]
    ss = _dot_sel(nope * nope, e64) + _dot_sel(ropep * ropep, e128)
    r = lax.rsqrt(ss / (QK_NOPE + QK_ROPE) + EPS)
    scale = (QK_NOPE + QK_ROPE) ** -0.5
    nope_n = nope * _dot_sel(r, e64t) * gmn_ref[...] * scale
    rope_n = ropep * _dot_sel(r, e128t) * gmr_ref[...] * scale
    wr = MLA_HEADS * LANE
    cos_t = jnp.concatenate([cos] * MLA_HEADS, axis=1)
    sina_t = jnp.concatenate([sina] * MLA_HEADS, axis=1)
    sinb_t = jnp.concatenate([sinb] * MLA_HEADS, axis=1)
    qr = rope_n * cos_t + pltpu.roll(rope_n, half_r, 1) * sina_t + pltpu.roll(rope_n, wr - half_r, 1) * sinb_t
    qr_ref[...] = qr.astype(BF16)
    nb = nope_n.astype(BF16)
    for h in range(MLA_HEADS):
        qa = _dot_nt(nb[:, h * QK_NOPE:(h + 1) * QK_NOPE], wuk_ref[h])
        qa_ref[:, h * KV_LORA:(h + 1) * KV_LORA] = qa.astype(BF16)


def _post(z, zoff, tabs, prm, feature_major):
    n = z.shape[0]
    tm = min(256, n)
    nq = NSA_HEADS * HEAD_DIM
    row = lambda w, c: pl.BlockSpec((tm, w), lambda i, c=c: (i, c))
    full = lambda a: pl.BlockSpec(a.shape, lambda i, nd=a.ndim: (0,) * nd)
    cos, sina, sinb = tabs
    consts = [prm["gq"], prm["gks"], prm["gkw"], prm["gcq"], prm["wuq"], prm["gmn"], prm["gmr"], prm["gckv"],
              prm["gkr"], prm["wuk"], prm["e64"], prm["e64t"], prm["e128"], prm["e128t"]]
    in_specs = [row(nq, zoff["q"] // nq), row(Q_LORA, zoff["cq"] // Q_LORA), row(3 * KV_W, zoff["kv"] // (3 * KV_W)),
                row(KV_LORA, zoff["ckv"] // KV_LORA), row(LANE, zoff["misc"] // LANE),
                row(LANE, 0), row(LANE, 0), row(LANE, 0)] + [full(a) for a in consts]
    sds = jax.ShapeDtypeStruct
    out_shape = [sds((n, nq), BF16), sds((n, KV_W), F32), sds((n, KV_W), F32), sds((n, KV_W), F32),
                 sds((NSA_KV_HEADS, n, LANE), BF16), sds((NSA_KV_HEADS, n, LANE), BF16),
                 sds((n, KV_LORA), F32), sds((n, KV_LORA), BF16), sds((n, LANE), F32), sds((n, LANE), BF16),
                 sds((n, MLA_HEADS * KV_LORA), BF16), sds((n, MLA_HEADS * LANE), BF16)]
    grp = pl.BlockSpec((NSA_KV_HEADS, tm, LANE), lambda i: (0, i, 0))
    out_specs = [row(nq, 0), row(KV_W, 0), row(KV_W, 0), row(KV_W, 0), grp, grp,
                 row(KV_LORA, 0), row(KV_LORA, 0), row(LANE, 0), row(LANE, 0),
                 row(MLA_HEADS * KV_LORA, 0), row(MLA_HEADS * LANE, 0)]
    names = ["qn", "kvc", "kvs", "kvw", "kvsg", "kvwg", "ckv", "ckvb", "misc", "krp", "qa", "qr"]
    if feature_major:
        tiles = (NSA_KV_HEADS, n // Q_BLOCK, LANE, Q_BLOCK)
        grpt = pl.BlockSpec((NSA_KV_HEADS, tm // Q_BLOCK, LANE, Q_BLOCK), lambda i: (0, i, 0, 0))
        out_shape += [sds((nq, n), BF16), sds(tiles, BF16), sds(tiles, BF16)]
        out_specs += [pl.BlockSpec((nq, tm), lambda i: (0, i)), grpt, grpt]
        names += ["qnt", "kvsgt", "kvwgt"]
    outs = pl.pallas_call(
        _post_kernel, grid=(n // tm,), in_specs=in_specs, out_specs=out_specs, out_shape=out_shape,
        compiler_params=_cparams(("parallel",)), name="post_proj",
    )(z, z, z, z, z, cos, sina, sinb, *consts)
    return dict(zip(names, outs))


def _chunk_partials(rows, wlo, whi):
    ch = rows.reshape(rows.shape[0] // CMP_STRIDE, CMP_STRIDE, rows.shape[1])
    return jnp.sum(ch * wlo[None], axis=1), jnp.sum(ch * whi[None], axis=1)


def _blocks_to_groups(lo, hi, gkc, e64, e64t):
    nr = lo.shape[0]
    rid = lax.broadcasted_iota(jnp.int32, hi.shape, 0)
    hi_next = jnp.where(rid < nr - 1, pltpu.roll(hi, nr - 1, 0), 0.0)
    blk = lo + hi_next
    half = KV_W // 2
    k, v = blk[:, :half], blk[:, half:]
    kn = k * _seg_rinv(k, e64[:half], e64t[:, :half], HEAD_DIM) * gkc
    out = []
    for g in range(NSA_KV_HEADS):
        sl = slice(g * HEAD_DIM, (g + 1) * HEAD_DIM)
        out.append(jnp.concatenate([kn[:, sl], v[:, sl]], axis=1))
    return out


def _cmp_prompt_kernel(kvc_ref, wlo_ref, whi_ref, gkc_ref, e64_ref, e64t_ref, o_ref, ot_ref):
    lo, hi = _chunk_partials(kvc_ref[0], wlo_ref[...], whi_ref[...])
    groups = _blocks_to_groups(lo, hi, gkc_ref[...], e64_ref[...], e64t_ref[...])
    for g in range(NSA_KV_HEADS):
        o_ref[0, g] = groups[g].astype(BF16)
        ot_ref[0, g] = groups[g].T.astype(BF16)


def _cmp_prompt(kvc, prm):
    b, s, _ = kvc.shape
    nc = s // CMP_STRIDE
    full = lambda a: pl.BlockSpec(a.shape, lambda i, nd=a.ndim: (0,) * nd)
    consts = [prm["wlo"], prm["whi"], prm["gkc"], prm["e64"], prm["e64t"]]
    return pl.pallas_call(
        _cmp_prompt_kernel, grid=(b,),
        in_specs=[pl.BlockSpec((1, s, KV_W), lambda i: (i, 0, 0))] + [full(a) for a in consts],
        out_specs=[pl.BlockSpec((1, NSA_KV_HEADS, nc, LANE), lambda i: (i, 0, 0, 0)),
                   pl.BlockSpec((1, NSA_KV_HEADS, LANE, nc), lambda i: (i, 0, 0, 0))],
        out_shape=[jax.ShapeDtypeStruct((b, NSA_KV_HEADS, nc, LANE), BF16),
                   jax.ShapeDtypeStruct((b, NSA_KV_HEADS, LANE, nc), BF16)],
        compiler_params=_cparams(("parallel",)), name="cmp_blocks_prompt",
    )(kvc, *consts)


def _softmax_update(s, v, m_ref, l_ref, acc_ref, v_feature_major=False):
    m_old = m_ref[...]
    m_new = jnp.maximum(m_old, jnp.max(s, axis=-1, keepdims=True))
    p = jnp.exp(s - jnp.maximum(m_new, NEG_CLAMP))
    alpha = jnp.exp(m_old - m_new)
    l_ref[...] = alpha * l_ref[...] + jnp.sum(p, axis=-1, keepdims=True)
    pv = _dot_nt(p.astype(BF16), v) if v_feature_major else _dot(p.astype(BF16), v)
    acc_ref[...] = alpha * acc_ref[...] + pv
    m_ref[...] = m_new


def _softmax_init(m_ref, l_ref, acc_ref):
    m_ref[...] = jnp.full(m_ref.shape, NEG, F32)
    l_ref[...] = jnp.zeros(l_ref.shape, F32)
    acc_ref[...] = jnp.zeros(acc_ref.shape, F32)


def _col_softmax_update(s, kvt, m_ref, l_ref, acc_ref):
    m_old = m_ref[...]
    m_new = jnp.maximum(m_old, jnp.max(s, axis=0, keepdims=True))
    p = jnp.exp(s - jnp.maximum(m_new, NEG_CLAMP))
    alpha = jnp.exp(m_old - m_new)
    l_ref[...] = alpha * l_ref[...] + jnp.sum(p, axis=0, keepdims=True)
    acc_ref[...] = alpha * acc_ref[...] + _dot(kvt, p.astype(BF16))
    m_ref[...] = m_new


def _nsa_prompt_kernel(qt_ref, misc_ref, kvc_ref, kvct_ref, kvs_ref, kvst_ref, kvw_ref, kvwt_ref,
                       t01_ref, bwin_ref, c31_ref, at_ref, o_ref,
                       qt_s, sel_s, ocmp_s, osel_s, m_ref, l_ref, acc_ref, sc_ref):
    i = pl.program_id(1)
    qb = Q_BLOCK
    nc = kvc_ref.shape[2]
    nb = at_ref.shape[0]
    bw = bwin_ref.shape[0]
    gate_t = misc_ref[0].T
    tk2 = lax.broadcasted_iota(jnp.int32, (qb, qb), 0)
    tq2 = lax.broadcasted_iota(jnp.int32, (qb, qb), 1)
    causal = tk2 <= tq2
    anti = tk2 > tq2
    tile8 = lambda m: jnp.concatenate([m] * NSA_GROUP, axis=1)
    lanes = lambda g: slice(g * NSA_GROUP * qb, (g + 1) * NSA_GROUP * qb)
    for g in range(NSA_KV_HEADS):
        h0 = g * NSA_GROUP
        ls = lanes(g)
        qt = jnp.concatenate([qt_ref[(h0 + h) * HEAD_DIM:(h0 + h + 1) * HEAD_DIM, :] for h in range(NSA_GROUP)], axis=1)
        qt = jnp.concatenate([qt, jnp.zeros_like(qt)], axis=0)
        qt_s[g] = qt
        c31 = c31_ref[:, ls]

        kvc = kvc_ref[0, g]
        s = _dot(kvc, qt)
        shift_c = lax.broadcasted_iota(jnp.int32, (nc, bw), 0)
        shift_j = lax.broadcasted_iota(jnp.int32, (nc, bw), 1)
        first = (qb // CMP_STRIDE) * i - (bw - qb // CMP_STRIDE)
        shift = (shift_c - shift_j == first).astype(BF16)
        hi, mid, lo = _split3(bwin_ref[:, ls])
        placed = _dot(shift, hi) + _dot(shift, mid) + _dot(shift, lo)
        bias = jnp.where(lax.broadcasted_iota(jnp.int32, (nc, 1), 0) < first, c31, placed)
        c_id = lax.broadcasted_iota(jnp.int32, (nc, qb), 0)
        t_id = lax.broadcasted_iota(jnp.int32, (nc, qb), 1)
        mask_c = tile8((qb * i + t_id - CMP_STRIDE * c_id - (CMP_BLOCK - 1)) >= 0)
        s = jnp.where(mask_c, s + bias, NEG)
        e = jnp.exp(s - jnp.max(s, axis=0, keepdims=True))
        p = jnp.where(mask_c, e * (1.0 / jnp.sum(e, axis=0, keepdims=True)), 0.0)
        ocmp_s[g] = _dot(kvct_ref[0, g], p.astype(BF16))[HEAD_DIM:]

        pg = p[:, :qb]
        for h in range(1, NSA_GROUP):
            pg = pg + p[:, h * qb:(h + 1) * qb]
        hi, mid, lo = _split3(pg)
        at = at_ref[...]
        sc = _dot(at, hi) + _dot(at, mid) + _dot(at, lo)
        j_id = lax.broadcasted_iota(jnp.int32, (nb, qb), 0)
        qpos = qb * i + lax.broadcasted_iota(jnp.int32, (nb, qb), 1)
        cur = qpos // SEL_BLOCK
        forced = (j_id == 0) | (j_id == cur) | (j_id == cur - 1)
        valid = j_id * SEL_BLOCK <= qpos
        sc = jnp.where(valid, jnp.where(forced, sc + FORCE_BONUS, sc), NEG)
        sc_ref[...] = sc

        def rank_body(r, rank):
            sr = sc_ref[pl.ds(r, 1), :]
            ahead = (sr > sc) | ((sr == sc) & (r < j_id))
            return rank + ahead.astype(jnp.int32)

        rank = lax.fori_loop(0, nb, rank_body, jnp.zeros((nb, qb), jnp.int32))
        sel_s[g] = (rank < min(N_SEL, nb)).astype(F32).astype(BF16)

    def sel_mask(g, kt, width=1):
        nk = width * qb
        ek = lax.broadcasted_iota(jnp.int32, (nk, nb), 0)
        ej = lax.broadcasted_iota(jnp.int32, (nk, nb), 1)
        expand = (ej == (qb // SEL_BLOCK) * kt + ek // SEL_BLOCK).astype(BF16)
        return _dot(expand, sel_s[g]) > 0.5

    def tile(g, kv_ref, kvt_ref, kt, near, mask, width=1):
        kv = kv_ref[g, 0, pl.ds(pl.multiple_of(kt * qb, qb), width * qb), :]
        kvt = jnp.concatenate([kvt_ref[g, 0, kt + w] for w in range(width)], axis=1)
        st = _dot(kv, qt_s[g])
        if near is not None:
            st = st + t01_ref[near, :, lanes(g)]
        if mask is not None:
            st = jnp.where(tile8(mask), st, NEG)
        _col_softmax_update(st, kvt, m_ref.at[g], l_ref.at[g], acc_ref.at[g])

    groups = range(NSA_KV_HEADS)

    _softmax_init(m_ref, l_ref, acc_ref)
    n_far = jnp.maximum(i - 1, 0)

    def far_body(pair, carry):
        for g in groups:
            tile(g, kvs_ref, kvst_ref, 2 * pair, None, sel_mask(g, 2 * pair, 2), 2)
        return carry

    lax.fori_loop(0, n_far // 2, far_body, 0)

    @pl.when(n_far % 2 == 1)
    def _():
        for g in groups:
            tile(g, kvs_ref, kvst_ref, n_far - 1, None, sel_mask(g, n_far - 1))

    @pl.when(i >= 1)
    def _():
        for g in groups:
            tile(g, kvs_ref, kvst_ref, i - 1, 1, sel_mask(g, i - 1))

    for g in groups:
        tile(g, kvs_ref, kvst_ref, i, 0, sel_mask(g, i) & causal)
    for g in groups:
        osel_s[g] = (acc_ref[g] * (1.0 / l_ref[g]))[HEAD_DIM:]

    _softmax_init(m_ref, l_ref, acc_ref)
    nwin = WINDOW // qb
    for diff in range(nwin, -1, -1):
        near = diff if diff <= 1 else None
        mask = causal if diff == 0 else (anti if diff == nwin else None)
        if diff == 0:
            for g in groups:
                tile(g, kvw_ref, kvwt_ref, i, near, mask)
        else:
            @pl.when(i >= diff)
            def _(diff=diff, near=near, mask=mask):
                for g in groups:
                    tile(g, kvw_ref, kvwt_ref, i - diff, near, mask)

    pieces = []
    for g in groups:
        o_win = (acc_ref[g] * (1.0 / l_ref[g]))[HEAD_DIM:]
        for h in range(NSA_GROUP):
            cs = slice(h * qb, (h + 1) * qb)
            r0 = MISC_GATE_OFF + g * NSA_GROUP + h
            gc = gate_t[r0:r0 + 1, :]
            gs = gate_t[r0 + NSA_HEADS:r0 + NSA_HEADS + 1, :]
            gw = gate_t[r0 + 2 * NSA_HEADS:r0 + 2 * NSA_HEADS + 1, :]
            oh = gc * ocmp_s[g, :, cs] + gs * osel_s[g, :, cs] + gw * o_win[:, cs]
            pieces.append(oh.T)
    o_ref[0] = jnp.concatenate(pieces, axis=1).astype(BF16)


def _nsa_prompt(qnt, misc, kvcg, kvcgt, kvsg, kvsgt, kvwg, kvwgt, t01t, bwint, c31row, at, b, s):
    nqb = s // Q_BLOCK
    nq = NSA_HEADS * HEAD_DIM
    nc = s // CMP_STRIDE
    nb = s // SEL_BLOCK
    cols = NSA_GROUP * Q_BLOCK
    full = lambda a: pl.BlockSpec(a.shape, lambda bi, i, nd=a.ndim: (0,) * nd)
    rowm = pl.BlockSpec((NSA_KV_HEADS, 1, s, LANE), lambda bi, i: (0, bi, 0, 0))
    featm = pl.BlockSpec((NSA_KV_HEADS, 1, nqb, LANE, Q_BLOCK), lambda bi, i: (0, bi, 0, 0, 0))
    return pl.pallas_call(
        _nsa_prompt_kernel, grid=(b, nqb),
        in_specs=[pl.BlockSpec((nq, Q_BLOCK), lambda bi, i: (0, bi * nqb + i)),
                  pl.BlockSpec((1, Q_BLOCK, LANE), lambda bi, i: (bi, i, 0)),
                  pl.BlockSpec((1, NSA_KV_HEADS, nc, LANE), lambda bi, i: (bi, 0, 0, 0)),
                  pl.BlockSpec((1, NSA_KV_HEADS, LANE, nc), lambda bi, i: (bi, 0, 0, 0)),
                  rowm, featm, rowm, featm, full(t01t), full(bwint), full(c31row), full(at)],
        out_specs=pl.BlockSpec((1, Q_BLOCK, nq), lambda bi, i: (bi, i, 0)),
        out_shape=jax.ShapeDtypeStruct((b, s, nq), BF16),
        scratch_shapes=[pltpu.VMEM((NSA_KV_HEADS, LANE, cols), BF16), pltpu.VMEM((NSA_KV_HEADS, nb, Q_BLOCK), BF16),
                        pltpu.VMEM((NSA_KV_HEADS, HEAD_DIM, cols), F32), pltpu.VMEM((NSA_KV_HEADS, HEAD_DIM, cols), F32),
                        pltpu.VMEM((NSA_KV_HEADS, 1, cols), F32), pltpu.VMEM((NSA_KV_HEADS, 1, cols), F32),
                        pltpu.VMEM((NSA_KV_HEADS, LANE, cols), F32), pltpu.VMEM((nb, Q_BLOCK), F32)],
        compiler_params=_cparams(("parallel", "arbitrary")), name="nsa_prompt",
    )(qnt, misc.reshape(b, s, LANE), kvcg, kvcgt,
      kvsg.reshape(NSA_KV_HEADS, b, s, LANE), kvsgt.reshape(NSA_KV_HEADS, b, nqb, LANE, Q_BLOCK),
      kvwg.reshape(NSA_KV_HEADS, b, s, LANE), kvwgt.reshape(NSA_KV_HEADS, b, nqb, LANE, Q_BLOCK),
      t01t, bwint, c31row, at)


def _mla_prompt_kernel(qa_ref, qr_ref, ckv_ref, krp_ref, wuv_ref, o_ref, qa_s, qr_s, m_ref, l_ref, acc_ref, *, tk):
    i = pl.program_id(1)
    qb = Q_BLOCK
    nsplit = qa_s.shape[0]
    hps = MLA_HEADS // nsplit
    rows = hps * qb
    for sp in range(nsplit):
        hs = range(sp * hps, (sp + 1) * hps)
        qa_s[sp] = jnp.concatenate([qa_ref[0, :, h * KV_LORA:(h + 1) * KV_LORA] for h in hs], axis=0)
        qr_s[sp] = jnp.concatenate([qr_ref[0, :, h * LANE:(h + 1) * LANE] for h in hs], axis=0)
    _softmax_init(m_ref, l_ref, acc_ref)

    def tile(kt, masked):
        ks = pl.ds(pl.multiple_of(kt * tk, tk), tk)
        ck = ckv_ref[0, ks, :]
        kr = krp_ref[0, ks, :]
        if masked:
            tq = qb * i + lax.broadcasted_iota(jnp.int32, (qb, tk), 0)
            kp = kt * tk + lax.broadcasted_iota(jnp.int32, (qb, tk), 1)
            m3 = jnp.broadcast_to((kp <= tq)[None], (hps, qb, tk))
        for sp in range(nsplit):
            s = _dot_nt(qa_s[sp], ck) + _dot_nt(qr_s[sp], kr)
            state = (m_ref.at[sp], l_ref.at[sp], acc_ref.at[sp])
            if masked:
                s = jnp.where(m3, s.reshape(hps, qb, tk), NEG).reshape(rows, tk)
            _softmax_update(s, ck, *state)

    nfull = (i * qb) // tk

    def body(kt, carry):
        tile(kt, False)
        return carry

    lax.fori_loop(0, nfull, body, 0)
    tile(nfull, True)
    pieces = []
    for sp in range(nsplit):
        o_lat = (acc_ref[sp] / l_ref[sp]).astype(BF16)
        pieces += [_dot(o_lat[h * qb:(h + 1) * qb], wuv_ref[sp * hps + h]) for h in range(hps)]
    o_ref[0] = jnp.concatenate(pieces, axis=1).astype(BF16)


def _mla_prompt(qa, qr, ckvb, krp, wuv, b, s):
    nqb = s // Q_BLOCK
    tk = min(1024, s)
    nsplit = 4
    rows = MLA_HEADS * Q_BLOCK // nsplit
    wa, wr = MLA_HEADS * KV_LORA, MLA_HEADS * LANE
    return pl.pallas_call(
        functools.partial(_mla_prompt_kernel, tk=tk), grid=(b, nqb),
        in_specs=[pl.BlockSpec((1, Q_BLOCK, wa), lambda bi, i: (bi, i, 0)),
                  pl.BlockSpec((1, Q_BLOCK, wr), lambda bi, i: (bi, i, 0)),
                  pl.BlockSpec((1, s, KV_LORA), lambda bi, i: (bi, 0, 0)),
                  pl.BlockSpec((1, s, LANE), lambda bi, i: (bi, 0, 0)),
                  pl.BlockSpec(wuv.shape, lambda bi, i: (0, 0, 0))],
        out_specs=pl.BlockSpec((1, Q_BLOCK, MLA_HEADS * V_DIM), lambda bi, i: (bi, i, 0)),
        out_shape=jax.ShapeDtypeStruct((b, s, MLA_HEADS * V_DIM), BF16),
        scratch_shapes=[pltpu.VMEM((nsplit, rows, KV_LORA), BF16), pltpu.VMEM((nsplit, rows, LANE), BF16),
                        pltpu.VMEM((nsplit, rows, 1), F32), pltpu.VMEM((nsplit, rows, 1), F32),
                        pltpu.VMEM((nsplit, rows, KV_LORA), F32)],
        compiler_params=_cparams(("parallel", "arbitrary")), name="mla_prompt",
    )(qa.reshape(b, s, wa), qr.reshape(b, s, wr), ckvb.reshape(b, s, KV_LORA), krp.reshape(b, s, LANE), wuv)


def _outproj_kernel(x_ref, on_ref, om_ref, ga_ref, gb_ref, wpn_ref, wpm_ref, wo_ref, gf_ref, x1_ref, h2_ref):
    a = _dot(on_ref[...], wpn_ref[...])
    b = _dot(om_ref[...], wpm_ref[...])
    mix = jax.nn.sigmoid(ga_ref[...]) * a + jax.nn.sigmoid(gb_ref[...]) * b
    x1 = x_ref[...] + _dot(mix.astype(BF16), wo_ref[...])
    x1_ref[...] = x1
    ms = jnp.mean(x1 * x1, axis=-1, keepdims=True)
    h2_ref[...] = (x1 * lax.rsqrt(ms + EPS) * gf_ref[...]).astype(BF16)


def _outproj(x2d, o_nsa, o_mla, z, zoff, prm):
    n, d = x2d.shape
    tm = min(256, n)
    row = lambda w, c: pl.BlockSpec((tm, w), lambda i, c=c: (i, c))
    once = lambda a: pl.BlockSpec(a.shape, lambda i, nd=a.ndim: (0,) * nd, pipeline_mode=pl.Buffered(1))
    return pl.pallas_call(
        _outproj_kernel, grid=(n // tm,),
        in_specs=[row(d, 0), row(o_nsa.shape[1], 0), row(o_mla.shape[1], 0), row(d, zoff["ga"] // d),
                  row(d, zoff["gb"] // d), once(prm["wpn"]), once(prm["wpm"]), once(prm["wo"]), once(prm["gf"])],
        out_specs=[row(d, 0), row(d, 0)],
        out_shape=[jax.ShapeDtypeStruct((n, d), F32), jax.ShapeDtypeStruct((n, d), BF16)],
        compiler_params=_cparams(("parallel",)), name="out_proj",
    )(x2d, o_nsa, o_mla, z, z, prm["wpn"], prm["wpm"], prm["wo"], prm["gf"])


def _ffn_kernel(h_ref, x1_ref, wg_ref, wu_ref, wd_ref, y_ref, acc_ref):
    f = pl.program_id(1)

    @pl.when(f == 0)
    def _():
        acc_ref[...] = jnp.zeros(acc_ref.shape, F32)

    h = h_ref[...]
    t = jax.nn.silu(_dot(h, wg_ref[...])) * _dot(h, wu_ref[...])
    acc_ref[...] += _dot(t.astype(BF16), wd_ref[...])

    @pl.when(f == pl.num_programs(1) - 1)
    def _():
        y_ref[...] = x1_ref[...] + acc_ref[...]


def _ffn(h2, x1, prm):
    n, d = x1.shape
    dff = prm["wg"].shape[1]
    tm = min(512, n)
    tf = 512 if dff % 512 == 0 else 256
    return pl.pallas_call(
        _ffn_kernel, grid=(n // tm, dff // tf),
        in_specs=[pl.BlockSpec((tm, d), lambda i, f: (i, 0)), pl.BlockSpec((tm, d), lambda i, f: (i, 0)),
                  pl.BlockSpec((d, tf), lambda i, f: (0, f)), pl.BlockSpec((d, tf), lambda i, f: (0, f)),
                  pl.BlockSpec((tf, d), lambda i, f: (f, 0))],
        out_specs=pl.BlockSpec((tm, d), lambda i, f: (i, 0)),
        out_shape=jax.ShapeDtypeStruct((n, d), F32),
        scratch_shapes=[pltpu.VMEM((tm, d), F32)],
        compiler_params=_cparams(("parallel", "arbitrary")), name="ffn",
    )(h2, x1, prm["wg"], prm["wu"], prm["wd"])


def _page_specs(pp, rows, cols):
    return [pl.BlockSpec((1, rows, cols), functools.partial(lambda b, s, pt, j: (pt[b, s * pp + j], 0, 0), j=j))
            for j in range(pp)]


def _n_chunks(pp):
    return 1


def _feature_major_pages(cache):
    n_pool = cache.shape[0]
    return jnp.swapaxes(cache.reshape(n_pool, PAGE_SIZE, -1), 1, 2)


def _group_queries(q, lanes):
    t = q.shape[0]
    blocks = []
    for g in range(NSA_KV_HEADS):
        qg = jnp.concatenate([q[:, (g * NSA_GROUP + h) * HEAD_DIM:(g * NSA_GROUP + h + 1) * HEAD_DIM]
                              for h in range(NSA_GROUP)], axis=0)
        parts = []
        if g > 0:
            parts.append(jnp.zeros((NSA_GROUP * t, g * HEAD_DIM), q.dtype))
        parts.append(qg)
        parts.append(jnp.zeros((NSA_GROUP * t, lanes - (g + 1) * HEAD_DIM), q.dtype))
        blocks.append(jnp.concatenate(parts, axis=1))
    return jnp.concatenate(blocks, axis=0)


def _smp_cmp_kernel(pt_ref, *refs, pp, past_chunks, nsb):
    pages = refs[:pp]
    (new_ref, q_ref, wlo_ref, whi_ref, wt_ref, gkc_ref, e64_ref, e64t_ref, bias_ref, a_ref,
     ocmp_ref, score_ref, lo_ref, hi_ref) = refs[pp:]
    s = pl.program_id(1)
    nr = lo_ref.shape[0]
    t = q_ref.shape[0]
    wlo, whi = wlo_ref[...], whi_ref[...]
    cpp = PAGE_SIZE // CMP_STRIDE
    half = KV_W // 2

    @pl.when(s == 0)
    def _():
        lo_ref[past_chunks:, :] = jnp.zeros((nr - past_chunks, KV_W), F32)
        hi_ref[past_chunks:, :] = jnp.zeros((nr - past_chunks, KV_W), F32)

    x = jnp.concatenate([pages[j][0] for j in range(pp)], axis=0).astype(BF16)
    r = _dot_nt(wt_ref[...].astype(BF16), x)
    for j in range(pp):
        c0 = j * KV_W
        r0 = pl.multiple_of((s * pp + j) * cpp, cpp)
        lo_ref[pl.ds(r0, cpp), :] = jnp.concatenate([r[0:cpp, c0:c0 + half],
                                                     r[2 * cpp:3 * cpp, c0 + half:c0 + KV_W]], axis=1)
        hi_ref[pl.ds(r0, cpp), :] = jnp.concatenate([r[cpp:2 * cpp, c0:c0 + half],
                                                     r[3 * cpp:4 * cpp, c0 + half:c0 + KV_W]], axis=1)

    @pl.when(s == pl.num_programs(1) - 1)
    def _():
        new = jnp.concatenate([new_ref[0], jnp.zeros((CMP_STRIDE - t, KV_W), F32)], axis=0)
        lo_n, hi_n = _chunk_partials(new, wlo, whi)
        pad = jnp.zeros((cpp - 1, KV_W), F32)
        lo_ref[past_chunks:past_chunks + cpp, :] = jnp.concatenate([lo_n, pad], axis=0)
        hi_ref[past_chunks:past_chunks + cpp, :] = jnp.concatenate([hi_n, pad], axis=0)
        groups = _blocks_to_groups(lo_ref[...], hi_ref[...], gkc_ref[...], e64_ref[...], e64t_ref[...])
        q = q_ref[...]
        nbp = a_ref.shape[1]
        past = past_chunks * CMP_STRIDE
        c_id = lax.broadcasted_iota(jnp.int32, (t, nr), 1)
        t_id = lax.broadcasted_iota(jnp.int32, (t, nr), 0)
        mask_c = (past + t_id - CMP_STRIDE * c_id - (CMP_BLOCK - 1)) >= 0
        j_id = lax.broadcasted_iota(jnp.int32, (t, nbp), 1)
        qpos = past + lax.broadcasted_iota(jnp.int32, (t, nbp), 0)
        cur = qpos // SEL_BLOCK
        forced = (j_id == 0) | (j_id == cur) | (j_id == cur - 1)
        valid = j_id * SEL_BLOCK <= qpos
        for g in range(NSA_KV_HEADS):
            h0 = g * NSA_GROUP
            qg = jnp.concatenate([q[:, (h0 + h) * HEAD_DIM:(h0 + h + 1) * HEAD_DIM] for h in range(NSA_GROUP)], axis=0)
            qp = jnp.concatenate([qg, jnp.zeros_like(qg)], axis=1)
            kv = groups[g].astype(BF16)
            sc = _dot_nt(qp, kv).reshape(NSA_GROUP, t, nr) + bias_ref[h0:h0 + NSA_GROUP]
            sc = jnp.where(mask_c[None], sc, NEG)
            e = jnp.exp(sc - jnp.max(sc, axis=-1, keepdims=True))
            p = jnp.where(mask_c[None], e / jnp.sum(e, axis=-1, keepdims=True), 0.0)
            ocmp_ref[0, g * NSA_GROUP * t:(g + 1) * NSA_GROUP * t, :] = _dot(
                p.reshape(NSA_GROUP * t, nr).astype(BF16), kv)
            score = _dot_sel(jnp.sum(p, axis=0), a_ref[...])
            score = jnp.where(valid, jnp.where(forced, score + FORCE_BONUS, score), NEG)
            score_ref[0, g * t:(g + 1) * t, :] = jnp.where(j_id < nsb, score, PAD_SCORE)


def _smp_cmp(page_table, cache_cmp, kvc_new, qn, bias_c, a_mat, prm, pp):
    bd, n_pages = page_table.shape
    t = kvc_new.shape[1]
    past_chunks = n_pages * PAGE_SIZE // CMP_STRIDE
    nr = a_mat.shape[0]
    nbp = a_mat.shape[1]
    nsb = (past_chunks + SEL_BLOCK // CMP_STRIDE) // (SEL_BLOCK // CMP_STRIDE)
    rows = NSA_HEADS * t
    full = lambda a: pl.BlockSpec(a.shape, lambda b, s, pt, nd=a.ndim: (0,) * nd)
    consts = [prm["wlo"], prm["whi"], prm["wt"], prm["gkc"], prm["e64"], prm["e64t"], bias_c, a_mat]
    gs = pltpu.PrefetchScalarGridSpec(
        num_scalar_prefetch=1, grid=(bd, n_pages // pp),
        in_specs=_page_specs(pp, KV_W, PAGE_SIZE) + [pl.BlockSpec((1, t, KV_W), lambda b, s, pt: (b, 0, 0)),
                                          pl.BlockSpec((t, qn.shape[1]), lambda b, s, pt: (b, 0))]
        + [full(a) for a in consts],
        out_specs=[pl.BlockSpec((1, rows, LANE), lambda b, s, pt: (b, 0, 0)),
                   pl.BlockSpec((1, NSA_KV_HEADS * t, nbp), lambda b, s, pt: (b, 0, 0))],
        scratch_shapes=[pltpu.VMEM((nr, KV_W), F32), pltpu.VMEM((nr, KV_W), F32)])
    return pl.pallas_call(
        functools.partial(_smp_cmp_kernel, pp=pp, past_chunks=past_chunks, nsb=nsb), grid_spec=gs,
        out_shape=[jax.ShapeDtypeStruct((bd, rows, LANE), F32),
                   jax.ShapeDtypeStruct((bd, NSA_KV_HEADS * t, nbp), F32)],
        compiler_params=_cparams(("parallel", "arbitrary")), name="sample_cmp",
    )(page_table, *([cache_cmp] * pp), kvc_new, qn, *consts)


def _topk_mask_kernel(score_ref, exp_ref, o_ref, sel_ref, sc_ref, *, nsb):
    @pl.when(pl.program_id(0) == 0)
    def _():
        nbp = score_ref.shape[1]
        nchunk, nr, cw = sc_ref.shape
        sct = score_ref[...].T
        for c in range(nchunk):
            sc_ref[c] = sct[:nr, c * cw:(c + 1) * cw]
        rz = _round_up(-(-nr // 2), 8)
        col_parts = []
        for c in range(nchunk):
            row_parts = []
            for r0 in range(0, nr, rz):
                r1 = min(r0 + rz, nr)
                piece = sc_ref[c, r0:r1, :]
                j_id = r0 + lax.broadcasted_iota(jnp.int32, piece.shape, 0)

                def rank_body(r, rank, piece=piece, j_id=j_id, c=c):
                    sr = sc_ref[c, pl.ds(r, 1), :]
                    ahead = (sr > piece) | ((sr == piece) & (r < j_id))
                    return rank + ahead.astype(jnp.int32)

                rank = lax.fori_loop(0, nsb, rank_body, jnp.zeros(piece.shape, jnp.int32))
                row_parts.append(((rank < min(N_SEL, nsb)) & (j_id < nsb)).astype(F32))
            col_parts.append(jnp.concatenate(row_parts, axis=0))
        sel = jnp.concatenate(col_parts, axis=1)
        if nbp > nr:
            sel = jnp.concatenate([sel, jnp.zeros((nbp - nr, sel.shape[1]), F32)], axis=0)
        sel_ref[...] = sel.T.astype(BF16)

    rows = sel_ref.shape[0]
    rc = min(512, rows)
    for r in range(0, rows, rc):
        o_ref[r:r + rc, :] = _dot(sel_ref[r:r + rc, :], exp_ref[...]).astype(BF16)


def _topk_mask(score2d, expand, nsb, tile):
    rows, nbp = score2d.shape
    keys = expand.shape[1]
    return pl.pallas_call(
        functools.partial(_topk_mask_kernel, nsb=nsb), grid=(keys // tile,),
        in_specs=[pl.BlockSpec((rows, nbp), lambda k: (0, 0)), pl.BlockSpec((nbp, tile), lambda k: (0, k))],
        out_specs=pl.BlockSpec((rows, tile), lambda k: (0, k)),
        out_shape=jax.ShapeDtypeStruct((rows, keys), BF16),
        scratch_shapes=[pltpu.VMEM((rows, nbp), BF16),
                        pltpu.VMEM((rows // min(LANE, rows), _round_up(nsb, 8), min(LANE, rows)), F32)],
        compiler_params=_cparams(("arbitrary",)), name="sample_topk_mask",
    )(score2d, expand)


def _rows_pad(a, rows):
    return jnp.concatenate([a, jnp.zeros((rows - a.shape[0], a.shape[1]), a.dtype)], axis=0)


def _smp_sel_kernel(pt_ref, *refs, pp):
    pages = refs[:pp]
    (mask_ref, maskt_ref, q_ref, new_ref, blast_ref, bnew_ref, o_ref, qp_ref, m_ref, l_ref, acc_ref) = refs[pp:]
    s = pl.program_id(1)
    t = q_ref.shape[0]
    rows = NSA_HEADS * t
    last = pl.num_programs(1) - 1

    @pl.when(s == 0)
    def _():
        qp_ref[...] = _group_queries(q_ref[...], KV_W)
        _softmax_init(m_ref, l_ref, acc_ref)

    qp = qp_ref[...]

    def expand_mask(mk, nk):
        m4 = jnp.broadcast_to(mk.astype(F32).reshape(NSA_KV_HEADS, 1, t, nk), (NSA_KV_HEADS, NSA_GROUP, t, nk))
        return m4.reshape(rows, nk) > 0.5

    nl = blast_ref.shape[1]
    nch = _n_chunks(pp)
    cp = pp // nch
    nk = cp * PAGE_SIZE

    def chunk(c, bias):
        keys = jnp.concatenate([pages[c * cp + j][0] for j in range(cp)], axis=1).astype(BF16)
        mask = expand_mask(mask_ref[0, :, c * nk:(c + 1) * nk], nk)
        raw = _dot(qp, keys)
        st = jnp.where(mask, raw if bias is None else raw + bias, NEG)
        _softmax_update(st, keys, m_ref, l_ref, acc_ref, v_feature_major=True)

    for c in range(nch - 1):
        chunk(c, None)

    @pl.when(s != last)
    def _():
        chunk(nch - 1, None)

    @pl.when(s == last)
    def _():
        far = [jnp.zeros((rows, nk - nl), F32)] if nk > nl else []
        chunk(nch - 1, jnp.concatenate(far + [blast_ref[...]], axis=1))
        kn = _rows_pad(new_ref[0], LANE).astype(BF16)
        j_id = lax.broadcasted_iota(jnp.int32, (rows, LANE), 1)
        t_id = lax.broadcasted_iota(jnp.int32, (rows, LANE), 0) % t
        mt = expand_mask(maskt_ref[0][:, :LANE], LANE) & (j_id <= t_id) & (j_id < t)
        st = jnp.where(mt, _dot_nt(qp, kn) + bnew_ref[...], NEG)
        _softmax_update(st, kn, m_ref, l_ref, acc_ref)
        o = acc_ref[...] / l_ref[...]
        half = KV_W // 2
        hr = NSA_GROUP * t
        o_ref[0] = jnp.concatenate([o[g * hr:(g + 1) * hr, half + g * HEAD_DIM:half + (g + 1) * HEAD_DIM]
                                    for g in range(NSA_KV_HEADS)], axis=0)


def _smp_sel(page_table, cache_slc, mask3, qn, kvs_new, blast, bnew, pp):
    bd, n_pages = page_table.shape
    t = kvs_new.shape[1]
    rows = NSA_HEADS * t
    nk = pp * PAGE_SIZE
    nsteps = n_pages // pp
    full = lambda a: pl.BlockSpec(a.shape, lambda b, s, pt, nd=a.ndim: (0,) * nd)
    gs = pltpu.PrefetchScalarGridSpec(
        num_scalar_prefetch=1, grid=(bd, nsteps),
        in_specs=_page_specs(pp, KV_W, PAGE_SIZE) + [
            pl.BlockSpec((1, NSA_KV_HEADS * t, nk), lambda b, s, pt: (b, 0, s)),
            pl.BlockSpec((1, NSA_KV_HEADS * t, nk), lambda b, s, pt: (b, 0, nsteps)),
            pl.BlockSpec((t, qn.shape[1]), lambda b, s, pt: (b, 0)),
            pl.BlockSpec((1, t, KV_W), lambda b, s, pt: (b, 0, 0)),
            full(blast), full(bnew)],
        out_specs=pl.BlockSpec((1, rows, HEAD_DIM), lambda b, s, pt: (b, 0, 0)),
        scratch_shapes=[pltpu.VMEM((rows, KV_W), BF16), pltpu.VMEM((rows, 1), F32), pltpu.VMEM((rows, 1), F32),
                        pltpu.VMEM((rows, KV_W), F32)])
    return pl.pallas_call(
        functools.partial(_smp_sel_kernel, pp=pp), grid_spec=gs,
        out_shape=jax.ShapeDtypeStruct((bd, rows, HEAD_DIM), F32),
        compiler_params=_cparams(("parallel", "arbitrary")), name="sample_sel",
    )(page_table, *([cache_slc] * pp), mask3, mask3, qn, kvs_new, blast, bnew)


def _smp_mla_kernel(pt_ref, *refs, pp):
    cpages = refs[:pp]
    rpages = refs[pp:2 * pp]
    (qa_ref, qr_ref, cnew_ref, rnew_ref, wuv_ref, o_ref, qa_s, qr_s, m_ref, l_ref, acc_ref) = refs[2 * pp:]
    s = pl.program_id(1)
    t = qa_ref.shape[0]
    rows = MLA_HEADS * t

    @pl.when(s == 0)
    def _():
        qa_s[...] = jnp.concatenate([qa_ref[:, h * KV_LORA:(h + 1) * KV_LORA] for h in range(MLA_HEADS)], axis=0)
        qr_s[...] = jnp.concatenate([qr_ref[:, h * LANE:(h + 1) * LANE] for h in range(MLA_HEADS)], axis=0)
        _softmax_init(m_ref, l_ref, acc_ref)

    qa, qr = qa_s[...], qr_s[...]
    nch = _n_chunks(pp)
    cp = pp // nch
    for c in range(nch):
        js = range(c * cp, (c + 1) * cp)
        ck = jnp.concatenate([cpages[j][0] for j in js], axis=0).astype(BF16)
        kr = jnp.concatenate([rpages[j][0] for j in js], axis=1).astype(BF16)
        st = _dot_nt(qa, ck) + _dot(qr[:, :QK_ROPE], kr)
        _softmax_update(st, ck, m_ref, l_ref, acc_ref)

    @pl.when(s == pl.num_programs(1) - 1)
    def _():
        cn = _rows_pad(cnew_ref[0], LANE)
        rn = _rows_pad(rnew_ref[0], LANE)
        j_id = lax.broadcasted_iota(jnp.int32, (rows, LANE), 1)
        t_id = lax.broadcasted_iota(jnp.int32, (rows, LANE), 0) % t
        mt = (j_id <= t_id) & (j_id < t)
        sn = jnp.where(mt, _dot_nt(qa, cn) + _dot_nt(qr, rn), NEG)
        _softmax_update(sn, cn, m_ref, l_ref, acc_ref)
        o_lat = (acc_ref[...] / l_ref[...]).astype(BF16)
        o_ref[...] = jnp.concatenate([_dot(o_lat[h * t:(h + 1) * t], wuv_ref[h]) for h in range(MLA_HEADS)],
                                     axis=1).astype(BF16)


def _smp_mla(page_table, cache_ckv, cache_kr, qa, qr, ckvb_new, krp_new, wuv, pp):
    bd, n_pages = page_table.shape
    t = ckvb_new.shape[1]
    rows = MLA_HEADS * t
    gs = pltpu.PrefetchScalarGridSpec(
        num_scalar_prefetch=1, grid=(bd, n_pages // pp),
        in_specs=_page_specs(pp, PAGE_SIZE, KV_LORA) + _page_specs(pp, QK_ROPE, PAGE_SIZE) + [
            pl.BlockSpec((t, qa.shape[1]), lambda b, s, pt: (b, 0)),
            pl.BlockSpec((t, qr.shape[1]), lambda b, s, pt: (b, 0)),
            pl.BlockSpec((1, t, KV_LORA), lambda b, s, pt: (b, 0, 0)),
            pl.BlockSpec((1, t, LANE), lambda b, s, pt: (b, 0, 0)),
            pl.BlockSpec(wuv.shape, lambda b, s, pt: (0, 0, 0))],
        out_specs=pl.BlockSpec((t, MLA_HEADS * V_DIM), lambda b, s, pt: (b, 0)),
        scratch_shapes=[pltpu.VMEM((rows, KV_LORA), BF16), pltpu.VMEM((rows, LANE), BF16),
                        pltpu.VMEM((rows, 1), F32), pltpu.VMEM((rows, 1), F32), pltpu.VMEM((rows, KV_LORA), F32)])
    return pl.pallas_call(
        functools.partial(_smp_mla_kernel, pp=pp), grid_spec=gs,
        out_shape=jax.ShapeDtypeStruct((bd * t, MLA_HEADS * V_DIM), BF16),
        compiler_params=_cparams(("parallel", "arbitrary")), name="sample_mla",
    )(page_table, *([cache_ckv] * pp), *([cache_kr] * pp), qa, qr, ckvb_new, krp_new, wuv)


def _smp_win_kernel(state_ref, new_ref, q_ref, misc_ref, ocmp_ref, osel_ref, bias_ref, o_ref, win_ref):
    t = q_ref.shape[0]
    wl = state_ref.shape[1]
    rows = NSA_HEADS * t
    state = state_ref[0]
    new = new_ref[0]
    win_ref[0, :wl - t, :] = state[t:, :]
    win_ref[0, wl - t:, :] = new
    nkp = bias_ref.shape[1]
    ctx = jnp.concatenate([state, new, jnp.zeros((nkp - wl - t, KV_W), F32)], axis=0).astype(BF16)
    qp = _group_queries(q_ref[...], KV_W)
    j_id = lax.broadcasted_iota(jnp.int32, (rows, nkp), 1)
    t_id = lax.broadcasted_iota(jnp.int32, (rows, nkp), 0) % t
    dist = wl + t_id - j_id
    mask = (dist >= 0) & (dist < WINDOW) & (j_id < wl + t)
    s = jnp.where(mask, _dot_nt(qp, ctx) + bias_ref[...], NEG)
    e = jnp.exp(s - jnp.max(s, axis=-1, keepdims=True))
    p = jnp.where(mask, e / jnp.sum(e, axis=-1, keepdims=True), 0.0)
    o_win = _dot(p.astype(BF16), ctx)
    misc = misc_ref[...]
    ocmp = ocmp_ref[0]
    osel = osel_ref[0]
    half = KV_W // 2
    pieces = []
    for hh in range(NSA_HEADS):
        g = hh // NSA_GROUP
        rs = slice(hh * t, (hh + 1) * t)
        gc = misc[:, MISC_GATE_OFF + hh:MISC_GATE_OFF + hh + 1]
        gsel = misc[:, MISC_GATE_OFF + NSA_HEADS + hh:MISC_GATE_OFF + NSA_HEADS + hh + 1]
        gw = misc[:, MISC_GATE_OFF + 2 * NSA_HEADS + hh:MISC_GATE_OFF + 2 * NSA_HEADS + hh + 1]
        pieces.append(gc * ocmp[rs, HEAD_DIM:] + gsel * osel[rs, :]
                      + gw * o_win[rs, half + g * HEAD_DIM:half + (g + 1) * HEAD_DIM])
    o_ref[...] = jnp.concatenate(pieces, axis=1).astype(BF16)


def _smp_win(state, kvw_new, qn, misc, ocmp, osel, bias_w):
    bd, wl, _ = state.shape
    t = kvw_new.shape[1]
    rows = NSA_HEADS * t
    nq = NSA_HEADS * HEAD_DIM
    return pl.pallas_call(
        _smp_win_kernel, grid=(bd,),
        in_specs=[pl.BlockSpec((1, wl, KV_W), lambda b: (b, 0, 0)), pl.BlockSpec((1, t, KV_W), lambda b: (b, 0, 0)),
                  pl.BlockSpec((t, nq), lambda b: (b, 0)), pl.BlockSpec((t, LANE), lambda b: (b, 0)),
                  pl.BlockSpec((1, rows, LANE), lambda b: (b, 0, 0)), pl.BlockSpec((1, rows, HEAD_DIM), lambda b: (b, 0, 0)),
                  pl.BlockSpec(bias_w.shape, lambda b: (0, 0))],
        out_specs=[pl.BlockSpec((t, nq), lambda b: (b, 0)), pl.BlockSpec((1, wl, KV_W), lambda b: (b, 0, 0))],
        out_shape=[jax.ShapeDtypeStruct((bd * t, nq), BF16), jax.ShapeDtypeStruct((bd, wl, KV_W), F32)],
        compiler_params=_cparams(("parallel",)), name="sample_window_merge",
    )(state, kvw_new, qn, misc, ocmp, osel, bias_w)


def _prep_params(d, attn_norm, w_in, g_q_nsa, g_k_cmp, g_k_slc, g_k_win, phi_cmp, g_cq, w_uq, g_q_mla, g_ckv,
                 g_krope, w_uk, w_uv, w_proj_nsa, w_proj_mla, w_out, ffn_norm, w_gate, w_up, w_down):
    nq = NSA_HEADS * HEAD_DIM
    widths = (nq, KV_W, KV_W, KV_W, GATE_N_W, Q_LORA, KV_LORA, QK_ROPE, d, d)
    names = ("q", "kvc", "kvs", "kvw", "gn", "cq", "ckv", "kr", "ga", "gb")
    splits = np.cumsum(widths)[:-1]
    seg = dict(zip(names, jnp.split(w_in, splits, axis=1)))
    zoff = _z_layout(d)
    tn = 1024 if d >= 1024 else 256
    zw = _round_up(zoff["end"], tn)
    zc = lambda w: jnp.zeros((d, w), w_in.dtype)
    cols = [seg["q"], seg["cq"], seg["kr"], seg["gn"], zc(zoff["ga"] - zoff["misc"] - QK_ROPE - GATE_N_W),
            seg["ga"], seg["gb"], zc(zoff["kv"] - zoff["gb"] - d), seg["kvc"], seg["kvs"], seg["kvw"], seg["ckv"],
            zc(zw - zoff["end"])]
    prm = {"w_cat": jnp.concatenate(cols, axis=1).astype(BF16), "tn": tn}
    prm["g_attn"] = attn_norm.reshape(1, d)
    prm["gq"] = jnp.tile(g_q_nsa, NSA_HEADS).reshape(1, nq)
    prm["gks"] = jnp.tile(g_k_slc, NSA_KV_HEADS).reshape(1, LANE)
    prm["gkw"] = jnp.tile(g_k_win, NSA_KV_HEADS).reshape(1, LANE)
    prm["gkc"] = jnp.tile(g_k_cmp, NSA_KV_HEADS).reshape(1, LANE)
    prm["gcq"] = g_cq.reshape(1, Q_LORA)
    wq = w_uq.reshape(Q_LORA, MLA_HEADS, QK_NOPE + QK_ROPE)
    w_rope = jnp.pad(wq[:, :, QK_NOPE:], ((0, 0), (0, 0), (0, LANE - QK_ROPE)))
    prm["wuq"] = jnp.concatenate([wq[:, :, :QK_NOPE].reshape(Q_LORA, MLA_HEADS * QK_NOPE),
                                  w_rope.reshape(Q_LORA, MLA_HEADS * LANE)], axis=1).astype(BF16)
    prm["gmn"] = jnp.tile(g_q_mla[:QK_NOPE], MLA_HEADS).reshape(1, MLA_HEADS * QK_NOPE)
    prm["gmr"] = jnp.tile(jnp.pad(g_q_mla[QK_NOPE:], (0, LANE - QK_ROPE)), MLA_HEADS).reshape(1, MLA_HEADS * LANE)
    prm["gckv"] = g_ckv.reshape(1, KV_LORA)
    prm["gkr"] = jnp.pad(g_krope, (0, LANE - QK_ROPE)).reshape(1, LANE)
    prm["wuk"] = w_uk.astype(BF16)
    prm["wuv"] = w_uv.astype(BF16)
    e64 = _seg_indicator(nq, HEAD_DIM)
    e128 = _seg_indicator(MLA_HEADS * LANE, LANE)
    prm["e64"], prm["e64t"] = jnp.asarray(e64, BF16), jnp.asarray(e64.T, BF16)
    prm["e128"], prm["e128t"] = jnp.asarray(e128, BF16), jnp.asarray(e128.T, BF16)
    ph = jnp.repeat(phi_cmp, KV_W // 2, axis=0).T
    prm["wlo"], prm["whi"] = ph[:CMP_STRIDE], ph[CMP_STRIDE:]
    eye = jnp.eye(PAGE_SIZE // CMP_STRIDE, dtype=F32)
    prm["wt"] = jnp.concatenate([jnp.kron(eye, phi_cmp[e, part * CMP_STRIDE:(part + 1) * CMP_STRIDE][None, :])
                                 for e in range(2) for part in range(2)], axis=0)
    prm["wpn"], prm["wpm"], prm["wo"] = w_proj_nsa.astype(BF16), w_proj_mla.astype(BF16), w_out.astype(BF16)
    prm["gf"] = ffn_norm.reshape(1, d)
    prm["wg"], prm["wu"], prm["wd"] = w_gate.astype(BF16), w_up.astype(BF16), w_down.astype(BF16)
    return prm, zoff


def _rope_tables(pos):
    half = QK_ROPE // 2
    inv_freq = ROPE_THETA ** (-jnp.arange(half, dtype=jnp.float32) / half)
    ang = pos.astype(jnp.float32)[:, None] * inv_freq
    cos, sin = jnp.cos(ang), jnp.sin(ang)
    z = jnp.zeros((pos.shape[0], LANE - QK_ROPE), F32)
    zh = jnp.zeros_like(sin)
    return (jnp.concatenate([cos, cos, z], axis=1), jnp.concatenate([zh, sin, z], axis=1),
            jnp.concatenate([-sin, zh, z], axis=1))


def _layer(xp, xs, cache_cmp, cache_slc, cache_ckv, cache_kr, win_state, page_table, rel_bias_table, weights):
    b, s, d = xp.shape
    bd, t, _ = xs.shape
    n_pages = page_table.shape[1]
    past = n_pages * PAGE_SIZE
    n_pool = cache_cmp.shape[0]
    wl = win_state.shape[1]
    prm, zoff = _prep_params(d, *weights)
    pp = max(1, min(32, n_pages // 2))
    nsteps = n_pages // pp

    qb = Q_BLOCK
    ar = np.arange
    d_t01 = (ar(qb)[None, :, None] - ar(qb)[None, None, :]) + qb * ar(2)[:, None, None]
    bw = 32
    d_bwin = ar(qb)[:, None] - CMP_STRIDE * (ar(bw)[None, :] - (bw - qb // CMP_STRIDE)) - (CMP_BLOCK - 1)
    d_c31 = np.full((LANE,), MAX_DISTANCE)
    past_chunks = past // CMP_STRIDE
    nc_s = past_chunks + SEL_BLOCK // CMP_STRIDE
    nr_s = _round_up(nc_s, LANE)
    nsb = nc_s // (SEL_BLOCK // CMP_STRIDE)
    nbp = _round_up(nsb, LANE)
    d_cmp_s = past + ar(t)[:, None] - CMP_STRIDE * ar(nr_s)[None, :] - (CMP_BLOCK - 1)
    nl = min(2 * LANE, pp * PAGE_SIZE)
    d_last = nl + ar(t)[:, None] - ar(nl)[None, :]
    d_new = ar(t)[:, None] - ar(LANE)[None, :]
    nkp = _round_up(wl + t, LANE)
    d_win = wl + ar(t)[:, None] - ar(nkp)[None, :]
    t01, bwin, c31, bias_c, blast, bnew, bias_w = _bias_tables(
        rel_bias_table, [d_t01, d_bwin, d_c31, d_cmp_s, d_last, d_new, d_win])
    c31row = jnp.repeat(c31[:, 0], qb).reshape(1, NSA_HEADS * qb)
    t01t = t01.transpose(1, 3, 0, 2).reshape(2, qb, NSA_HEADS * qb) - c31row[None]
    bwint = bwin.transpose(2, 0, 1).reshape(bw, NSA_HEADS * qb)
    rows_s = NSA_HEADS * t
    c31r = jnp.repeat(c31[:, :1], t, axis=0)
    blast = blast.reshape(rows_s, nl) - c31r
    bnew = bnew.reshape(rows_s, LANE) - c31r
    bias_w = bias_w.reshape(rows_s, nkp)
    at = jnp.asarray(_score_matrix(s // CMP_STRIDE, s // SEL_BLOCK).T, BF16)
    a_s = np.zeros((nr_s, nbp), np.float32)
    a_s[:nc_s] = _score_matrix(nc_s, nbp)
    a_s = jnp.asarray(a_s, BF16)
    keys_pad = (nsteps + 1) * pp * PAGE_SIZE
    expand = np.zeros((nbp, keys_pad), np.float32)
    kk = ar(nsb * SEL_BLOCK)
    expand[kk // SEL_BLOCK, kk] = 1.0
    expand = jnp.asarray(expand, BF16)

    pos_p = jnp.arange(s)
    tabs_p = tuple(jnp.tile(a, (b, 1)) for a in _rope_tables(pos_p))
    zp = _proj(xp.reshape(b * s, d), prm["g_attn"], prm["w_cat"], prm["tn"])
    pp_ = _post(zp, zoff, tabs_p, prm, True)
    kvcg, kvcgt = _cmp_prompt(pp_["kvc"].reshape(b, s, KV_W), prm)
    o_nsa_p = _nsa_prompt(pp_["qnt"], pp_["misc"], kvcg, kvcgt, pp_["kvsg"], pp_["kvsgt"], pp_["kvwg"], pp_["kvwgt"],
                          t01t, bwint, c31row, at, b, s)
    o_mla_p = _mla_prompt(pp_["qa"], pp_["qr"], pp_["ckvb"], pp_["krp"], prm["wuv"], b, s)
    x1p, h2p = _outproj(xp.reshape(b * s, d), o_nsa_p.reshape(b * s, -1), o_mla_p.reshape(b * s, -1), zp, zoff, prm)
    yp = _ffn(h2p, x1p, prm).reshape(b, s, d)

    pos_s = past + jnp.arange(t)
    tabs_s = tuple(jnp.tile(a, (bd, 1)) for a in _rope_tables(pos_s))
    zs = _proj(xs.reshape(bd * t, d), prm["g_attn"], prm["w_cat"], prm["tn"])
    ps = _post(zs, zoff, tabs_s, prm, False)
    cc = _feature_major_pages(cache_cmp)
    cs = _feature_major_pages(cache_slc)
    cache_kr = _feature_major_pages(cache_kr)
    ocmp_s, score = _smp_cmp(page_table, cc, ps["kvc"].reshape(bd, t, KV_W), ps["qn"], bias_c, a_s, prm, pp)
    mask = _topk_mask(score.reshape(bd * NSA_KV_HEADS * t, nbp), expand, nsb, pp * PAGE_SIZE)
    osel_s = _smp_sel(page_table, cs, mask.reshape(bd, NSA_KV_HEADS * t, keys_pad), ps["qn"],
                      ps["kvs"].reshape(bd, t, KV_W), blast, bnew, pp)
    o_mla_s = _smp_mla(page_table, cache_ckv, cache_kr, ps["qa"], ps["qr"], ps["ckvb"].reshape(bd, t, KV_LORA),
                       ps["krp"].reshape(bd, t, LANE), prm["wuv"], pp)
    o_nsa_s, win_s = _smp_win(win_state.reshape(bd, wl, KV_W), ps["kvw"].reshape(bd, t, KV_W), ps["qn"], ps["misc"],
                              ocmp_s, osel_s, bias_w)
    x1s, h2s = _outproj(xs.reshape(bd * t, d), o_nsa_s, o_mla_s, zs, zoff, prm)
    ys = _ffn(h2s, x1s, prm).reshape(bd, t, d)

    kvh = (2, NSA_KV_HEADS, HEAD_DIM)
    wp = min(WINDOW, s)
    return (yp, ys,
            pp_["kvc"].reshape((b, s) + kvh), ps["kvc"].reshape((bd, t) + kvh),
            pp_["kvs"].reshape((b, s) + kvh), ps["kvs"].reshape((bd, t) + kvh),
            pp_["ckv"].reshape(b, s, KV_LORA), ps["ckv"].reshape(bd, t, KV_LORA),
            pp_["misc"][:, :QK_ROPE].reshape(b, s, QK_ROPE), ps["misc"][:, :QK_ROPE].reshape(bd, t, QK_ROPE),
            pp_["kvw"].reshape((b, s) + kvh)[:, s - wp:], win_s.reshape((bd, wl) + kvh))


def kernel(x_prompt, x_sample, cache_cmp_kv, cache_slc_kv, cache_mla_ckv, cache_mla_krope, state_win_kv, page_table, rel_bias_table, attn_norm, w_in, g_q_nsa, g_k_cmp, g_k_slc, g_k_win, phi_cmp, g_cq, w_uq, g_q_mla, g_ckv, g_krope, w_uk, w_uv, w_proj_nsa, w_proj_mla, w_out, ffn_norm, w_gate, w_up, w_down):
    depth = w_in.shape[0]
    xp, xs = x_prompt, x_sample
    per_layer = []
    for l in range(depth):
        weights = (attn_norm[l], w_in[l], g_q_nsa[l], g_k_cmp[l], g_k_slc[l], g_k_win[l], phi_cmp[l], g_cq[l],
                   w_uq[l], g_q_mla[l], g_ckv[l], g_krope[l], w_uk[l], w_uv[l], w_proj_nsa[l], w_proj_mla[l],
                   w_out[l], ffn_norm[l], w_gate[l], w_up[l], w_down[l])
        xp, xs, *st = _layer(xp, xs, cache_cmp_kv[l], cache_slc_kv[l], cache_mla_ckv[l], cache_mla_krope[l],
                             state_win_kv[l], page_table, rel_bias_table, weights)
        per_layer.append(st)
    stacked = [jnp.stack(v) for v in zip(*per_layer)]
    return (xp, xs, *stacked)
```
